```python
import math
import jax
import jax.numpy as jnp
from jax import lax
import numpy as np

D_MODEL = 1024
BATCH = 8
SEQ = 2048
DEPTH = 4
DEC_BATCH = 32
DEC_SEQ = 8
PAST_LEN = 8192
PAGE_SIZE = 128

N_MIXERS = 3
N_A = (DEPTH + 2) // 3
N_B = (DEPTH + 1) // 3
N_C = DEPTH // 3

DIL_WINDOWS = (128, 512, 2048)
DIL_RATES = (1, 4, 16)
N_GROUPS = 3
A_SPAN = 128
A_HEADS = 4
A_HEAD_DIM = 128
A_BLOCK = 128

HG_HEADS = 8
HG_DK = 128
HG_DV = 128
HG_WIDTH = HG_HEADS * HG_DK
HG_CHUNK = 32

RW_HEADS = 16
RW_HEAD_DIM = 64
RW_LORA_W = 64
RW_LORA_A = 64
RW_LORA_G = 160
RW_GN_EPS = 64e-5

N_MEM = 256
XA_HEADS = 4
XA_HEAD_DIM = D_MODEL // XA_HEADS

D_FF = 2816
CONV_W = 3
RMS_EPS = 1e-6

kernel_name = 'hybrid_dilated_hgrn2_rwkv7_step'


def rmsnorm(x, g):
    xf = x.astype(jnp.float32)
    y = xf * lax.rsqrt(jnp.mean(xf * xf, axis=-1, keepdims=True) + RMS_EPS)
    return y.astype(x.dtype) * g


def dilated_attn_prompt(q, k, v, dil):
    B, S, H, E = q.shape
    n = S // dil
    blk = math.gcd(A_BLOCK, n)
    nb = n // blk

    def by_residue(a):
        return a.reshape(B, n, dil, H, E).transpose(0, 2, 1, 3, 4)

    qr = by_residue(q).reshape(B, dil, nb, blk, H, E)
    pad = ((0, 0), (0, 0), (A_SPAN, 0), (0, 0), (0, 0))
    kp = jnp.pad(by_residue(k), pad)
    vp = jnp.pad(by_residue(v), pad)
    kidx = jnp.arange(nb)[:, None] * blk + jnp.arange(A_SPAN + blk)[None, :]
    kb = kp[:, :, kidx]
    vb = vp[:, :, kidx]
    s = jnp.einsum('brnqhe,brnkhe->brnhqk', qr, kb, preferred_element_type=jnp.float32) * (A_HEAD_DIM ** -0.5)
    rel = (jnp.arange(blk)[:, None] + A_SPAN) - jnp.arange(A_SPAN + blk)[None, :]
    valid = ((rel >= 0) & (rel <= A_SPAN))[None] & (kidx >= A_SPAN)[:, None, :]
    s = jnp.where(valid[None, None, :, None], s, -jnp.inf)
    lse = jax.nn.logsumexp(s, axis=-1)
    p = jnp.exp(s - lse[..., None])
    o = jnp.einsum('brnhqk,brnkhe->brnqhe', p, vb.astype(jnp.float32))
    o = o.reshape(B, dil, n, H, E).transpose(0, 2, 1, 3, 4).reshape(B, S, H, E)
    lse = lse.transpose(0, 1, 2, 4, 3).reshape(B, dil, n, H).transpose(0, 2, 1, 3).reshape(B, S, H)
    return o, lse


def dilated_attn_sample(q, k, v, L, dil):
    T = q.shape[1]
    kidx = (L + jnp.arange(T))[:, None] - dil * jnp.arange(A_SPAN + 1)[None, :]
    valid = kidx >= 0
    kidx = jnp.maximum(kidx, 0)
    kg = k[:, kidx]
    vg = v[:, kidx]
    s = jnp.einsum('bqhe,bqkhe->bhqk', q, kg, preferred_element_type=jnp.float32) * (A_HEAD_DIM ** -0.5)
    s = jnp.where(valid[None, None], s, -jnp.inf)
    lse = jax.nn.logsumexp(s, axis=-1)
    p = jnp.exp(s - lse[..., None])
    o = jnp.einsum('bhqk,bqkhe->bqhe', p, vg.astype(jnp.float32))
    return o, lse.transpose(0, 2, 1)


def mixer_dilated(h, w_qkv, q_gain, k_gain, w_o, bufs):
    B, T, _ = h.shape
    qkv = (h @ w_qkv).reshape(B, T, 3, N_GROUPS, A_HEADS, A_HEAD_DIM)
    q = rmsnorm(qkv[:, :, 0], q_gain[:, None])
    k = rmsnorm(qkv[:, :, 1], k_gain[:, None])
    v = qkv[:, :, 2]
    outs, lses, new_bufs = [], [], []
    for gi in range(N_GROUPS):
        win, dil = DIL_WINDOWS[gi], DIL_RATES[gi]
        kv_new = jnp.stack([k[:, :, gi], v[:, :, gi]], axis=2)
        if bufs is None:
            o, lse = dilated_attn_prompt(q[:, :, gi], k[:, :, gi], v[:, :, gi], dil)
            new_bufs.append(kv_new[:, -min(win, T):])
        else:
            L = bufs[gi].shape[1]
            kv_all = jnp.concatenate([bufs[gi].astype(kv_new.dtype), kv_new], axis=1)
            o, lse = dilated_attn_sample(q[:, :, gi], kv_all[:, :, 0], kv_all[:, :, 1], L, dil)
            new_bufs.append(kv_all[:, T:])
        outs.append(o)
        lses.append(lse)
    wgt = jax.nn.softmax(jnp.stack(lses, axis=0), axis=0)[..., None]
    o = jnp.sum(wgt * jnp.stack(outs, axis=0), axis=0)
    return o.reshape(B, T, A_HEADS * A_HEAD_DIM).astype(h.dtype) @ w_o, new_bufs


def gla_chunked(q, k, v, log_f, S0):
    B, T, H, K = q.shape
    C = math.gcd(HG_CHUNK, T)
    n = T // C
    mask = jnp.tril(jnp.ones((C, C), dtype=bool))[None, :, :, None, None]

    def blocks(a):
        return a.reshape(B, n, C, H, a.shape[-1]).swapaxes(0, 1)

    def step(S, xs):
        qc, kc, vc, gc = xs
        b = jnp.cumsum(gc, axis=1)
        o_inter = jnp.einsum('bchk,bhkv->bchv', qc * jnp.exp(b), S)
        diff = b[:, :, None] - b[:, None, :]
        decay = jnp.where(mask, jnp.exp(jnp.where(mask, diff, 0.0)), 0.0)
        A = jnp.einsum('bihk,bjhk,bijhk->bhij', qc, kc, decay)
        o_intra = jnp.einsum('bhij,bjhv->bihv', A, vc)
        bC = b[:, -1]
        S_new = jnp.exp(bC)[..., None] * S + jnp.einsum('bjhk,bjhv->bhkv', kc * jnp.exp(bC[:, None] - b), vc)
        return S_new, o_inter + o_intra

    S_T, o = lax.scan(step, S0.astype(jnp.float32), (blocks(q), blocks(k), blocks(v), blocks(log_f)))
    return o.swapaxes(0, 1).reshape(B, T, H, v.shape[-1]), S_T


def mixer_hgrn(h, w_in, lb, out_gain, w_o, S0):
    B, T, _ = h.shape
    q, z, i, g = jnp.split(h @ w_in, 4, axis=-1)
    zf = z.astype(jnp.float32)
    log_f = jnp.logaddexp(jnp.log(lb), jnp.log1p(-lb) + jax.nn.log_sigmoid(zf))
    kg = (1.0 - lb) * jax.nn.sigmoid(-zf)

    def heads(t, e):
        return t.astype(jnp.float32).reshape(B, T, HG_HEADS, e)

    o, S_T = gla_chunked(heads(jax.nn.silu(q), HG_DK), heads(kg, HG_DK), heads(i, HG_DV), heads(log_f, HG_DK), S0)
    o = rmsnorm(o, out_gain) * heads(jax.nn.silu(g), HG_DV)
    return o.reshape(B, T, HG_HEADS * HG_DV).astype(h.dtype) @ w_o, S_T


def mixer_rwkv(h, shift, S0, w, j):
    B, T, D = h.shape
    f32 = jnp.float32
    prev = jnp.concatenate([shift[:, None].astype(h.dtype), h[:, :-1]], axis=1)
    xs = h[:, :, None] + (prev - h)[:, :, None] * w['rw_mu'][j]
    rkv = jnp.einsum('btjd,jde->btje', xs[:, :, :3], w['rw_w_rkv'][j])
    r, k, v = rkv[:, :, 0], rkv[:, :, 1], rkv[:, :, 2]
    xw, xa, xg = xs[:, :, 3], xs[:, :, 4], xs[:, :, 5]
    wlog = -jax.nn.softplus(-(w['rw_w0'][j] + jnp.tanh(xw @ w['rw_w1'][j]) @ w['rw_w2'][j])) - 0.5
    decay = jnp.exp(-jnp.exp(wlog.astype(f32)))
    a = jax.nn.sigmoid(w['rw_a0'][j] + (xa @ w['rw_a1'][j]) @ w['rw_a2'][j])
    g = jax.nn.sigmoid(xg @ w['rw_g1'][j]) @ w['rw_g2'][j]
    kk = k * w['rw_k_k'][j]
    k = k * (1.0 + (a - 1.0) * w['rw_k_a'][j])

    def heads(t):
        return t.astype(f32).reshape(B, T, RW_HEADS, RW_HEAD_DIM)

    r, k, v, a, decay, kk = heads(r), heads(k), heads(v), heads(a), heads(decay), heads(kk)
    kk = kk / jnp.maximum(jnp.sqrt(jnp.sum(kk * kk, axis=-1, keepdims=True)), 1e-12)

    def step(S, inp):
        r_t, d_t, k_t, v_t, kk_t, b_t = inp
        sa = jnp.einsum('bhvk,bhk->bhv', S, -kk_t)
        S = S * d_t[:, :, None, :] + sa[..., None] * b_t[:, :, None, :] + v_t[..., None] * k_t[:, :, None, :]
        return S, jnp.einsum('bhvk,bhk->bhv', S, r_t)

    S_T, y = lax.scan(step, S0.astype(f32), tuple(t.swapaxes(0, 1) for t in (r, decay, k, v, kk, kk * a)))
    y = y.swapaxes(0, 1)
    mean = jnp.mean(y, axis=-1, keepdims=True)
    var = jnp.mean(jnp.square(y - mean), axis=-1, keepdims=True)
    y = ((y - mean) * lax.rsqrt(var + RW_GN_EPS)).reshape(B, T, D) * w['rw_ln_g'][j] + w['rw_ln_b'][j]
    bonus = jnp.sum(r * k * w['rw_r_k'][j], axis=-1, keepdims=True) * v
    y = y + bonus.reshape(B, T, D)
    return (y * g).astype(h.dtype) @ w['rw_w_o'][j], S_T, h[:, -1]


def memory_kv(mem, mem_norm, w_kv, k_gain):
    B = mem.shape[0]
    m = rmsnorm(mem[None], mem_norm[:, None, None])
    kv = jnp.einsum('lbmd,lde->lbme', m, w_kv).reshape(DEPTH, B, N_MEM, 2, XA_HEADS, XA_HEAD_DIM)
    kn = rmsnorm(kv[:, :, :, 0], k_gain[:, None, None, None])
    return jnp.stack([kn, kv[:, :, :, 1]], axis=3)


def mem_attend(h, kv, w_q, q_gain, w_o):
    B, T, _ = h.shape
    q = rmsnorm((h @ w_q).reshape(B, T, XA_HEADS, XA_HEAD_DIM), q_gain)
    s = jnp.einsum('bthe,bmhe->bhtm', q, kv[:, :, 0], preferred_element_type=jnp.float32) * (XA_HEAD_DIM ** -0.5)
    p = jax.nn.softmax(s, axis=-1)
    o = jnp.einsum('bhtm,bmhe->bthe', p.astype(kv.dtype), kv[:, :, 1])
    return o.reshape(B, T, D_MODEL).astype(h.dtype) @ w_o


def conv_ffn(h, buf, w_in, conv_w, conv_b, w_down):
    T = h.shape[1]
    u, gate = jnp.split(h @ w_in, 2, axis=-1)
    uc = jnp.concatenate([buf.astype(u.dtype), u], axis=1)
    c = conv_b
    for jj in range(CONV_W):
        c = c + conv_w[jj] * uc[:, jj:jj + T]
    return (jax.nn.silu(c) * gate) @ w_down, uc[:, T:]


def trunk(x, mem_kv, a_bufs, hg_S, rw_S, rw_shift, ffn_buf, w):
    new_a = ([], [], [])
    new_hg, new_rw, new_sh, new_ffn = [], [], [], []
    for i in range(DEPTH):
        kind, j = i % N_MIXERS, i // N_MIXERS
        h = rmsnorm(x, w['norm_mix'][i])
        if kind == 0:
            bufs = None if a_bufs is None else tuple(b[j] for b in a_bufs)
            y, nb = mixer_dilated(h, w['attn_w_qkv'][j], w['attn_q_gain'][j], w['attn_k_gain'][j], w['attn_w_o'][j], bufs)
            for gi in range(N_GROUPS):
                new_a[gi].append(nb[gi])
        elif kind == 1:
            y, S = mixer_hgrn(h, w['hg_w_in'][j], w['hg_lb'][i], w['hg_out_gain'][j], w['hg_w_o'][j], hg_S[j])
            new_hg.append(S)
        else:
            y, S, sh = mixer_rwkv(h, rw_shift[j], rw_S[j], w, j)
            new_rw.append(S)
            new_sh.append(sh)
        x = x + y
        h = rmsnorm(x, w['norm_mem'][i])
        x = x + mem_attend(h, mem_kv[i], w['xa_w_q'][i], w['xa_q_gain'][i], w['xa_w_o'][i])
        h = rmsnorm(x, w['norm_ffn'][i])
        y, fb = conv_ffn(h, ffn_buf[i], w['ffn_w_in'][i], w['ffn_conv_w'][i], w['ffn_conv_b'][i], w['ffn_w_down'][i])
        new_ffn.append(fb)
        x = x + y
    return (x, tuple(jnp.stack(b) for b in new_a), jnp.stack(new_hg), jnp.stack(new_rw), jnp.stack(new_sh), jnp.stack(new_ffn))


def setup_inputs(seed: int = 0) -> dict:
    key = jax.random.key(seed)
    ks = iter(jax.random.split(key, 64))
    D = D_MODEL

    def nrm(shape, scale=1.0):
        return jax.random.normal(next(ks), shape, jnp.float32) * scale

    def gain(shape):
        return 1.0 + nrm(shape, 0.05)

    def unif(shape, lo, hi):
        return jax.random.uniform(next(ks), shape, jnp.float32, lo, hi)

    return {
        'x_prompt': nrm((BATCH, SEQ, D)),
        'x_sample': nrm((DEC_BATCH, DEC_SEQ, D)),
        'mem_prompt': nrm((BATCH, N_MEM, D)),
        'cache_attn_kv_w128': nrm((N_A, DEC_BATCH, min(DIL_WINDOWS[0], PAST_LEN), 2, A_HEADS, A_HEAD_DIM)),
        'cache_attn_kv_w512': nrm((N_A, DEC_BATCH, min(DIL_WINDOWS[1], PAST_LEN), 2, A_HEADS, A_HEAD_DIM)),
        'cache_attn_kv_w2048': nrm((N_A, DEC_BATCH, min(DIL_WINDOWS[2], PAST_LEN), 2, A_HEADS, A_HEAD_DIM)),
        'state_hgrn': nrm((N_B, DEC_BATCH, HG_HEADS, HG_DK, HG_DV), 0.3),
        'state_rwkv': nrm((N_C, DEC_BATCH, RW_HEADS, RW_HEAD_DIM, RW_HEAD_DIM), 0.3),
        'state_rwkv_shift': nrm((N_C, DEC_BATCH, D)),
        'state_ffn_conv': nrm((DEPTH, DEC_BATCH, CONV_W - 1, D_FF)),
        'cache_mem_kv': nrm((DEPTH, DEC_BATCH, N_MEM, 2, XA_HEADS, XA_HEAD_DIM)),
        'norm_mix': gain((DEPTH, D)),
        'norm_mem': gain((DEPTH, D)),
        'norm_ffn': gain((DEPTH, D)),
        'mem_norm': gain((DEPTH, D)),
        'attn_w_qkv': nrm((N_A, D, 3 * N_GROUPS * A_HEADS * A_HEAD_DIM), D ** -0.5),
        'attn_q_gain': gain((N_A, N_GROUPS, A_HEAD_DIM)),
        'attn_k_gain': gain((N_A, N_GROUPS, A_HEAD_DIM)),
        'attn_w_o': nrm((N_A, A_HEADS * A_HEAD_DIM, D), (A_HEADS * A_HEAD_DIM) ** -0.5),
        'hg_w_in': nrm((N_B, D, 4 * HG_WIDTH), D ** -0.5),
        'hg_lb_logits': nrm((DEPTH, HG_WIDTH), 0.3),
        'hg_out_gain': gain((N_B, HG_DV)),
        'hg_w_o': nrm((N_B, HG_HEADS * HG_DV, D), (HG_HEADS * HG_DV) ** -0.5),
        'rw_mu': unif((N_C, 6, D), 0.0, 1.0),
        'rw_w_rkv': nrm((N_C, 3, D, D), D ** -0.5),
        'rw_w0': unif((N_C, D), -6.0, 1.0),
        'rw_w1': nrm((N_C, D, RW_LORA_W), D ** -0.5),
        'rw_w2': nrm((N_C, RW_LORA_W, D), RW_LORA_W ** -0.5),
        'rw_a0': nrm((N_C, D), 0.1),
        'rw_a1': nrm((N_C, D, RW_LORA_A), D ** -0.5),
        'rw_a2': nrm((N_C, RW_LORA_A, D), RW_LORA_A ** -0.5),
        'rw_g1': nrm((N_C, D, RW_LORA_G), D ** -0.5),
        'rw_g2': nrm((N_C, RW_LORA_G, D), RW_LORA_G ** -0.5),
        'rw_k_k': 0.85 + nrm((N_C, D), 0.05),
        'rw_k_a': gain((N_C, D)),
        'rw_r_k': nrm((N_C, RW_HEADS, RW_HEAD_DIM), 0.1),
        'rw_ln_g': gain((N_C, D)),
        'rw_ln_b': nrm((N_C, D), 0.02),
        'rw_w_o': nrm((N_C, D, D), D ** -0.5),
        'xa_w_q': nrm((DEPTH, D, D), D ** -0.5),
        'xa_w_kv': nrm((DEPTH, D, 2 * D), D ** -0.5),
        'xa_q_gain': gain((DEPTH, XA_HEAD_DIM)),
        'xa_k_gain': gain((DEPTH, XA_HEAD_DIM)),
        'xa_w_o': nrm((DEPTH, D, D), D ** -0.5),
        'ffn_w_in': nrm((DEPTH, D, 2 * D_FF), D ** -0.5),
        'ffn_conv_w': nrm((DEPTH, CONV_W, D_FF), 0.5),
        'ffn_conv_b': nrm((DEPTH, D_FF), 0.02),
        'ffn_w_down': nrm((DEPTH, D_FF, D), D_FF ** -0.5),
    }


def reference(x_prompt, x_sample, mem_prompt, cache_attn_kv_w128, cache_attn_kv_w512, cache_attn_kv_w2048,
              state_hgrn, state_rwkv, state_rwkv_shift, state_ffn_conv, cache_mem_kv,
              norm_mix, norm_mem, norm_ffn, mem_norm,
              attn_w_qkv, attn_q_gain, attn_k_gain, attn_w_o,
              hg_w_in, hg_lb_logits, hg_out_gain, hg_w_o,
              rw_mu, rw_w_rkv, rw_w0, rw_w1, rw_w2, rw_a0, rw_a1, rw_a2, rw_g1, rw_g2,
              rw_k_k, rw_k_a, rw_r_k, rw_ln_g, rw_ln_b, rw_w_o,
              xa_w_q, xa_w_kv, xa_q_gain, xa_k_gain, xa_w_o,
              ffn_w_in, ffn_conv_w, ffn_conv_b, ffn_w_down):
    lb = jnp.cumsum(jax.nn.softmax(hg_lb_logits.astype(jnp.float32), axis=0), axis=0)
    lb = lb - lb[0:1]
    w = {
        'norm_mix': norm_mix, 'norm_mem': norm_mem, 'norm_ffn': norm_ffn,
        'attn_w_qkv': attn_w_qkv, 'attn_q_gain': attn_q_gain, 'attn_k_gain': attn_k_gain, 'attn_w_o': attn_w_o,
        'hg_w_in': hg_w_in, 'hg_lb': lb, 'hg_out_gain': hg_out_gain, 'hg_w_o': hg_w_o,
        'rw_mu': rw_mu, 'rw_w_rkv': rw_w_rkv, 'rw_w0': rw_w0, 'rw_w1': rw_w1, 'rw_w2': rw_w2,
        'rw_a0': rw_a0, 'rw_a1': rw_a1, 'rw_a2': rw_a2, 'rw_g1': rw_g1, 'rw_g2': rw_g2,
        'rw_k_k': rw_k_k, 'rw_k_a': rw_k_a, 'rw_r_k': rw_r_k, 'rw_ln_g': rw_ln_g, 'rw_ln_b': rw_ln_b, 'rw_w_o': rw_w_o,
        'xa_w_q': xa_w_q, 'xa_q_gain': xa_q_gain, 'xa_w_o': xa_w_o,
        'ffn_w_in': ffn_w_in, 'ffn_conv_w': ffn_conv_w, 'ffn_conv_b': ffn_conv_b, 'ffn_w_down': ffn_w_down,
    }
    Bp = x_prompt.shape[0]
    mem_kv_prompt = memory_kv(mem_prompt, mem_norm, xa_w_kv, xa_k_gain)
    y_prompt, a_p, hg_p, rw_p, sh_p, ffn_p = trunk(
        x_prompt, mem_kv_prompt, None,
        jnp.zeros((N_B, Bp, HG_HEADS, HG_DK, HG_DV), jnp.float32),
        jnp.zeros((N_C, Bp, RW_HEADS, RW_HEAD_DIM, RW_HEAD_DIM), jnp.float32),
        jnp.zeros((N_C, Bp, D_MODEL), x_prompt.dtype),
        jnp.zeros((DEPTH, Bp, CONV_W - 1, D_FF), x_prompt.dtype), w)
    y_sample, a_s, hg_s, rw_s, sh_s, ffn_s = trunk(
        x_sample, cache_mem_kv, (cache_attn_kv_w128, cache_attn_kv_w512, cache_attn_kv_w2048),
        state_hgrn, state_rwkv, state_rwkv_shift, state_ffn_conv, w)
    return (y_prompt, y_sample, a_p[0], a_p[1], a_p[2], hg_p, rw_p, sh_p, ffn_p, mem_kv_prompt,
            a_s[0], a_s[1], a_s[2], hg_s, rw_s, sh_s, ffn_s)
```

```python
import functools
import math

import numpy as np
import jax
import jax.numpy as jnp
from jax import lax
from jax.experimental import pallas as pl
from jax.experimental.pallas import tpu as pltpu

F32 = jnp.float32
BF16 = jnp.bfloat16

D_MODEL = 1024
DEPTH = 4
N_MIXERS = 3
DIL_WINDOWS = (128, 512, 2048)
DIL_RATES = (1, 4, 16)
N_GROUPS = 3
A_SPAN = 128
A_HEADS = 4
A_HEAD_DIM = 128
A_BLOCK = 128
A_WIDTH = A_HEADS * A_HEAD_DIM
HG_HEADS = 8
HG_DK = 128
HG_CHUNK = 128
RW_HEADS = 16
RW_HEAD_DIM = 64
RW_CHUNK = 64
RW_GN_EPS = 64e-5
N_MEM = 256
XA_HEADS = 4
XA_HEAD_DIM = D_MODEL // XA_HEADS
D_FF = 2816
CONV_W = 3
RMS_EPS = 1e-6
LANES = 128
VMEM_LIMIT = 52 * 1024 * 1024

NT_DIMS = (((1,), (1,)), ((), ()))
TN_DIMS = (((0,), (0,)), ((), ()))


def _params(n_axes):
    return pltpu.CompilerParams(dimension_semantics=("arbitrary",) * n_axes, vmem_limit_bytes=VMEM_LIMIT)


def _rms(x):
    return x * lax.rsqrt(jnp.mean(x * x, axis=-1, keepdims=True) + RMS_EPS)


def _dot(a, b):
    return jnp.dot(a.astype(BF16), b.astype(BF16), preferred_element_type=F32)


def _dot_nt(a, b):
    return lax.dot_general(a.astype(BF16), b.astype(BF16), NT_DIMS, preferred_element_type=F32)


def _dot_tn(a, b):
    return lax.dot_general(a.astype(BF16), b.astype(BF16), TN_DIMS, preferred_element_type=F32)


def _split3(x):
    p1 = x.astype(BF16)
    r1 = x - p1.astype(F32)
    p2 = r1.astype(BF16)
    p3 = (r1 - p2.astype(F32)).astype(BF16)
    return jnp.concatenate([p1, p2, p3], axis=1)


def _exact_rowmix(mat_bf16, x):
    w = x.shape[1]
    y = jnp.dot(mat_bf16, _split3(x), preferred_element_type=F32)
    return y[:, :w] + y[:, w:2 * w] + y[:, 2 * w:]


def _softplus(x):
    return jnp.maximum(x, 0.0) + jnp.log1p(jnp.exp(-jnp.abs(x)))


def _sigmoid(x):
    return 1.0 / (1.0 + jnp.exp(-x))


def _silu(x):
    return x * _sigmoid(x)


def _nmm_body(x_ref, g_ref, w_ref, hg_ref, o_ref, h_scr, *, hn_width, hn_pred):
    j = pl.program_id(2)

    @pl.when(j == 0)
    def _():
        h_scr[...] = (_rms(x_ref[...]) * g_ref[...]).astype(BF16)

    y = jnp.dot(h_scr[...], w_ref[...], preferred_element_type=F32)
    if hn_width is None:
        o_ref[...] = y
        return
    do_norm = hn_pred(j)

    @pl.when(do_norm)
    def _():
        for c in range(y.shape[1] // hn_width):
            sl = slice(c * hn_width, (c + 1) * hn_width)
            o_ref[:, sl] = _rms(y[:, sl]) * hg_ref[:, sl]

    @pl.when(jnp.logical_not(do_norm))
    def _():
        o_ref[...] = y


def _norm_matmul(x, gain, w, head_gain=None, hn_width=None, hn_pred=None, tm=512, tn=512):
    m, d = x.shape
    nl, _, n = w.shape
    tm = min(tm, m)
    if head_gain is None:
        head_gain = jnp.ones((nl, 1, n), F32)
    grid = (nl, m // tm, n // tn)
    return pl.pallas_call(
        functools.partial(_nmm_body, hn_width=hn_width, hn_pred=hn_pred),
        grid=grid,
        in_specs=[
            pl.BlockSpec((tm, d), lambda l, i, j: (i, 0)),
            pl.BlockSpec((None, 1, d), lambda l, i, j: (l, 0, 0)),
            pl.BlockSpec((None, d, tn), lambda l, i, j: (l, 0, j)),
            pl.BlockSpec((None, 1, tn), lambda l, i, j: (l, 0, j)),
        ],
        out_specs=pl.BlockSpec((None, tm, tn), lambda l, i, j: (l, i, j)),
        out_shape=jax.ShapeDtypeStruct((nl, m, n), F32),
        scratch_shapes=[pltpu.VMEM((tm, d), BF16)],
        compiler_params=_params(3),
        name="norm_matmul",
    )(x, gain, w, head_gain)


def _mm_res_body(x_ref, a_ref, w_ref, o_ref):
    o_ref[...] = x_ref[...] + jnp.dot(a_ref[...].astype(BF16), w_ref[...], preferred_element_type=F32)


def _matmul_residual(x, a, w, tm=512):
    m, d = x.shape
    k = a.shape[1]
    tm = min(tm, m)
    return pl.pallas_call(
        _mm_res_body,
        grid=(m // tm,),
        in_specs=[
            pl.BlockSpec((tm, d), lambda i: (i, 0)),
            pl.BlockSpec((tm, k), lambda i: (i, 0)),
            pl.BlockSpec((k, d), lambda i: (0, 0)),
        ],
        out_specs=pl.BlockSpec((tm, d), lambda i: (i, 0)),
        out_shape=jax.ShapeDtypeStruct((m, d), F32),
        compiler_params=_params(1),
        name="matmul_residual",
    )(x, a, w)


def _ds(start, size, stride):
    return pl.ds(start, size) if stride == 1 else pl.ds(start, size, stride=stride)


def _attn_prompt_body(q0, q1, q2, k0, v0, k1, v1, k2, v2, o_ref, acc, m_s, l_s, *, seq):
    scale = A_HEAD_DIM ** -0.5
    blk = A_BLOCK
    for g, (q_ref, k_ref, v_ref) in enumerate(((q0, k0, v0), (q1, k1, v1), (q2, k2, v2))):
        dil = DIL_RATES[g]
        n = seq // dil
        nb = n // blk
        width = 2 * blk if nb > 1 else blk
        ii = lax.broadcasted_iota(jnp.int32, (blk, width), 0)
        jj = lax.broadcasted_iota(jnp.int32, (blk, width), 1)
        if nb > 1:
            in_cur = (jj >= blk) & (jj - blk <= ii)
            in_prev = (jj < blk) & (jj >= ii)
        else:
            in_cur, in_prev = jj <= ii, None

        def body(idx, carry, g=g, dil=dil, nb=nb, q_ref=q_ref, k_ref=k_ref, v_ref=v_ref, in_cur=in_cur,
                 in_prev=in_prev):
            r = idx // nb
            mb = idx % nb
            rows = _ds(r + dil * blk * mb, blk, dil)
            q = q_ref[rows, :]
            kc = k_ref[rows, :].astype(BF16)
            vc = v_ref[rows, :].astype(BF16)
            if nb > 1:
                prev = _ds(r + dil * blk * jnp.maximum(mb - 1, 0), blk, dil)
                kw = jnp.concatenate([k_ref[prev, :].astype(BF16), kc], axis=0)
                vw = jnp.concatenate([v_ref[prev, :].astype(BF16), vc], axis=0)
                prev_bias = jnp.where(mb > 0, 0.0, -jnp.inf)
                bias = jnp.where(in_cur, 0.0, jnp.where(in_prev, prev_bias, -jnp.inf))
            else:
                kw, vw = kc, vc
                bias = jnp.where(in_cur, 0.0, -jnp.inf)
            s = _dot_nt(q, kw) * scale + bias
            m = jnp.max(s, axis=-1, keepdims=True)
            p = jnp.exp(s - m)
            l = jnp.sum(p, axis=-1, keepdims=True)
            o = _dot(p, vw)
            m_b = jnp.broadcast_to(m, (blk, LANES))
            l_b = jnp.broadcast_to(l, (blk, LANES))
            if g == 0:
                acc[rows, :] = o
                m_s[rows, :] = m_b
                l_s[rows, :] = l_b
            else:
                m_old = m_s[rows, :]
                m_new = jnp.maximum(m_old, m_b)
                a_old = jnp.exp(m_old - m_new)
                a_cur = jnp.exp(m_b - m_new)
                acc[rows, :] = acc[rows, :] * a_old + o * a_cur
                l_s[rows, :] = l_s[rows, :] * a_old + l_b * a_cur
                m_s[rows, :] = m_new
            return carry

        lax.fori_loop(0, dil * nb, body, 0)
    o_ref[...] = acc[...] / l_s[...]


def _attn_prompt(qkv, batch, seq):
    m = batch * seq
    assert seq % (A_BLOCK * max(DIL_RATES)) == 0 and A_BLOCK == A_SPAN
    nq = N_GROUPS * A_HEADS

    def spec(col_fn):
        return pl.BlockSpec((seq, A_HEAD_DIM), lambda b, h: (b, col_fn(h)))

    in_specs = [spec(lambda h, g=g: g * A_HEADS + h) for g in range(N_GROUPS)]
    for g in range(N_GROUPS):
        in_specs.append(spec(lambda h, g=g: nq + g * 2 * A_HEADS + h))
        in_specs.append(spec(lambda h, g=g: nq + g * 2 * A_HEADS + A_HEADS + h))
    return pl.pallas_call(
        functools.partial(_attn_prompt_body, seq=seq),
        grid=(batch, A_HEADS),
        in_specs=in_specs,
        out_specs=pl.BlockSpec((seq, A_HEAD_DIM), lambda b, h: (b, h)),
        out_shape=jax.ShapeDtypeStruct((m, A_WIDTH), F32),
        scratch_shapes=[pltpu.VMEM((seq, A_HEAD_DIM), F32), pltpu.VMEM((seq, LANES), F32),
                        pltpu.VMEM((seq, LANES), F32)],
        compiler_params=_params(2),
        name="dilated_attn_prompt",
    )(*([qkv] * 9))


def _attn_sample_body(q_ref, n0, n1, n2, b0, b1, b2, o_ref, c0, c1, c2, s0, s1, s2, sn, *, t_new):
    c = pl.program_id(1)
    scale = A_HEAD_DIM ** -0.5
    news, bufs, outs, scrs = (n0, n1, n2), (b0, b1, b2), (c0, c1, c2), (s0, s1, s2)
    pad = jnp.zeros((LANES - t_new, A_WIDTH), F32)
    for g in range(N_GROUPS):
        length = bufs[g].shape[0]
        outs[g][0:length - t_new, :] = bufs[g][t_new:length, :]
        outs[g][length - t_new:length, :] = news[g][...]

    @pl.when(c == 0)
    def _():
        for h in range(A_HEADS):
            hs = slice(h * A_HEAD_DIM, (h + 1) * A_HEAD_DIM)
            parts = []
            for g in range(N_GROUPS):
                dil = DIL_RATES[g]
                length = bufs[g].shape[0]
                q = q_ref[:, g * A_WIDTH + h * A_HEAD_DIM: g * A_WIDTH + (h + 1) * A_HEAD_DIM]
                s_buf = _dot_nt(q, bufs[g][:, hs]) * scale
                knew = jnp.concatenate([news[g][...], pad], axis=0)[:, hs]
                s_new = _dot_nt(q, knew) * scale
                for s, base in ((s_buf, 0), (s_new, length)):
                    ii = lax.broadcasted_iota(jnp.int32, s.shape, 0)
                    jj = lax.broadcasted_iota(jnp.int32, s.shape, 1) + base
                    dd = jj - ii
                    valid = (dd >= 0) & (dd <= A_SPAN * dil) & ((dd & (dil - 1)) == 0) & (jj < length + t_new)
                    parts.append(jnp.where(valid, s, -jnp.inf))
            mx = functools.reduce(jnp.maximum, [jnp.max(s, axis=-1, keepdims=True) for s in parts])
            exps = [jnp.exp(s - mx) for s in parts]
            den = functools.reduce(lambda a, b: a + b, [jnp.sum(e, axis=-1, keepdims=True) for e in exps])
            for g in range(N_GROUPS):
                scrs[g][h] = exps[2 * g] / den
                sn[g * A_HEADS + h] = exps[2 * g + 1] / den

    @pl.when(c == 1)
    def _():
        for h in range(A_HEADS):
            hs = slice(h * A_HEAD_DIM, (h + 1) * A_HEAD_DIM)
            o = jnp.zeros((t_new, A_HEAD_DIM), F32)
            for g in range(N_GROUPS):
                vnew = jnp.concatenate([news[g][...], pad], axis=0)[:, hs]
                o = o + _dot(scrs[g][h], bufs[g][:, hs]) + _dot(sn[g * A_HEADS + h], vnew)
            o_ref[:, hs] = o


def _attn_sample(qkv, caches, j, batch, t_new):
    assert t_new % 8 == 0 and t_new <= LANES
    lens = [cch.shape[2] for cch in caches]
    for g in range(N_GROUPS):
        assert lens[g] == A_SPAN * DIL_RATES[g] and (DIL_RATES[g] & (DIL_RATES[g] - 1)) == 0
    in_specs = [pl.BlockSpec((t_new, N_GROUPS * A_WIDTH), lambda b, c: (b, 0))]
    for g in range(N_GROUPS):
        in_specs.append(pl.BlockSpec((t_new, A_WIDTH), lambda b, c, g=g: (b, N_GROUPS + 2 * g + c)))
    for g in range(N_GROUPS):
        in_specs.append(pl.BlockSpec((None, None, lens[g], A_WIDTH), lambda b, c: (j, b, 0, c)))
    out_specs = [pl.BlockSpec((t_new, A_WIDTH), lambda b, c: (b, 0))]
    out_shape = [jax.ShapeDtypeStruct((batch * t_new, A_WIDTH), F32)]
    for g in range(N_GROUPS):
        out_specs.append(pl.BlockSpec((None, lens[g], A_WIDTH), lambda b, c: (b, 0, c)))
        out_shape.append(jax.ShapeDtypeStruct((batch, lens[g], 2 * A_WIDTH), F32))
    scratch = [pltpu.VMEM((A_HEADS, t_new, lens[g]), F32) for g in range(N_GROUPS)]
    scratch.append(pltpu.VMEM((N_GROUPS * A_HEADS, t_new, LANES), F32))
    return pl.pallas_call(
        functools.partial(_attn_sample_body, t_new=t_new),
        grid=(batch, 2),
        in_specs=in_specs,
        out_specs=out_specs,
        out_shape=out_shape,
        scratch_shapes=scratch,
        compiler_params=_params(2),
        name="dilated_attn_sample",
    )(qkv, qkv, qkv, qkv, *caches)


def _hgrn_consts(c):
    t = np.arange(c)[:, None]
    u = np.arange(c)[None, :]
    mats = [(u <= t), (u > t)]
    masks = [(u == t)]
    s = 1
    while s < c:
        upper = (t // s) % 2 == 1
        mid = (t // (2 * s)) * 2 * s + s - 1
        mats.append(np.where(upper, (u > mid) & (u <= t), (u > t) & (u <= mid)))
        lower_u = (u // s) % 2 == 0
        masks.append(upper & lower_u & (t // (2 * s) == u // (2 * s)))
        s *= 2
    mats = np.concatenate([m.astype(np.float32) for m in mats], axis=0)
    masks = np.stack([m.astype(np.float32) for m in masks], axis=0)
    return jnp.asarray(mats, BF16), jnp.asarray(masks, F32)


def _hgrn_body(q_ref, z_ref, i_ref, g_ref, lbl_ref, og_ref, s0_ref, mat_ref, msk_ref, o_ref, st_ref, st_scr,
               *, layer, t_len):
    c = HG_CHUNK
    n_lvl = msk_ref.shape[0] - 1
    lg = lbl_ref[...]
    e = jnp.exp(lg - jnp.max(lg, axis=0, keepdims=True))
    prob = e / jnp.sum(e, axis=0, keepdims=True)
    lb = jnp.sum(prob[0:layer + 1], axis=0, keepdims=True) - prob[0:1]
    log_lb = jnp.log(lb)
    log_1m = jnp.log1p(-lb)
    st_scr[...] = s0_ref[...].T
    row = lax.broadcasted_iota(jnp.int32, (c, HG_DK), 0)

    def chunk(ci, carry):
        if t_len >= c:
            rows = pl.ds(pl.multiple_of(ci * c, c), c)
            ld = lambda ref: ref[rows, :]
            live = None
        else:
            padz = jnp.zeros((c - t_len, HG_DK), F32)
            ld = lambda ref: jnp.concatenate([ref[...], padz], axis=0)
            live = row < t_len
        z = ld(z_ref)
        lsig = jnp.minimum(z, 0.0) - jnp.log1p(jnp.exp(-jnp.abs(z)))
        a1 = jnp.broadcast_to(log_lb, z.shape)
        a2 = log_1m + lsig
        hi = jnp.maximum(a1, a2)
        lo = jnp.minimum(a1, a2)
        log_f = hi + jnp.log1p(jnp.exp(lo - hi))
        kg = (1.0 - lb) * _sigmoid(-z)
        if live is not None:
            log_f = jnp.where(live, log_f, 0.0)
            kg = jnp.where(live, kg, 0.0)
        qa = _silu(ld(q_ref))
        vv = ld(i_ref)
        ex = _exact_rowmix(mat_ref[...], log_f)
        b = ex[0:c]
        st = st_scr[...]
        a = _dot_nt(qa, kg) * msk_ref[0]
        for lv in range(n_lvl):
            w = jnp.exp(ex[(2 + lv) * c:(3 + lv) * c])
            a = a + _dot_nt(qa * w, kg * w) * msk_ref[1 + lv]
        o = _dot_nt(qa * jnp.exp(b), st) + _dot(a, vv)
        st_scr[...] = st * jnp.exp(b[c - 1:c]) + _dot_tn(vv, kg * jnp.exp(ex[c:2 * c]))
        y = _rms(o) * og_ref[...] * _silu(ld(g_ref))
        if t_len >= c:
            o_ref[rows, :] = y
        else:
            o_ref[...] = y[0:t_len]
        return carry

    lax.fori_loop(0, max(t_len // c, 1), chunk, 0)
    st_ref[...] = st_scr[...].T


def _hgrn(proj, lb_logits, out_gain, s0, layer, batch, t_len):
    mats, masks = _hgrn_consts(HG_CHUNK)
    assert t_len % HG_CHUNK == 0 or (t_len < HG_CHUNK and t_len % 8 == 0)

    def col(part):
        return pl.BlockSpec((t_len, HG_DK), lambda b, h: (b, part * HG_HEADS + h))

    return pl.pallas_call(
        functools.partial(_hgrn_body, layer=layer, t_len=t_len),
        grid=(batch, HG_HEADS),
        in_specs=[col(0), col(1), col(2), col(3),
                  pl.BlockSpec((DEPTH, HG_DK), lambda b, h: (0, h)),
                  pl.BlockSpec((1, HG_DK), lambda b, h: (0, 0)),
                  pl.BlockSpec((None, None, HG_DK, HG_DK), lambda b, h: (b, h, 0, 0)),
                  pl.BlockSpec(mats.shape, lambda b, h: (0, 0)),
                  pl.BlockSpec(masks.shape, lambda b, h: (0, 0, 0))],
        out_specs=[pl.BlockSpec((t_len, HG_DK), lambda b, h: (b, h)),
                   pl.BlockSpec((None, None, HG_DK, HG_DK), lambda b, h: (b, h, 0, 0))],
        out_shape=[jax.ShapeDtypeStruct((batch * t_len, HG_HEADS * HG_DK), F32),
                   jax.ShapeDtypeStruct((batch, HG_HEADS, HG_DK, HG_DK), F32)],
        scratch_shapes=[pltpu.VMEM((HG_DK, HG_DK), F32)],
        compiler_params=_params(2),
        name="hgrn2_chunked",
    )(proj, proj, proj, proj, lb_logits, out_gain, s0, mats, masks)


def _rwkv_proj_body(x_ref, g_ref, sh_ref, mu_ref, wrkv_ref, w0_ref, w1_ref, w2_ref, a0_ref, a1_ref, a2_ref,
                    g1_ref, g2_ref, kk_ref, ka_ref,
                    r_o, k_o, v_o, ld_o, kk_o, a_o, g_o, sh_o, carry, *, bb, tt):
    t = pl.program_id(1)
    d = D_MODEL

    @pl.when(t == 0)
    def _():
        carry[...] = sh_ref[...]

    h3 = _rms(x_ref[...]) * g_ref[...]
    h = h3.reshape(bb * tt, d)
    rolled = pltpu.roll(h, 1, 0).reshape(bb, tt, d)
    tpos = lax.broadcasted_iota(jnp.int32, (bb, tt, d), 1)
    prev = jnp.where(tpos == 0, carry[...], rolled).reshape(bb * tt, d)
    last = h3[:, tt - 1:tt, :]
    carry[...] = last
    sh_o[...] = last
    dx = prev - h
    mix = lambda jm: h + dx * mu_ref[jm:jm + 1, :]
    r = _dot(mix(0), wrkv_ref[0])
    k = _dot(mix(1), wrkv_ref[1])
    v = _dot(mix(2), wrkv_ref[2])
    wl = w0_ref[...] + _dot(jnp.tanh(_dot(mix(3), w1_ref[...])), w2_ref[...])
    wlog = -_softplus(-wl) - 0.5
    a = _sigmoid(a0_ref[...] + _dot(_dot(mix(4), a1_ref[...]), a2_ref[...]))
    gate = _dot(_sigmoid(_dot(mix(5), g1_ref[...])), g2_ref[...])
    r_o[...] = r
    k_o[...] = k * (1.0 + (a - 1.0) * ka_ref[...])
    v_o[...] = v
    ld_o[...] = -jnp.exp(wlog)
    kk_o[...] = k * kk_ref[...]
    a_o[...] = a
    g_o[...] = gate


def _rwkv_proj(x3, gain, shift, mu, wrkv, w0, w1, w2, a0, a1, a2, g1, g2, k_k, k_a, bb, tt):
    batch, t_len, d = x3.shape
    m = batch * t_len
    rows = bb * tt
    full = lambda arr: pl.BlockSpec(arr.shape, lambda b, t: (0,) * arr.ndim)
    row_spec = pl.BlockSpec((rows, d), lambda b, t: (b * (t_len // tt) + t, 0))
    outs = pl.pallas_call(
        functools.partial(_rwkv_proj_body, bb=bb, tt=tt),
        grid=(batch // bb, t_len // tt),
        in_specs=[pl.BlockSpec((bb, tt, d), lambda b, t: (b, t, 0)), full(gain),
                  pl.BlockSpec((bb, 1, d), lambda b, t: (b, 0, 0)), full(mu), full(wrkv),
                  full(w0), full(w1), full(w2), full(a0), full(a1), full(a2), full(g1), full(g2),
                  full(k_k), full(k_a)],
        out_specs=[row_spec] * 7 + [pl.BlockSpec((bb, 1, d), lambda b, t: (b, 0, 0))],
        out_shape=[jax.ShapeDtypeStruct((m, d), F32)] * 7 + [jax.ShapeDtypeStruct((batch, 1, d), F32)],
        scratch_shapes=[pltpu.VMEM((bb, 1, d), F32)],
        compiler_params=_params(2),
        name="rwkv_proj",
    )(x3, gain, shift, mu, wrkv, w0, w1, w2, a0, a1, a2, g1, g2, k_k, k_a)
    return outs


def _rwkv_consts(c):
    t = np.arange(c)[:, None]
    u = np.arange(c)[None, :]
    tri = jnp.asarray((u <= t).astype(np.float32), BF16)
    lo_strict = jnp.asarray((u < t).astype(np.float32), F32)
    lo_incl = jnp.asarray((u <= t).astype(np.float32), F32)
    eye = jnp.asarray((u == t).astype(np.float32), F32)
    return tri, lo_strict, lo_incl, eye


def _rwkv_rec_body(r_ref, k_ref, v_ref, ld_ref, kk_ref, a_ref, g_ref, rk_ref, lng_ref, lnb_ref, s0_ref,
                   tri_ref, los_ref, loi_ref, eye_ref, o_ref, st_ref, st_scr, *, heads, tt):
    c = RW_CHUNK
    n = RW_HEAD_DIM
    t = pl.program_id(2)

    @pl.when(t == 0)
    def _():
        st_scr[...] = s0_ref[...]

    def chunk(ci, carry):
        if tt >= c:
            rows = pl.ds(pl.multiple_of(ci * c, c), c)
        for hd in range(heads):
            hs = slice(hd * n, (hd + 1) * n)
            if tt >= c:
                ld = lambda ref: ref[rows, hs]
            else:
                padz = jnp.zeros((c - tt, n), F32)
                ld = lambda ref: jnp.concatenate([ref[:, hs], padz], axis=0)
            r, kf, v, lc, kk, a, gate = (ld(ref) for ref in (r_ref, k_ref, v_ref, ld_ref, kk_ref, a_ref, g_ref))
            kk = kk / jnp.maximum(jnp.sqrt(jnp.sum(kk * kk, axis=-1, keepdims=True)), 1e-12)
            bvec = kk * a
            cum = _exact_rowmix(tri_ref[...], lc)
            c_end = cum[c - 1:c]
            e_neg = jnp.exp(-cum)
            e_end = jnp.exp(c_end - cum)
            kap = kk * jnp.exp(cum - lc)
            rho = r * jnp.exp(cum)
            s_old = st_scr[hd]
            kr = jnp.concatenate([kap, rho], axis=0)
            bx = jnp.concatenate([bvec * e_neg, kf * e_neg], axis=0)
            gm = _dot_nt(kr, bx)
            a_ab = gm[0:c, 0:c] * los_ref[...]
            a_ak = gm[0:c, c:2 * c] * los_ref[...]
            b_rb = gm[c:2 * c, 0:c] * loi_ref[...]
            b_rk = gm[c:2 * c, c:2 * c] * loi_ref[...]
            tm = eye_ref[...] - a_ab
            npow = _dot(a_ab, a_ab)
            p = 2
            while p < c:
                tm = tm + _dot(tm, npow)
                p *= 2
                if p < c:
                    npow = _dot(npow, npow)
            ks = _dot_nt(kr, s_old)
            av = _dot(jnp.concatenate([a_ak, b_rk], axis=0), v)
            u = -_dot(tm, ks[0:c] + av[0:c])
            y = ks[c:2 * c] + _dot(b_rb, u) + av[c:2 * c]
            st_scr[hd] = s_old * jnp.exp(c_end) + _dot_tn(jnp.concatenate([u, v], axis=0),
                                                          jnp.concatenate([bvec * e_end, kf * e_end], axis=0))
            mean = jnp.mean(y, axis=-1, keepdims=True)
            yc = y - mean
            var = jnp.mean(yc * yc, axis=-1, keepdims=True)
            yn = yc * lax.rsqrt(var + RW_GN_EPS) * lng_ref[:, hs] + lnb_ref[:, hs]
            bonus = jnp.sum(r * kf * rk_ref[:, hs], axis=-1, keepdims=True) * v
            z = (yn + bonus) * gate
            if tt >= c:
                o_ref[rows, hs] = z
            else:
                o_ref[:, hs] = z[0:tt]
        return carry

    lax.fori_loop(0, max(tt // c, 1), chunk, 0)
    st_ref[...] = st_scr[...]


def _rwkv_rec(proj, r_k, ln_g, ln_b, s0, batch, t_len, heads=8, tt=512):
    r, kf, v, ld, kk, a, gate = proj
    tt = min(tt, t_len)
    assert tt % RW_CHUNK == 0 or (tt < RW_CHUNK and tt % 8 == 0)
    tri, lo_strict, lo_incl, eye = _rwkv_consts(RW_CHUNK)
    width = heads * RW_HEAD_DIM
    nt = t_len // tt
    row_spec = pl.BlockSpec((tt, width), lambda b, hg, t: (b * nt + t, hg))
    par_spec = pl.BlockSpec((1, width), lambda b, hg, t: (0, hg))
    st_spec = pl.BlockSpec((None, heads, RW_HEAD_DIM, RW_HEAD_DIM), lambda b, hg, t: (b, hg, 0, 0))
    const = lambda arr: pl.BlockSpec(arr.shape, lambda b, hg, t: (0, 0))
    return pl.pallas_call(
        functools.partial(_rwkv_rec_body, heads=heads, tt=tt),
        grid=(batch, RW_HEADS // heads, nt),
        in_specs=[row_spec] * 7 + [par_spec] * 3 + [st_spec, const(tri), const(lo_strict), const(lo_incl), const(eye)],
        out_specs=[row_spec, st_spec],
        out_shape=[jax.ShapeDtypeStruct((batch * t_len, D_MODEL), F32),
                   jax.ShapeDtypeStruct((batch, RW_HEADS, RW_HEAD_DIM, RW_HEAD_DIM), F32)],
        scratch_shapes=[pltpu.VMEM((heads, RW_HEAD_DIM, RW_HEAD_DIM), F32)],
        compiler_params=_params(3),
        name="rwkv7_chunked",
    )(r, kf, v, ld, kk, a, gate, r_k, ln_g, ln_b, s0, tri, lo_strict, lo_incl, eye)


def _xattn_body(x_ref, g_ref, wq_ref, qg_ref, kv_ref, wo_ref, o_ref, *, bb, tt):
    d = D_MODEL
    e = XA_HEAD_DIM
    scale = XA_HEAD_DIM ** -0.5
    x = x_ref[...].reshape(bb * tt, d)
    q = jnp.dot((_rms(x) * g_ref[...]).astype(BF16), wq_ref[...], preferred_element_type=F32)
    per_batch = []
    for b in range(bb):
        heads = []
        for h in range(XA_HEADS):
            qh = _rms(q[b * tt:(b + 1) * tt, h * e:(h + 1) * e]) * qg_ref[...]
            s = _dot_nt(qh, kv_ref[b, :, h * e:(h + 1) * e]) * scale
            p = jnp.exp(s - jnp.max(s, axis=-1, keepdims=True))
            p = p / jnp.sum(p, axis=-1, keepdims=True)
            heads.append(_dot(p, kv_ref[b, :, d + h * e: d + (h + 1) * e]))
        per_batch.append(jnp.concatenate(heads, axis=-1))
    o = jnp.concatenate(per_batch, axis=0) if bb > 1 else per_batch[0]
    y = x + jnp.dot(o.astype(BF16), wo_ref[...], preferred_element_type=F32)
    o_ref[...] = y.reshape(bb, tt, d)


def _xattn(x3, gain, wq, q_gain, kv, wo, bb, tt):
    batch, t_len, d = x3.shape
    return pl.pallas_call(
        functools.partial(_xattn_body, bb=bb, tt=tt),
        grid=(batch // bb, t_len // tt),
        in_specs=[pl.BlockSpec((bb, tt, d), lambda b, t: (b, t, 0)),
                  pl.BlockSpec((1, d), lambda b, t: (0, 0)),
                  pl.BlockSpec((d, d), lambda b, t: (0, 0)),
                  pl.BlockSpec((1, XA_HEAD_DIM), lambda b, t: (0, 0)),
                  pl.BlockSpec((bb, N_MEM, 2 * d), lambda b, t: (b, 0, 0)),
                  pl.BlockSpec((d, d), lambda b, t: (0, 0))],
        out_specs=pl.BlockSpec((bb, tt, d), lambda b, t: (b, t, 0)),
        out_shape=jax.ShapeDtypeStruct((batch, t_len, d), F32),
        compiler_params=_params(2),
        name="memory_xattn",
    )(x3, gain, wq, q_gain, kv, wo)


def _ffn_body(x_ref, g_ref, wu_ref, wg_ref, cw_ref, cb_ref, wd_ref, pin_ref, o_ref, st_ref, h_scr, acc, carry,
              *, bb, tt, nf):
    t = pl.program_id(1)
    f = pl.program_id(2)
    d = D_MODEL
    tf = wu_ref.shape[1]

    @pl.when(f == 0)
    def _():
        x = x_ref[...].reshape(bb * tt, d)
        h_scr[...] = (_rms(x) * g_ref[...]).astype(BF16)
        acc[...] = x

    @pl.when(t == 0)
    def _():
        carry[f] = pin_ref[...]

    h = h_scr[...]
    u = jnp.dot(h, wu_ref[...], preferred_element_type=F32)
    gate = jnp.dot(h, wg_ref[...], preferred_element_type=F32)
    prev = carry[f]
    tpos = lax.broadcasted_iota(jnp.int32, (bb, tt, tf), 1)
    u3 = u.reshape(bb, tt, tf)
    u1 = jnp.where(tpos == 0, prev[:, 1:2, :], pltpu.roll(u, 1, 0).reshape(bb, tt, tf))
    u2 = pltpu.roll(u, 2, 0).reshape(bb, tt, tf)
    u2 = jnp.where(tpos == 0, prev[:, 0:1, :], jnp.where(tpos == 1, prev[:, 1:2, :], u2))
    conv = cb_ref[...] + cw_ref[0:1, :] * u2 + cw_ref[1:2, :] * u1 + cw_ref[2:3, :] * u3
    act = (_silu(conv) * gate.reshape(bb, tt, tf)).reshape(bb * tt, tf)
    acc[...] += jnp.dot(act.astype(BF16), wd_ref[...], preferred_element_type=F32)
    tail = u3[:, tt - 2:tt, :]
    carry[f] = tail
    st_ref[...] = tail

    @pl.when(f == nf - 1)
    def _():
        o_ref[...] = acc[...].reshape(bb, tt, d)


def _ffn(x3, gain, w_in, conv_w, conv_b, w_down, state, bb, tt, tf=256):
    batch, t_len, d = x3.shape
    nf = D_FF // tf
    assert t_len >= CONV_W - 1 and tt >= CONV_W - 1
    return pl.pallas_call(
        functools.partial(_ffn_body, bb=bb, tt=tt, nf=nf),
        grid=(batch // bb, t_len // tt, nf),
        in_specs=[pl.BlockSpec((bb, tt, d), lambda b, t, f: (b, t, 0)),
                  pl.BlockSpec((1, d), lambda b, t, f: (0, 0)),
                  pl.BlockSpec((d, tf), lambda b, t, f: (0, f)),
                  pl.BlockSpec((d, tf), lambda b, t, f: (0, nf + f)),
                  pl.BlockSpec((CONV_W, tf), lambda b, t, f: (0, f)),
                  pl.BlockSpec((1, tf), lambda b, t, f: (0, f)),
                  pl.BlockSpec((tf, d), lambda b, t, f: (f, 0)),
                  pl.BlockSpec((bb, CONV_W - 1, tf), lambda b, t, f: (b, 0, f))],
        out_specs=[pl.BlockSpec((bb, tt, d), lambda b, t, f: (b, t, 0)),
                   pl.BlockSpec((bb, CONV_W - 1, tf), lambda b, t, f: (b, 0, f))],
        out_shape=[jax.ShapeDtypeStruct((batch, t_len, d), F32),
                   jax.ShapeDtypeStruct((batch, CONV_W - 1, D_FF), F32)],
        scratch_shapes=[pltpu.VMEM((bb * tt, d), BF16), pltpu.VMEM((bb * tt, d), F32),
                        pltpu.VMEM((nf, bb, CONV_W - 1, tf), F32)],
        compiler_params=_params(3),
        name="conv_ffn",
    )(x3, gain, w_in, w_in, conv_w, conv_b, w_down, state)


def _trunk(x3, mem_kv, a_bufs, hg_s, rw_s, rw_shift, ffn_buf, w, prompt):
    batch, t_len, d = x3.shape
    m = batch * t_len
    new_a = ([], [], [])
    new_hg, new_rw, new_sh, new_ffn = [], [], [], []
    if prompt:
        xa_bb, xa_tt = 1, 512
        ff_bb, ff_tt = 1, 1024
        rw_bb, rw_tt = 1, 256
    else:
        xa_bb, xa_tt = 4, t_len
        ff_bb, ff_tt = batch, t_len
        rw_bb, rw_tt = batch, t_len
    for i in range(DEPTH):
        kind, j = i % N_MIXERS, i // N_MIXERS
        x2 = x3.reshape(m, d)
        g_mix = w['norm_mix'][i][None, None]
        if kind == 0:
            qkv = _norm_matmul(x2, g_mix, w['attn_w_qkv'][j][None], w['attn_head_gain'][j][None],
                               hn_width=A_HEAD_DIM, hn_pred=lambda jt: (jt < 3) | (jt % 2 == 1))[0]
            if prompt:
                o = _attn_prompt(qkv, batch, t_len)
                kv = qkv[:, N_GROUPS * A_WIDTH:].reshape(batch, t_len, N_GROUPS, 2, A_HEADS, A_HEAD_DIM)
                for gi in range(N_GROUPS):
                    new_a[gi].append(kv[:, t_len - min(DIL_WINDOWS[gi], t_len):, gi])
            else:
                o, c0, c1, c2 = _attn_sample(qkv, a_bufs, j, batch, t_len)
                for gi, cc in enumerate((c0, c1, c2)):
                    new_a[gi].append(cc.reshape(batch, cc.shape[1], 2, A_HEADS, A_HEAD_DIM))
            x2 = _matmul_residual(x2, o, w['attn_w_o'][j])
        elif kind == 1:
            proj = _norm_matmul(x2, g_mix, w['hg_w_in'][j][None])[0]
            o, st = _hgrn(proj, w['hg_lb_logits'], w['hg_out_gain'][j][None], hg_s[j], i, batch, t_len)
            new_hg.append(st)
            x2 = _matmul_residual(x2, o, w['hg_w_o'][j])
        else:
            proj = _rwkv_proj(x3, w['norm_mix'][i][None], rw_shift[j][:, None, :], w['rw_mu'][j], w['rw_w_rkv'][j],
                              w['rw_w0'][j][None], w['rw_w1'][j], w['rw_w2'][j], w['rw_a0'][j][None],
                              w['rw_a1'][j], w['rw_a2'][j], w['rw_g1'][j], w['rw_g2'][j],
                              w['rw_k_k'][j][None], w['rw_k_a'][j][None], rw_bb, rw_tt)
            o, st = _rwkv_rec(proj[:7], w['rw_r_k'][j].reshape(1, d), w['rw_ln_g'][j][None], w['rw_ln_b'][j][None],
                              rw_s[j], batch, t_len)
            new_rw.append(st)
            new_sh.append(proj[7][:, 0, :])
            x2 = _matmul_residual(x2, o, w['rw_w_o'][j])
        x3 = x2.reshape(batch, t_len, d)
        x3 = _xattn(x3, w['norm_mem'][i][None], w['xa_w_q'][i], w['xa_q_gain'][i][None], mem_kv[i],
                    w['xa_w_o'][i], xa_bb, xa_tt)
        x3, fb = _ffn(x3, w['norm_ffn'][i][None], w['ffn_w_in'][i], w['ffn_conv_w'][i], w['ffn_conv_b'][i][None],
                      w['ffn_w_down'][i], ffn_buf[i], ff_bb, ff_tt)
        new_ffn.append(fb)
    return (x3, tuple(jnp.stack(b) for b in new_a), jnp.stack(new_hg), jnp.stack(new_rw), jnp.stack(new_sh),
            jnp.stack(new_ffn))


def kernel(x_prompt, x_sample, mem_prompt, cache_attn_kv_w128, cache_attn_kv_w512, cache_attn_kv_w2048, state_hgrn, state_rwkv, state_rwkv_shift, state_ffn_conv, cache_mem_kv, norm_mix, norm_mem, norm_ffn, mem_norm, attn_w_qkv, attn_q_gain, attn_k_gain, attn_w_o, hg_w_in, hg_lb_logits, hg_out_gain, hg_w_o, rw_mu, rw_w_rkv, rw_w0, rw_w1, rw_w2, rw_a0, rw_a1, rw_a2, rw_g1, rw_g2, rw_k_k, rw_k_a, rw_r_k, rw_ln_g, rw_ln_b, rw_w_o, xa_w_q, xa_w_kv, xa_q_gain, xa_k_gain, xa_w_o, ffn_w_in, ffn_conv_w, ffn_conv_b, ffn_w_down):
    d = D_MODEL
    bp = x_prompt.shape[0]
    bs = x_sample.shape[0]
    n_a, n_b, n_c = attn_w_qkv.shape[0], hg_w_in.shape[0], rw_w_rkv.shape[0]
    bf = lambda a: a.astype(BF16)

    wq4 = attn_w_qkv.reshape(n_a, d, 3, N_GROUPS, A_WIDTH)
    w_qkv = jnp.concatenate([wq4[:, :, 0].reshape(n_a, d, N_GROUPS * A_WIDTH),
                             jnp.swapaxes(wq4[:, :, 1:3], 2, 3).reshape(n_a, d, N_GROUPS * 2 * A_WIDTH)], axis=-1)
    tile_h = lambda gn: jnp.broadcast_to(gn[:, :, None, :], (n_a, N_GROUPS, A_HEADS, A_HEAD_DIM))
    kgain = jnp.stack([tile_h(attn_k_gain), jnp.ones((n_a, N_GROUPS, A_HEADS, A_HEAD_DIM), F32)], axis=2)
    head_gain = jnp.concatenate([tile_h(attn_q_gain).reshape(n_a, 1, -1), kgain.reshape(n_a, 1, -1)], axis=-1)

    w = {
        'norm_mix': norm_mix, 'norm_mem': norm_mem, 'norm_ffn': norm_ffn,
        'attn_w_qkv': bf(w_qkv), 'attn_head_gain': head_gain, 'attn_w_o': bf(attn_w_o),
        'hg_w_in': bf(hg_w_in), 'hg_lb_logits': hg_lb_logits, 'hg_out_gain': hg_out_gain, 'hg_w_o': bf(hg_w_o),
        'rw_mu': rw_mu, 'rw_w_rkv': bf(rw_w_rkv), 'rw_w0': rw_w0, 'rw_w1': bf(rw_w1), 'rw_w2': bf(rw_w2),
        'rw_a0': rw_a0, 'rw_a1': bf(rw_a1), 'rw_a2': bf(rw_a2), 'rw_g1': bf(rw_g1), 'rw_g2': bf(rw_g2),
        'rw_k_k': rw_k_k, 'rw_k_a': rw_k_a, 'rw_r_k': rw_r_k, 'rw_ln_g': rw_ln_g, 'rw_ln_b': rw_ln_b,
        'rw_w_o': bf(rw_w_o),
        'xa_w_q': bf(xa_w_q), 'xa_q_gain': xa_q_gain, 'xa_w_o': bf(xa_w_o),
        'ffn_w_in': bf(ffn_w_in), 'ffn_conv_w': ffn_conv_w, 'ffn_conv_b': ffn_conv_b, 'ffn_w_down': bf(ffn_w_down),
    }

    mem_gain = jnp.concatenate([jnp.broadcast_to(xa_k_gain[:, None, :], (DEPTH, XA_HEADS, XA_HEAD_DIM)).reshape(DEPTH, 1, d),
                                jnp.ones((DEPTH, 1, d), F32)], axis=-1)
    mem_kv_flat = _norm_matmul(mem_prompt.reshape(bp * N_MEM, d), mem_norm[:, None, :], bf(xa_w_kv), mem_gain,
                               hn_width=XA_HEAD_DIM, hn_pred=lambda jt: jt < (d // 512))
    mem_kv_prompt = mem_kv_flat.reshape(DEPTH, bp, N_MEM, 2, XA_HEADS, XA_HEAD_DIM)

    y_prompt, a_p, hg_p, rw_p, sh_p, ffn_p = _trunk(
        x_prompt, mem_kv_flat.reshape(DEPTH, bp, N_MEM, 2 * d), None,
        jnp.zeros((n_b, bp, HG_HEADS, HG_DK, HG_DK), F32),
        jnp.zeros((n_c, bp, RW_HEADS, RW_HEAD_DIM, RW_HEAD_DIM), F32),
        jnp.zeros((n_c, bp, d), F32),
        jnp.zeros((DEPTH, bp, CONV_W - 1, D_FF), F32), w, True)
    caches = tuple(cc.reshape(cc.shape[0], cc.shape[1], cc.shape[2], 2 * A_WIDTH)
                   for cc in (cache_attn_kv_w128, cache_attn_kv_w512, cache_attn_kv_w2048))
    y_sample, a_s, hg_s, rw_s, sh_s, ffn_s = _trunk(
        x_sample, cache_mem_kv.reshape(DEPTH, bs, N_MEM, 2 * d), caches,
        state_hgrn, state_rwkv, state_rwkv_shift, state_ffn_conv, w, False)
    return (y_prompt, y_sample, a_p[0], a_p[1], a_p[2], hg_p, rw_p, sh_p, ffn_p, mem_kv_prompt,
            a_s[0], a_s[1], a_s[2], hg_s, rw_s, sh_s, ffn_s)
```

```python
import functools
import math

import numpy as np
import jax
import jax.numpy as jnp
from jax import lax
from jax.experimental import pallas as pl
from jax.experimental.pallas import tpu as pltpu

F32 = jnp.float32
BF16 = jnp.bfloat16

D_MODEL = 1024
DEPTH = 4
N_MIXERS = 3
DIL_WINDOWS = (128, 512, 2048)
DIL_RATES = (1, 4, 16)
N_GROUPS = 3
A_SPAN = 128
A_HEADS = 4
A_HEAD_DIM = 128
A_BLOCK = 128
A_UNROLL = 4
A_WIDTH = A_HEADS * A_HEAD_DIM
KV_ROWS = 2 * A_HEADS
HG_HEADS = 8
HG_DK = 128
HG_CHUNK = 128
RW_HEADS = 16
RW_HEAD_DIM = 64
RW_CHUNK = 64
RW_GN_EPS = 64e-5
N_MEM = 256
XA_HEADS = 4
XA_HEAD_DIM = D_MODEL // XA_HEADS
D_FF = 2816
CONV_W = 3
RMS_EPS = 1e-6
LANES = 128
VMEM_LIMIT = 52 * 1024 * 1024

NT_DIMS = (((1,), (1,)), ((), ()))
TN_DIMS = (((0,), (0,)), ((), ()))


def _params(n_axes):
    return pltpu.CompilerParams(dimension_semantics=("arbitrary",) * n_axes, vmem_limit_bytes=VMEM_LIMIT)


def _rms(x):
    return x * lax.rsqrt(jnp.mean(x * x, axis=-1, keepdims=True) + RMS_EPS)


def _dot(a, b):
    return jnp.dot(a.astype(BF16), b.astype(BF16), preferred_element_type=F32)


def _dot_nt(a, b):
    return lax.dot_general(a.astype(BF16), b.astype(BF16), NT_DIMS, preferred_element_type=F32)


def _dot_tn(a, b):
    return lax.dot_general(a.astype(BF16), b.astype(BF16), TN_DIMS, preferred_element_type=F32)


def _split3(x):
    p1 = x.astype(BF16)
    r1 = x - p1.astype(F32)
    p2 = r1.astype(BF16)
    p3 = (r1 - p2.astype(F32)).astype(BF16)
    return jnp.concatenate([p1, p2, p3], axis=1)


def _exact_rowmix(mat_bf16, x):
    w = x.shape[1]
    y = jnp.dot(mat_bf16, _split3(x), preferred_element_type=F32)
    return y[:, :w] + y[:, w:2 * w] + y[:, 2 * w:]


def _softplus(x):
    return jnp.maximum(x, 0.0) + jnp.log1p(jnp.exp(-jnp.abs(x)))


def _sigmoid(x):
    return 1.0 / (1.0 + jnp.exp(-x))


def _silu(x):
    return x * _sigmoid(x)


def _nmm_body(x_ref, g_ref, w_ref, hg_ref, o_ref, h_scr, *, hn_width, hn_pred):
    j = pl.program_id(2)

    @pl.when(j == 0)
    def _():
        h_scr[...] = (_rms(x_ref[...]) * g_ref[...]).astype(BF16)

    y = jnp.dot(h_scr[...], w_ref[...], preferred_element_type=F32)
    if hn_width is None:
        o_ref[...] = y
        return
    do_norm = hn_pred(j)

    @pl.when(do_norm)
    def _():
        for c in range(y.shape[1] // hn_width):
            sl = slice(c * hn_width, (c + 1) * hn_width)
            o_ref[:, sl] = _rms(y[:, sl]) * hg_ref[:, sl]

    @pl.when(jnp.logical_not(do_norm))
    def _():
        o_ref[...] = y


def _norm_matmul(x, gain, w, head_gain=None, hn_width=None, hn_pred=None, tm=512, tn=512):
    m, d = x.shape
    nl, _, n = w.shape
    tm = min(tm, m)
    if head_gain is None:
        head_gain = jnp.ones((nl, 1, n), F32)
    grid = (nl, m // tm, n // tn)
    return pl.pallas_call(
        functools.partial(_nmm_body, hn_width=hn_width, hn_pred=hn_pred),
        grid=grid,
        in_specs=[
            pl.BlockSpec((tm, d), lambda l, i, j: (i, 0)),
            pl.BlockSpec((None, 1, d), lambda l, i, j: (l, 0, 0)),
            pl.BlockSpec((None, d, tn), lambda l, i, j: (l, 0, j)),
            pl.BlockSpec((None, 1, tn), lambda l, i, j: (l, 0, j)),
        ],
        out_specs=pl.BlockSpec((None, tm, tn), lambda l, i, j: (l, i, j)),
        out_shape=jax.ShapeDtypeStruct((nl, m, n), F32),
        scratch_shapes=[pltpu.VMEM((tm, d), BF16)],
        compiler_params=_params(3),
        name="norm_matmul",
    )(x, gain, w, head_gain)


def _mm_res_body(x_ref, a_ref, w_ref, o_ref):
    o_ref[...] = x_ref[...] + jnp.dot(a_ref[...].astype(BF16), w_ref[...], preferred_element_type=F32)


def _matmul_residual(x, a, w, tm=512):
    m, d = x.shape
    k = a.shape[1]
    tm = min(tm, m)
    return pl.pallas_call(
        _mm_res_body,
        grid=(m // tm,),
        in_specs=[
            pl.BlockSpec((tm, d), lambda i: (i, 0)),
            pl.BlockSpec((tm, k), lambda i: (i, 0)),
            pl.BlockSpec((k, d), lambda i: (0, 0)),
        ],
        out_specs=pl.BlockSpec((tm, d), lambda i: (i, 0)),
        out_shape=jax.ShapeDtypeStruct((m, d), F32),
        compiler_params=_params(1),
        name="matmul_residual",
    )(x, a, w)


def _ds(start, size, stride):
    return pl.ds(start, size) if stride == 1 else pl.ds(start, size, stride=stride)


def _attn_prompt_body(q0, q1, q2, k0, v0, k1, v1, k2, v2, o_ref, acc, m_s, l_s, *, seq):
    scale = A_HEAD_DIM ** -0.5
    blk = A_BLOCK
    for g, (q_ref, k_ref, v_ref) in enumerate(((q0, k0, v0), (q1, k1, v1), (q2, k2, v2))):
        dil = DIL_RATES[g]
        n = seq // dil
        nb = n // blk
        width = 2 * blk if nb > 1 else blk
        ii = lax.broadcasted_iota(jnp.int32, (blk, width), 0)
        jj = lax.broadcasted_iota(jnp.int32, (blk, width), 1)
        if nb > 1:
            in_cur = (jj >= blk) & (jj - blk <= ii)
            in_prev = (jj < blk) & (jj >= ii)
        else:
            in_cur, in_prev = jj <= ii, None

        def body(it, carry, g=g, dil=dil, nb=nb, q_ref=q_ref, k_ref=k_ref, v_ref=v_ref, in_cur=in_cur,
                 in_prev=in_prev):
            rows, ss, vws = [], [], []
            for uu in range(A_UNROLL):
                idx = it * A_UNROLL + uu
                r = idx // nb
                mb = idx % nb
                rw = _ds(r + dil * blk * mb, blk, dil)
                kc = k_ref[rw, :].astype(BF16)
                vc = v_ref[rw, :].astype(BF16)
                if nb > 1:
                    prev = _ds(r + dil * blk * jnp.maximum(mb - 1, 0), blk, dil)
                    kw = jnp.concatenate([k_ref[prev, :].astype(BF16), kc], axis=0)
                    vw = jnp.concatenate([v_ref[prev, :].astype(BF16), vc], axis=0)
                    prev_bias = jnp.where(mb > 0, 0.0, -jnp.inf)
                    bias = jnp.where(in_cur, 0.0, jnp.where(in_prev, prev_bias, -jnp.inf))
                else:
                    kw, vw = kc, vc
                    bias = jnp.where(in_cur, 0.0, -jnp.inf)
                rows.append(rw)
                vws.append(vw)
                ss.append(_dot_nt(q_ref[rw, :], kw) * scale + bias)
            ms = [jnp.max(s, axis=-1, keepdims=True) for s in ss]
            ps = [jnp.exp(s - m) for s, m in zip(ss, ms)]
            ls = [jnp.sum(p, axis=-1, keepdims=True) for p in ps]
            os_ = [_dot(p, vw) for p, vw in zip(ps, vws)]
            m_bs = [jnp.broadcast_to(m, (blk, LANES)) for m in ms]
            l_bs = [jnp.broadcast_to(l, (blk, LANES)) for l in ls]
            if g == 0:
                for rw, o, m_b, l_b in zip(rows, os_, m_bs, l_bs):
                    acc[rw, :] = o
                    m_s[rw, :] = m_b
                    l_s[rw, :] = l_b
            else:
                olds = [(m_s[rw, :], l_s[rw, :], acc[rw, :]) for rw in rows]
                for rw, o, m_b, l_b, (m_old, l_old, acc_old) in zip(rows, os_, m_bs, l_bs, olds):
                    m_new = jnp.maximum(m_old, m_b)
                    a_old = jnp.exp(m_old - m_new)
                    a_cur = jnp.exp(m_b - m_new)
                    acc[rw, :] = acc_old * a_old + o * a_cur
                    l_s[rw, :] = l_old * a_old + l_b * a_cur
                    m_s[rw, :] = m_new
            return carry

        assert (dil * nb) % A_UNROLL == 0
        lax.fori_loop(0, dil * nb // A_UNROLL, body, 0)
    o_ref[...] = acc[...] / l_s[...]


def _attn_prompt(qkv, batch, seq):
    m = batch * seq
    assert seq % (A_BLOCK * max(DIL_RATES)) == 0 and A_BLOCK == A_SPAN
    nq = N_GROUPS * A_HEADS

    def spec(col_fn):
        return pl.BlockSpec((seq, A_HEAD_DIM), lambda b, h: (b, col_fn(h)))

    in_specs = [spec(lambda h, g=g: g * A_HEADS + h) for g in range(N_GROUPS)]
    for g in range(N_GROUPS):
        in_specs.append(spec(lambda h, g=g: nq + g * 2 * A_HEADS + h))
        in_specs.append(spec(lambda h, g=g: nq + g * 2 * A_HEADS + A_HEADS + h))
    return pl.pallas_call(
        functools.partial(_attn_prompt_body, seq=seq),
        grid=(batch, A_HEADS),
        in_specs=in_specs,
        out_specs=pl.BlockSpec((seq, A_HEAD_DIM), lambda b, h: (b, h)),
        out_shape=jax.ShapeDtypeStruct((m, A_WIDTH), F32),
        scratch_shapes=[pltpu.VMEM((seq, A_HEAD_DIM), F32), pltpu.VMEM((seq, LANES), F32),
                        pltpu.VMEM((seq, LANES), F32)],
        compiler_params=_params(2),
        name="dilated_attn_prompt",
    )(*([qkv] * 9))


def _attn_sample_body(*refs, t_new, n_alias):
    qkv_ref, bufs = refs[0], refs[1:4]
    o_ref, outs = refs[4 + n_alias], refs[5 + n_alias:8 + n_alias]
    scale = A_HEAD_DIM ** -0.5
    pad = jnp.zeros((LANES - t_new, A_HEAD_DIM), F32)
    nq = N_GROUPS * A_WIDTH

    def new_rows(g, kv, h):
        base = nq + (2 * g + kv) * A_WIDTH + h * A_HEAD_DIM
        return qkv_ref[:, base:base + A_HEAD_DIM]

    for g in range(N_GROUPS):
        length = bufs[g].shape[0] // KV_ROWS
        keep = (length - t_new) * KV_ROWS
        outs[g][0:keep, :] = bufs[g][t_new * KV_ROWS:length * KV_ROWS, :]
        for kv in range(2):
            for h in range(A_HEADS):
                outs[g][pl.ds(keep + kv * A_HEADS + h, t_new, stride=KV_ROWS), :] = new_rows(g, kv, h)

    for h in range(A_HEADS):
        parts, vals = [], []
        for g in range(N_GROUPS):
            dil = DIL_RATES[g]
            length = bufs[g].shape[0] // KV_ROWS
            q = qkv_ref[:, g * A_WIDTH + h * A_HEAD_DIM: g * A_WIDTH + (h + 1) * A_HEAD_DIM]
            s_buf = _dot_nt(q, bufs[g][pl.ds(h, length, stride=KV_ROWS), :]) * scale
            s_new = _dot_nt(q, jnp.concatenate([new_rows(g, 0, h), pad], axis=0)) * scale
            for s, base in ((s_buf, 0), (s_new, length)):
                ii = lax.broadcasted_iota(jnp.int32, s.shape, 0)
                jj = lax.broadcasted_iota(jnp.int32, s.shape, 1) + base
                dd = jj - ii
                valid = (dd >= 0) & (dd <= A_SPAN * dil) & ((dd & (dil - 1)) == 0) & (jj < length + t_new)
                parts.append(jnp.where(valid, s, -jnp.inf))
            vals.append(bufs[g][pl.ds(A_HEADS + h, length, stride=KV_ROWS), :])
            vals.append(jnp.concatenate([new_rows(g, 1, h), pad], axis=0))
        mx = functools.reduce(jnp.maximum, [jnp.max(s, axis=-1, keepdims=True) for s in parts])
        exps = [jnp.exp(s - mx) for s in parts]
        den = functools.reduce(lambda a, b: a + b, [jnp.sum(e, axis=-1, keepdims=True) for e in exps])
        o = functools.reduce(lambda a, b: a + b, [_dot(e / den, vv) for e, vv in zip(exps, vals)])
        o_ref[:, h * A_HEAD_DIM:(h + 1) * A_HEAD_DIM] = o


def _attn_sample(qkv, caches, prev_out, j, batch, t_new):
    assert t_new % 8 == 0 and t_new <= LANES
    lens = [cch.shape[2] // KV_ROWS for cch in caches]
    for g in range(N_GROUPS):
        assert lens[g] == A_SPAN * DIL_RATES[g] and (DIL_RATES[g] & (DIL_RATES[g] - 1)) == 0

    def cache_spec(g):
        return pl.BlockSpec((None, None, lens[g] * KV_ROWS, A_HEAD_DIM), lambda b: (j, b, 0, 0))

    in_specs = [pl.BlockSpec((t_new, qkv.shape[1]), lambda b: (b, 0))] + [cache_spec(g) for g in range(N_GROUPS)]
    n_alias = 0 if prev_out is None else N_GROUPS
    aliases = {}
    if prev_out is not None:
        in_specs += [pl.BlockSpec(memory_space=pl.ANY)] * N_GROUPS
        aliases = {4 + g: 1 + g for g in range(N_GROUPS)}
    out_specs = [pl.BlockSpec((t_new, A_WIDTH), lambda b: (b, 0))] + [cache_spec(g) for g in range(N_GROUPS)]
    out_shape = [jax.ShapeDtypeStruct((batch * t_new, A_WIDTH), F32)]
    out_shape += [jax.ShapeDtypeStruct(cch.shape, F32) for cch in caches]
    return pl.pallas_call(
        functools.partial(_attn_sample_body, t_new=t_new, n_alias=n_alias),
        grid=(batch,),
        in_specs=in_specs,
        out_specs=out_specs,
        out_shape=out_shape,
        input_output_aliases=aliases,
        compiler_params=_params(1),
        name="dilated_attn_sample",
    )(qkv, *caches, *(prev_out or ()))


def _kv_cache_body(*refs):
    k_ref, v_ref, o_ref = refs[0], refs[1], refs[-1]
    tr = k_ref.shape[0]
    for kv, x_ref in enumerate((k_ref, v_ref)):
        for h in range(A_HEADS):
            o_ref[pl.ds(kv * A_HEADS + h, tr, stride=KV_ROWS), :] = x_ref[:, h * A_HEAD_DIM:(h + 1) * A_HEAD_DIM]


def _kv_cache_out(qkv, prev_out, g, j, n_layers, batch, seq):
    win = min(DIL_WINDOWS[g], seq)
    tr = min(win, 512)
    first = (seq - win) // tr
    row = lambda b, t: b * (seq // tr) + first + t
    in_specs = [pl.BlockSpec((tr, A_WIDTH), lambda b, t: (row(b, t), N_GROUPS + 2 * g)),
                pl.BlockSpec((tr, A_WIDTH), lambda b, t: (row(b, t), N_GROUPS + 2 * g + 1))]
    aliases = {}
    if prev_out is not None:
        in_specs.append(pl.BlockSpec(memory_space=pl.ANY))
        aliases = {2: 0}
    return pl.pallas_call(
        _kv_cache_body,
        grid=(batch, win // tr),
        in_specs=in_specs,
        out_specs=pl.BlockSpec((None, None, tr * KV_ROWS, A_HEAD_DIM), lambda b, t: (j, b, t, 0)),
        out_shape=jax.ShapeDtypeStruct((n_layers, batch, win * KV_ROWS, A_HEAD_DIM), F32),
        input_output_aliases=aliases,
        compiler_params=_params(2),
        name="kv_cache_out",
    )(qkv, qkv, *(() if prev_out is None else (prev_out,)))


def _hgrn_consts(c):
    t = np.arange(c)[:, None]
    u = np.arange(c)[None, :]
    mats = [(u <= t), (u > t)]
    masks = [(u == t)]
    s = 1
    while s < c:
        upper = (t // s) % 2 == 1
        mid = (t // (2 * s)) * 2 * s + s - 1
        mats.append(np.where(upper, (u > mid) & (u <= t), (u > t) & (u <= mid)))
        lower_u = (u // s) % 2 == 0
        masks.append(upper & lower_u & (t // (2 * s) == u // (2 * s)))
        s *= 2
    mats = np.concatenate([m.astype(np.float32) for m in mats], axis=0)
    masks = np.stack([m.astype(np.float32) for m in masks], axis=0)
    return jnp.asarray(mats, BF16), jnp.asarray(masks, F32)


def _hgrn_body(q_ref, z_ref, i_ref, g_ref, lbl_ref, og_ref, s0_ref, mat_ref, msk_ref, o_ref, st_ref, st_scr,
               *, layer, t_len):
    c = HG_CHUNK
    n_lvl = msk_ref.shape[0] - 1
    lg = lbl_ref[...]
    e = jnp.exp(lg - jnp.max(lg, axis=0, keepdims=True))
    prob = e / jnp.sum(e, axis=0, keepdims=True)
    lb = jnp.sum(prob[0:layer + 1], axis=0, keepdims=True) - prob[0:1]
    log_lb = jnp.log(lb)
    log_1m = jnp.log1p(-lb)
    st_scr[...] = s0_ref[...].T
    row = lax.broadcasted_iota(jnp.int32, (c, HG_DK), 0)

    def chunk(ci, carry):
        if t_len >= c:
            rows = pl.ds(pl.multiple_of(ci * c, c), c)
            ld = lambda ref: ref[rows, :]
            live = None
        else:
            padz = jnp.zeros((c - t_len, HG_DK), F32)
            ld = lambda ref: jnp.concatenate([ref[...], padz], axis=0)
            live = row < t_len
        z = ld(z_ref)
        lsig = jnp.minimum(z, 0.0) - jnp.log1p(jnp.exp(-jnp.abs(z)))
        a1 = jnp.broadcast_to(log_lb, z.shape)
        a2 = log_1m + lsig
        hi = jnp.maximum(a1, a2)
        lo = jnp.minimum(a1, a2)
        log_f = hi + jnp.log1p(jnp.exp(lo - hi))
        kg = (1.0 - lb) * _sigmoid(-z)
        if live is not None:
            log_f = jnp.where(live, log_f, 0.0)
            kg = jnp.where(live, kg, 0.0)
        qa = _silu(ld(q_ref))
        vv = ld(i_ref)
        ex = _exact_rowmix(mat_ref[...], log_f)
        b = ex[0:c]
        st = st_scr[...]
        a = _dot_nt(qa, kg) * msk_ref[0]
        for lv in range(n_lvl):
            w = jnp.exp(ex[(2 + lv) * c:(3 + lv) * c])
            a = a + _dot_nt(qa * w, kg * w) * msk_ref[1 + lv]
        o = _dot_nt(qa * jnp.exp(b), st) + _dot(a, vv)
        st_scr[...] = st * jnp.exp(b[c - 1:c]) + _dot_tn(vv, kg * jnp.exp(ex[c:2 * c]))
        y = _rms(o) * og_ref[...] * _silu(ld(g_ref))
        if t_len >= c:
            o_ref[rows, :] = y
        else:
            o_ref[...] = y[0:t_len]
        return carry

    lax.fori_loop(0, max(t_len // c, 1), chunk, 0)
    st_ref[...] = st_scr[...].T


def _hgrn(proj, lb_logits, out_gain, s0, layer, batch, t_len):
    mats, masks = _hgrn_consts(HG_CHUNK)
    assert t_len % HG_CHUNK == 0 or (t_len < HG_CHUNK and t_len % 8 == 0)

    def col(part):
        return pl.BlockSpec((t_len, HG_DK), lambda b, h: (b, part * HG_HEADS + h))

    return pl.pallas_call(
        functools.partial(_hgrn_body, layer=layer, t_len=t_len),
        grid=(batch, HG_HEADS),
        in_specs=[col(0), col(1), col(2), col(3),
                  pl.BlockSpec((DEPTH, HG_DK), lambda b, h: (0, h)),
                  pl.BlockSpec((1, HG_DK), lambda b, h: (0, 0)),
                  pl.BlockSpec((None, None, HG_DK, HG_DK), lambda b, h: (b, h, 0, 0)),
                  pl.BlockSpec(mats.shape, lambda b, h: (0, 0)),
                  pl.BlockSpec(masks.shape, lambda b, h: (0, 0, 0))],
        out_specs=[pl.BlockSpec((t_len, HG_DK), lambda b, h: (b, h)),
                   pl.BlockSpec((None, None, HG_DK, HG_DK), lambda b, h: (b, h, 0, 0))],
        out_shape=[jax.ShapeDtypeStruct((batch * t_len, HG_HEADS * HG_DK), F32),
                   jax.ShapeDtypeStruct((batch, HG_HEADS, HG_DK, HG_DK), F32)],
        scratch_shapes=[pltpu.VMEM((HG_DK, HG_DK), F32)],
        compiler_params=_params(2),
        name="hgrn2_chunked",
    )(proj, proj, proj, proj, lb_logits, out_gain, s0, mats, masks)


def _rwkv_proj_body(x_ref, g_ref, sh_ref, mu_ref, wrkv_ref, w0_ref, w1_ref, w2_ref, a0_ref, a1_ref, a2_ref,
                    g1_ref, g2_ref, kk_ref, ka_ref,
                    r_o, k_o, v_o, ld_o, kk_o, a_o, g_o, sh_o, carry, *, bb, tt):
    t = pl.program_id(1)
    d = D_MODEL

    @pl.when(t == 0)
    def _():
        carry[...] = sh_ref[...]

    h3 = _rms(x_ref[...]) * g_ref[...]
    h = h3.reshape(bb * tt, d)
    rolled = pltpu.roll(h, 1, 0).reshape(bb, tt, d)
    tpos = lax.broadcasted_iota(jnp.int32, (bb, tt, d), 1)
    prev = jnp.where(tpos == 0, carry[...], rolled).reshape(bb * tt, d)
    last = h3[:, tt - 1:tt, :]
    carry[...] = last
    sh_o[...] = last
    dx = prev - h
    mix = lambda jm: h + dx * mu_ref[jm:jm + 1, :]
    r = _dot(mix(0), wrkv_ref[0])
    k = _dot(mix(1), wrkv_ref[1])
    v = _dot(mix(2), wrkv_ref[2])
    wl = w0_ref[...] + _dot(jnp.tanh(_dot(mix(3), w1_ref[...])), w2_ref[...])
    wlog = -_softplus(-wl) - 0.5
    a = _sigmoid(a0_ref[...] + _dot(_dot(mix(4), a1_ref[...]), a2_ref[...]))
    gate = _dot(_sigmoid(_dot(mix(5), g1_ref[...])), g2_ref[...])
    r_o[...] = r
    k_o[...] = k * (1.0 + (a - 1.0) * ka_ref[...])
    v_o[...] = v
    ld_o[...] = -jnp.exp(wlog)
    kk_o[...] = k * kk_ref[...]
    a_o[...] = a
    g_o[...] = gate


def _rwkv_proj(x3, gain, shift, mu, wrkv, w0, w1, w2, a0, a1, a2, g1, g2, k_k, k_a, bb, tt):
    batch, t_len, d = x3.shape
    m = batch * t_len
    rows = bb * tt
    full = lambda arr: pl.BlockSpec(arr.shape, lambda b, t: (0,) * arr.ndim)
    row_spec = pl.BlockSpec((rows, d), lambda b, t: (b * (t_len // tt) + t, 0))
    outs = pl.pallas_call(
        functools.partial(_rwkv_proj_body, bb=bb, tt=tt),
        grid=(batch // bb, t_len // tt),
        in_specs=[pl.BlockSpec((bb, tt, d), lambda b, t: (b, t, 0)), full(gain),
                  pl.BlockSpec((bb, 1, d), lambda b, t: (b, 0, 0)), full(mu), full(wrkv),
                  full(w0), full(w1), full(w2), full(a0), full(a1), full(a2), full(g1), full(g2),
                  full(k_k), full(k_a)],
        out_specs=[row_spec] * 7 + [pl.BlockSpec((bb, 1, d), lambda b, t: (b, 0, 0))],
        out_shape=[jax.ShapeDtypeStruct((m, d), F32)] * 7 + [jax.ShapeDtypeStruct((batch, 1, d), F32)],
        scratch_shapes=[pltpu.VMEM((bb, 1, d), F32)],
        compiler_params=_params(2),
        name="rwkv_proj",
    )(x3, gain, shift, mu, wrkv, w0, w1, w2, a0, a1, a2, g1, g2, k_k, k_a)
    return outs


def _rwkv_consts(c):
    n = RW_HEAD_DIM
    t = np.arange(c)[:, None]
    u = np.arange(c)[None, :]
    f = lambda mat, dt: jnp.asarray(np.asarray(mat).astype(np.float32), dt)
    tri = f(u <= t, BF16)
    lo = f(np.concatenate([u < t, u <= t], axis=0), F32)
    eye = f(u == t, F32)
    ch = np.arange(D_MODEL)[:, None] // n
    col = np.arange(LANES)[None, :]
    seg = f(ch == col, BF16)
    seg_t = f((ch == col).T, BF16)
    lane = np.arange(LANES)
    bdiag = f((lane[:, None] // n) == (lane[None, :] // n), F32)
    return tri, lo, eye, seg, seg_t, bdiag


def _split3_rows(x):
    p1 = x.astype(BF16)
    r1 = x - p1.astype(F32)
    p2 = r1.astype(BF16)
    p3 = (r1 - p2.astype(F32)).astype(BF16)
    return jnp.concatenate([p1, p2, p3], axis=0)


def _exact_colmix(x, mat_bf16):
    rws = x.shape[0]
    y = jnp.dot(_split3_rows(x), mat_bf16, preferred_element_type=F32)
    return y[0:rws] + y[rws:2 * rws] + y[2 * rws:]


def _rwkv_rec_body(r_ref, k_ref, v_ref, ld_ref, kk_ref, a_ref, g_ref, rk_ref, lng_ref, lnb_ref, s0_ref,
                   tri_ref, lo_ref, eye_ref, seg_ref, segt_ref, bd_ref, o_ref, st_ref, st_scr, *, tt):
    c = RW_CHUNK
    n = RW_HEAD_DIM
    heads = RW_HEADS
    pairs = heads // 2
    w = D_MODEL
    t = pl.program_id(1)

    @pl.when(t == 0)
    def _():
        zero = jnp.zeros((n, n), F32)
        for p in range(pairs):
            top = jnp.concatenate([s0_ref[2 * p], zero], axis=1)
            bot = jnp.concatenate([zero, s0_ref[2 * p + 1]], axis=1)
            st_scr[p] = jnp.concatenate([top, bot], axis=0)

    low_2c = lax.broadcasted_iota(jnp.int32, (2 * c, LANES), 1) < n
    low_c = lax.broadcasted_iota(jnp.int32, (c, LANES), 1) < n
    head_sums = lambda x: _exact_colmix(_exact_colmix(x, seg_ref[...]), segt_ref[...])
    tile = lambda x, p: x[:, p * LANES:(p + 1) * LANES]
    hr = range(heads)
    pr = range(pairs)

    def chunk(ci, carry):
        if tt >= c:
            rows = pl.ds(pl.multiple_of(ci * c, c), c)
            ld = lambda ref: ref[rows, :]
        else:
            padz = jnp.zeros((c - tt, w), F32)
            ld = lambda ref: jnp.concatenate([ref[...], padz], axis=0)
        r, kf, v, lc, kk, a, gate = (ld(ref) for ref in (r_ref, k_ref, v_ref, ld_ref, kk_ref, a_ref, g_ref))
        kk = kk / jnp.maximum(jnp.sqrt(head_sums(kk * kk)), 1e-12)
        bvec = kk * a
        cum = _exact_rowmix(tri_ref[...], lc)
        c_end = cum[c - 1:c]
        e_neg = jnp.exp(-cum)
        e_end = jnp.exp(c_end - cum)
        d_end = jnp.exp(c_end)
        kr = jnp.concatenate([kk * jnp.exp(cum - lc), r * jnp.exp(cum)], axis=0)
        bet = bvec * e_neg
        chi = kf * e_neg
        bk = jnp.concatenate([bvec * e_end, kf * e_end], axis=0)
        lo = lo_ref[...]
        eye = eye_ref[...]
        krm = [jnp.where(low_2c if h % 2 == 0 else jnp.logical_not(low_2c), tile(kr, h // 2), 0.0) for h in hr]
        ab = [_dot_nt(krm[h], tile(bet, h // 2)) * lo for h in hr]
        ak = [_dot_nt(krm[h], tile(chi, h // 2)) * lo for h in hr]
        nn = [x[0:c] for x in ab]
        tm = [eye - x for x in nn]
        npow = [_dot(x, x) for x in nn]
        p2 = 2
        while p2 < c:
            tm = [x + _dot(x, y) for x, y in zip(tm, npow)]
            p2 *= 2
            if p2 < c:
                npow = [_dot(x, x) for x in npow]
        sts = [st_scr[p] for p in pr]
        ks = [_dot_nt(tile(kr, p), sts[p]) for p in pr]
        av = [jnp.where(low_2c, _dot(ak[2 * p], tile(v, p)), _dot(ak[2 * p + 1], tile(v, p))) for p in pr]
        rhs = [ks[p][0:c] + av[p][0:c] for p in pr]
        u = [jnp.where(low_c, -_dot(tm[2 * p], rhs[p]), -_dot(tm[2 * p + 1], rhs[p])) for p in pr]
        y = [ks[p][c:2 * c] + av[p][c:2 * c]
             + jnp.where(low_c, _dot(ab[2 * p][c:2 * c], u[p]), _dot(ab[2 * p + 1][c:2 * c], u[p])) for p in pr]
        new_st = [sts[p] * tile(d_end, p)
                  + _dot_tn(jnp.concatenate([u[p], tile(v, p)], axis=0), tile(bk, p)) * bd_ref[...] for p in pr]
        st_scr[...] = jnp.stack(new_st, axis=0)
        yy = jnp.concatenate(y, axis=1)
        yc = yy - head_sums(yy) * (1.0 / n)
        var = head_sums(yc * yc) * (1.0 / n)
        yn = yc * lax.rsqrt(var + RW_GN_EPS) * lng_ref[...] + lnb_ref[...]
        z = (yn + head_sums(r * kf * rk_ref[...]) * v) * gate
        if tt >= c:
            o_ref[rows, :] = z
        else:
            o_ref[...] = z[0:tt]
        return carry

    lax.fori_loop(0, max(tt // c, 1), chunk, 0)
    for p in range(pairs):
        blk = st_scr[p]
        st_ref[2 * p] = blk[0:n, 0:n]
        st_ref[2 * p + 1] = blk[n:2 * n, n:2 * n]


def _rwkv_rec(proj, r_k, ln_g, ln_b, s0, batch, t_len, tt=256):
    r, kf, v, ld, kk, a, gate = proj
    tt = min(tt, t_len)
    assert tt % RW_CHUNK == 0 or (tt < RW_CHUNK and tt % 8 == 0)
    assert RW_HEADS * RW_HEAD_DIM == D_MODEL and 2 * RW_HEAD_DIM == LANES and 2 * RW_CHUNK == LANES
    consts = _rwkv_consts(RW_CHUNK)
    nt = t_len // tt
    row_spec = pl.BlockSpec((tt, D_MODEL), lambda b, t: (b * nt + t, 0))
    par_spec = pl.BlockSpec((1, D_MODEL), lambda b, t: (0, 0))
    st_spec = pl.BlockSpec((None, RW_HEADS, RW_HEAD_DIM, RW_HEAD_DIM), lambda b, t: (b, 0, 0, 0))
    const = lambda arr: pl.BlockSpec(arr.shape, lambda b, t: (0, 0))
    return pl.pallas_call(
        functools.partial(_rwkv_rec_body, tt=tt),
        grid=(batch, nt),
        in_specs=[row_spec] * 7 + [par_spec] * 3 + [st_spec] + [const(x) for x in consts],
        out_specs=[row_spec, st_spec],
        out_shape=[jax.ShapeDtypeStruct((batch * t_len, D_MODEL), F32),
                   jax.ShapeDtypeStruct((batch, RW_HEADS, RW_HEAD_DIM, RW_HEAD_DIM), F32)],
        scratch_shapes=[pltpu.VMEM((RW_HEADS // 2, LANES, LANES), F32)],
        compiler_params=_params(2),
        name="rwkv7_chunked",
    )(r, kf, v, ld, kk, a, gate, r_k, ln_g, ln_b, s0, *consts)


def _xattn_body(x_ref, g_ref, wq_ref, qg_ref, kv_ref, wo_ref, o_ref, *, bb, tt, native):
    d = D_MODEL
    e = XA_HEAD_DIM
    scale = XA_HEAD_DIM ** -0.5
    x = x_ref[...].reshape(bb * tt, d)
    q = jnp.dot((_rms(x) * g_ref[...]).astype(BF16), wq_ref[...], preferred_element_type=F32)
    if native:
        key = lambda b, h: kv_ref[b, :, 0, h, :]
        val = lambda b, h: kv_ref[b, :, 1, h, :]
    else:
        key = lambda b, h: kv_ref[b, :, h * e:(h + 1) * e]
        val = lambda b, h: kv_ref[b, :, d + h * e:d + (h + 1) * e]
    per_batch = []
    for b in range(bb):
        heads = []
        for h in range(XA_HEADS):
            qh = _rms(q[b * tt:(b + 1) * tt, h * e:(h + 1) * e]) * qg_ref[...]
            s = _dot_nt(qh, key(b, h)) * scale
            p = jnp.exp(s - jnp.max(s, axis=-1, keepdims=True))
            p = p / jnp.sum(p, axis=-1, keepdims=True)
            heads.append(_dot(p, val(b, h)))
        per_batch.append(jnp.concatenate(heads, axis=-1))
    o = jnp.concatenate(per_batch, axis=0) if bb > 1 else per_batch[0]
    y = x + jnp.dot(o.astype(BF16), wo_ref[...], preferred_element_type=F32)
    o_ref[...] = y.reshape(bb, tt, d)


def _xattn(x3, gain, wq, q_gain, kv_all, layer, wo, bb, tt):
    batch, t_len, d = x3.shape
    native = kv_all.ndim == 6
    if native:
        kv_spec = pl.BlockSpec((None, bb, N_MEM, 2, XA_HEADS, XA_HEAD_DIM), lambda b, t: (layer, b, 0, 0, 0, 0))
    else:
        kv_spec = pl.BlockSpec((None, bb, N_MEM, 2 * d), lambda b, t: (layer, b, 0, 0))
    return pl.pallas_call(
        functools.partial(_xattn_body, bb=bb, tt=tt, native=native),
        grid=(batch // bb, t_len // tt),
        in_specs=[pl.BlockSpec((bb, tt, d), lambda b, t: (b, t, 0)),
                  pl.BlockSpec((1, d), lambda b, t: (0, 0)),
                  pl.BlockSpec((d, d), lambda b, t: (0, 0)),
                  pl.BlockSpec((1, XA_HEAD_DIM), lambda b, t: (0, 0)),
                  kv_spec,
                  pl.BlockSpec((d, d), lambda b, t: (0, 0))],
        out_specs=pl.BlockSpec((bb, tt, d), lambda b, t: (b, t, 0)),
        out_shape=jax.ShapeDtypeStruct((batch, t_len, d), F32),
        compiler_params=_params(2),
        name="memory_xattn",
    )(x3, gain, wq, q_gain, kv_all, wo)


def _memkv_body(x_ref, g_ref, w_ref, kg_ref, o_ref, flat_ref, h_scr):
    c = pl.program_id(2)
    e = XA_HEAD_DIM

    @pl.when(c == 0)
    def _():
        h_scr[...] = (_rms(x_ref[...]) * g_ref[...]).astype(BF16)

    y = jnp.dot(h_scr[...], w_ref[...], preferred_element_type=F32)

    @pl.when(c == 0)
    def _():
        for h in range(XA_HEADS):
            kn = _rms(y[:, h * e:(h + 1) * e]) * kg_ref[...]
            o_ref[:, h, :] = kn
            flat_ref[:, h * e:(h + 1) * e] = kn

    @pl.when(c == 1)
    def _():
        flat_ref[...] = y
        for h in range(XA_HEADS):
            o_ref[:, h, :] = y[:, h * e:(h + 1) * e]


def _memory_kv(mem2, mem_norm, w_kv, k_gain, batch):
    d = D_MODEL
    return pl.pallas_call(
        _memkv_body,
        grid=(DEPTH, batch, 2),
        in_specs=[pl.BlockSpec((N_MEM, d), lambda l, b, c: (b, 0)),
                  pl.BlockSpec((None, 1, d), lambda l, b, c: (l, 0, 0)),
                  pl.BlockSpec((None, d, d), lambda l, b, c: (l, 0, c)),
                  pl.BlockSpec((None, 1, XA_HEAD_DIM), lambda l, b, c: (l, 0, 0))],
        out_specs=[pl.BlockSpec((None, None, N_MEM, None, XA_HEADS, XA_HEAD_DIM), lambda l, b, c: (l, b, 0, c, 0, 0)),
                   pl.BlockSpec((None, None, N_MEM, d), lambda l, b, c: (l, b, 0, c))],
        out_shape=[jax.ShapeDtypeStruct((DEPTH, batch, N_MEM, 2, XA_HEADS, XA_HEAD_DIM), F32),
                   jax.ShapeDtypeStruct((DEPTH, batch, N_MEM, 2 * d), F32)],
        scratch_shapes=[pltpu.VMEM((N_MEM, d), BF16)],
        compiler_params=_params(3),
        name="memory_kv",
    )(mem2, mem_norm, w_kv, k_gain)


def _ffn_body(x_ref, g_ref, wu_ref, wg_ref, cw_ref, cb_ref, wd_ref, pin_ref, o_ref, st_ref, h_scr, acc, carry,
              *, bb, tt, nf):
    t = pl.program_id(1)
    f = pl.program_id(2)
    d = D_MODEL
    tf = wu_ref.shape[1]

    @pl.when(f == 0)
    def _():
        x = x_ref[...].reshape(bb * tt, d)
        h_scr[...] = (_rms(x) * g_ref[...]).astype(BF16)
        acc[...] = x

    @pl.when(t == 0)
    def _():
        carry[f] = pin_ref[...]

    h = h_scr[...]
    u = jnp.dot(h, wu_ref[...], preferred_element_type=F32)
    gate = jnp.dot(h, wg_ref[...], preferred_element_type=F32)
    prev = carry[f]
    tpos = lax.broadcasted_iota(jnp.int32, (bb, tt, tf), 1)
    u3 = u.reshape(bb, tt, tf)
    u1 = jnp.where(tpos == 0, prev[:, 1:2, :], pltpu.roll(u, 1, 0).reshape(bb, tt, tf))
    u2 = pltpu.roll(u, 2, 0).reshape(bb, tt, tf)
    u2 = jnp.where(tpos == 0, prev[:, 0:1, :], jnp.where(tpos == 1, prev[:, 1:2, :], u2))
    conv = cb_ref[...] + cw_ref[0:1, :] * u2 + cw_ref[1:2, :] * u1 + cw_ref[2:3, :] * u3
    act = (_silu(conv) * gate.reshape(bb, tt, tf)).reshape(bb * tt, tf)
    acc[...] += jnp.dot(act.astype(BF16), wd_ref[...], preferred_element_type=F32)
    tail = u3[:, tt - 2:tt, :]
    carry[f] = tail
    st_ref[:, :, pl.ds(pl.multiple_of(f * tf, tf), tf)] = tail

    @pl.when(f == nf - 1)
    def _():
        o_ref[...] = acc[...].reshape(bb, tt, d)


def _ffn(x3, gain, w_in, conv_w, conv_b, w_down, state_all, layer, bb, tt, tf=256):
    batch, t_len, d = x3.shape
    nf = D_FF // tf
    assert t_len >= CONV_W - 1 and tt >= CONV_W - 1
    return pl.pallas_call(
        functools.partial(_ffn_body, bb=bb, tt=tt, nf=nf),
        grid=(batch // bb, t_len // tt, nf),
        in_specs=[pl.BlockSpec((bb, tt, d), lambda b, t, f: (b, t, 0)),
                  pl.BlockSpec((1, d), lambda b, t, f: (0, 0)),
                  pl.BlockSpec((d, tf), lambda b, t, f: (0, f)),
                  pl.BlockSpec((d, tf), lambda b, t, f: (0, nf + f)),
                  pl.BlockSpec((CONV_W, tf), lambda b, t, f: (0, f)),
                  pl.BlockSpec((1, tf), lambda b, t, f: (0, f)),
                  pl.BlockSpec((tf, d), lambda b, t, f: (f, 0)),
                  pl.BlockSpec((None, bb, CONV_W - 1, tf), lambda b, t, f: (layer, b, 0, f))],
        out_specs=[pl.BlockSpec((bb, tt, d), lambda b, t, f: (b, t, 0)),
                   pl.BlockSpec((bb, CONV_W - 1, D_FF), lambda b, t, f: (b, 0, 0))],
        out_shape=[jax.ShapeDtypeStruct((batch, t_len, d), F32),
                   jax.ShapeDtypeStruct((batch, CONV_W - 1, D_FF), F32)],
        scratch_shapes=[pltpu.VMEM((bb * tt, d), BF16), pltpu.VMEM((bb * tt, d), F32),
                        pltpu.VMEM((nf, bb, CONV_W - 1, tf), F32)],
        compiler_params=_params(3),
        name="conv_ffn",
    )(x3, gain, w_in, w_in, conv_w, conv_b, w_down, state_all)


def _trunk(x3, mem_kv, a_bufs, hg_s, rw_s, rw_shift, ffn_buf, w, prompt):
    batch, t_len, d = x3.shape
    m = batch * t_len
    n_a = w['attn_w_qkv'].shape[0]
    new_a = None
    new_hg, new_rw, new_sh, new_ffn = [], [], [], []
    if prompt:
        xa_bb, xa_tt = 1, 512
        ff_bb, ff_tt = 1, 1024
        rw_bb, rw_tt = 1, 256
    else:
        xa_bb, xa_tt = 4, t_len
        ff_bb, ff_tt = batch, t_len
        rw_bb, rw_tt = batch, t_len
    for i in range(DEPTH):
        kind, j = i % N_MIXERS, i // N_MIXERS
        x2 = x3.reshape(m, d)
        g_mix = w['norm_mix'][i][None, None]
        if kind == 0:
            qkv = _norm_matmul(x2, g_mix, w['attn_w_qkv'][j][None], w['attn_head_gain'][j][None],
                               hn_width=A_HEAD_DIM, hn_pred=lambda jt: (jt < 3) | (jt % 2 == 1))[0]
            if prompt:
                o = _attn_prompt(qkv, batch, t_len)
                new_a = tuple(_kv_cache_out(qkv, None if new_a is None else new_a[gi], gi, j, n_a, batch, t_len)
                              for gi in range(N_GROUPS))
            else:
                o, c0, c1, c2 = _attn_sample(qkv, a_bufs, new_a, j, batch, t_len)
                new_a = (c0, c1, c2)
            x2 = _matmul_residual(x2, o, w['attn_w_o'][j])
        elif kind == 1:
            proj = _norm_matmul(x2, g_mix, w['hg_w_in'][j][None])[0]
            o, st = _hgrn(proj, w['hg_lb_logits'], w['hg_out_gain'][j][None], hg_s[j], i, batch, t_len)
            new_hg.append(st)
            x2 = _matmul_residual(x2, o, w['hg_w_o'][j])
        else:
            proj = _rwkv_proj(x3, w['norm_mix'][i][None], rw_shift[j][:, None, :], w['rw_mu'][j], w['rw_w_rkv'][j],
                              w['rw_w0'][j][None], w['rw_w1'][j], w['rw_w2'][j], w['rw_a0'][j][None],
                              w['rw_a1'][j], w['rw_a2'][j], w['rw_g1'][j], w['rw_g2'][j],
                              w['rw_k_k'][j][None], w['rw_k_a'][j][None], rw_bb, rw_tt)
            o, st = _rwkv_rec(proj[:7], w['rw_r_k'][j].reshape(1, d), w['rw_ln_g'][j][None], w['rw_ln_b'][j][None],
                              rw_s[j], batch, t_len)
            new_rw.append(st)
            new_sh.append(proj[7][:, 0, :])
            x2 = _matmul_residual(x2, o, w['rw_w_o'][j])
        x3 = x2.reshape(batch, t_len, d)
        x3 = _xattn(x3, w['norm_mem'][i][None], w['xa_w_q'][i], w['xa_q_gain'][i][None], mem_kv, i,
                    w['xa_w_o'][i], xa_bb, xa_tt)
        x3, fb = _ffn(x3, w['norm_ffn'][i][None], w['ffn_w_in'][i], w['ffn_conv_w'][i], w['ffn_conv_b'][i][None],
                      w['ffn_w_down'][i], ffn_buf, i, ff_bb, ff_tt)
        new_ffn.append(fb)
    return (x3, new_a, jnp.stack(new_hg), jnp.stack(new_rw), jnp.stack(new_sh), jnp.stack(new_ffn))


def kernel(x_prompt, x_sample, mem_prompt, cache_attn_kv_w128, cache_attn_kv_w512, cache_attn_kv_w2048, state_hgrn, state_rwkv, state_rwkv_shift, state_ffn_conv, cache_mem_kv, norm_mix, norm_mem, norm_ffn, mem_norm, attn_w_qkv, attn_q_gain, attn_k_gain, attn_w_o, hg_w_in, hg_lb_logits, hg_out_gain, hg_w_o, rw_mu, rw_w_rkv, rw_w0, rw_w1, rw_w2, rw_a0, rw_a1, rw_a2, rw_g1, rw_g2, rw_k_k, rw_k_a, rw_r_k, rw_ln_g, rw_ln_b, rw_w_o, xa_w_q, xa_w_kv, xa_q_gain, xa_k_gain, xa_w_o, ffn_w_in, ffn_conv_w, ffn_conv_b, ffn_w_down):
    d = D_MODEL
    bp = x_prompt.shape[0]
    bs = x_sample.shape[0]
    n_a, n_b, n_c = attn_w_qkv.shape[0], hg_w_in.shape[0], rw_w_rkv.shape[0]
    bf = lambda a: a.astype(BF16)

    wq4 = attn_w_qkv.reshape(n_a, d, 3, N_GROUPS, A_WIDTH)
    w_qkv = jnp.concatenate([wq4[:, :, 0].reshape(n_a, d, N_GROUPS * A_WIDTH),
                             jnp.swapaxes(wq4[:, :, 1:3], 2, 3).reshape(n_a, d, N_GROUPS * 2 * A_WIDTH)], axis=-1)
    tile_h = lambda gn: jnp.broadcast_to(gn[:, :, None, :], (n_a, N_GROUPS, A_HEADS, A_HEAD_DIM))
    kgain = jnp.stack([tile_h(attn_k_gain), jnp.ones((n_a, N_GROUPS, A_HEADS, A_HEAD_DIM), F32)], axis=2)
    head_gain = jnp.concatenate([tile_h(attn_q_gain).reshape(n_a, 1, -1), kgain.reshape(n_a, 1, -1)], axis=-1)

    w = {
        'norm_mix': norm_mix, 'norm_mem': norm_mem, 'norm_ffn': norm_ffn,
        'attn_w_qkv': bf(w_qkv), 'attn_head_gain': head_gain, 'attn_w_o': bf(attn_w_o),
        'hg_w_in': bf(hg_w_in), 'hg_lb_logits': hg_lb_logits, 'hg_out_gain': hg_out_gain, 'hg_w_o': bf(hg_w_o),
        'rw_mu': rw_mu, 'rw_w_rkv': bf(rw_w_rkv), 'rw_w0': rw_w0, 'rw_w1': bf(rw_w1), 'rw_w2': bf(rw_w2),
        'rw_a0': rw_a0, 'rw_a1': bf(rw_a1), 'rw_a2': bf(rw_a2), 'rw_g1': bf(rw_g1), 'rw_g2': bf(rw_g2),
        'rw_k_k': rw_k_k, 'rw_k_a': rw_k_a, 'rw_r_k': rw_r_k, 'rw_ln_g': rw_ln_g, 'rw_ln_b': rw_ln_b,
        'rw_w_o': bf(rw_w_o),
        'xa_w_q': bf(xa_w_q), 'xa_q_gain': xa_q_gain, 'xa_w_o': bf(xa_w_o),
        'ffn_w_in': bf(ffn_w_in), 'ffn_conv_w': ffn_conv_w, 'ffn_conv_b': ffn_conv_b, 'ffn_w_down': bf(ffn_w_down),
    }

    mem_kv_prompt, mem_kv_flat = _memory_kv(mem_prompt.reshape(bp * N_MEM, d), mem_norm[:, None, :], bf(xa_w_kv),
                                            xa_k_gain[:, None, :], bp)

    as_rows = lambda cc: cc.reshape(cc.shape[0], cc.shape[1], cc.shape[2] * cc.shape[3] * cc.shape[4], cc.shape[5])
    as_cache = lambda rr: rr.reshape(rr.shape[0], rr.shape[1], rr.shape[2] // KV_ROWS, 2, A_HEADS, A_HEAD_DIM)
    y_prompt, a_p, hg_p, rw_p, sh_p, ffn_p = _trunk(
        x_prompt, mem_kv_flat, None,
        jnp.zeros((n_b, bp, HG_HEADS, HG_DK, HG_DK), F32),
        jnp.zeros((n_c, bp, RW_HEADS, RW_HEAD_DIM, RW_HEAD_DIM), F32),
        jnp.zeros((n_c, bp, d), F32),
        jnp.zeros((DEPTH, bp, CONV_W - 1, D_FF), F32), w, True)
    y_sample, a_s, hg_s, rw_s, sh_s, ffn_s = _trunk(
        x_sample, cache_mem_kv,
        tuple(as_rows(cc) for cc in (cache_attn_kv_w128, cache_attn_kv_w512, cache_attn_kv_w2048)),
        state_hgrn, state_rwkv, state_rwkv_shift, state_ffn_conv, w, False)
    a_p = tuple(as_cache(rr) for rr in a_p)
    a_s = tuple(as_cache(rr) for rr in a_s)
    return (y_prompt, y_sample, a_p[0], a_p[1], a_p[2], hg_p, rw_p, sh_p, ffn_p, mem_kv_prompt,
            a_s[0], a_s[1], a_s[2], hg_s, rw_s, sh_s, ffn_s)
```

```python
import functools
import math

import numpy as np
import jax
import jax.numpy as jnp
from jax import lax
from jax.experimental import pallas as pl
from jax.experimental.pallas import tpu as pltpu

F32 = jnp.float32
BF16 = jnp.bfloat16

D_MODEL = 1024
DEPTH = 4
N_MIXERS = 3
DIL_WINDOWS = (128, 512, 2048)
DIL_RATES = (1, 4, 16)
N_GROUPS = 3
A_SPAN = 128
A_HEADS = 4
A_HEAD_DIM = 128
A_BLOCK = 128
A_UNROLL = 4
A_WIDTH = A_HEADS * A_HEAD_DIM
KV_ROWS = 2 * A_HEADS
HG_HEADS = 8
HG_DK = 128
HG_CHUNK = 128
HG_SHORT_CHUNK = 16
RW_HEADS = 16
RW_HEAD_DIM = 64
RW_CHUNK = 64
RW_GN_EPS = 64e-5
N_MEM = 256
XA_HEADS = 4
XA_HEAD_DIM = D_MODEL // XA_HEADS
D_FF = 2816
CONV_W = 3
RMS_EPS = 1e-6
LANES = 128
VMEM_LIMIT = 52 * 1024 * 1024

NT_DIMS = (((1,), (1,)), ((), ()))
TN_DIMS = (((0,), (0,)), ((), ()))


def _params(n_axes):
    return pltpu.CompilerParams(dimension_semantics=("arbitrary",) * n_axes, vmem_limit_bytes=VMEM_LIMIT)


def _rms(x):
    return x * lax.rsqrt(jnp.mean(x * x, axis=-1, keepdims=True) + RMS_EPS)


def _dot(a, b):
    return jnp.dot(a.astype(BF16), b.astype(BF16), preferred_element_type=F32)


def _dot_nt(a, b):
    return lax.dot_general(a.astype(BF16), b.astype(BF16), NT_DIMS, preferred_element_type=F32)


def _dot_tn(a, b):
    return lax.dot_general(a.astype(BF16), b.astype(BF16), TN_DIMS, preferred_element_type=F32)


def _split3(x):
    p1 = x.astype(BF16)
    r1 = x - p1.astype(F32)
    p2 = r1.astype(BF16)
    p3 = (r1 - p2.astype(F32)).astype(BF16)
    return jnp.concatenate([p1, p2, p3], axis=1)


def _exact_rowmix(mat_bf16, x):
    w = x.shape[1]
    y = jnp.dot(mat_bf16, _split3(x), preferred_element_type=F32)
    return y[:, :w] + y[:, w:2 * w] + y[:, 2 * w:]


def _softplus(x):
    return jnp.maximum(x, 0.0) + jnp.log1p(jnp.exp(-jnp.abs(x)))


def _sigmoid(x):
    return 1.0 / (1.0 + jnp.exp(-x))


def _silu(x):
    return x * _sigmoid(x)


def _nmm_body(x_ref, g_ref, w_ref, hg_ref, o_ref, *, hn_width, hn_tiles, tn):
    h = (_rms(x_ref[...]) * g_ref[...]).astype(BF16)
    for c in range(w_ref.shape[1] // tn):
        y = jnp.dot(h, w_ref[:, c * tn:(c + 1) * tn], preferred_element_type=F32)
        if hn_width is not None and hn_tiles[c]:
            for cc in range(tn // hn_width):
                sl = slice(c * tn + cc * hn_width, c * tn + (cc + 1) * hn_width)
                o_ref[:, sl] = _rms(y[:, cc * hn_width:(cc + 1) * hn_width]) * hg_ref[:, sl]
        else:
            o_ref[:, c * tn:(c + 1) * tn] = y


def _norm_matmul(x, gain, w, head_gain=None, hn_width=None, hn_tiles=None, tm=256, tn=512):
    m, d = x.shape
    n = w.shape[1]
    tm = min(tm, m)
    if head_gain is None:
        head_gain = jnp.ones((1, n), F32)
    return pl.pallas_call(
        functools.partial(_nmm_body, hn_width=hn_width, hn_tiles=hn_tiles, tn=tn),
        grid=(m // tm,),
        in_specs=[
            pl.BlockSpec((tm, d), lambda i: (i, 0)),
            pl.BlockSpec((1, d), lambda i: (0, 0)),
            pl.BlockSpec((d, n), lambda i: (0, 0)),
            pl.BlockSpec((1, n), lambda i: (0, 0)),
        ],
        out_specs=pl.BlockSpec((tm, n), lambda i: (i, 0)),
        out_shape=jax.ShapeDtypeStruct((m, n), F32),
        compiler_params=_params(1),
        name="norm_matmul",
    )(x, gain, w, head_gain)


def _mm_res_body(x_ref, a_ref, w_ref, o_ref):
    o_ref[...] = x_ref[...] + jnp.dot(a_ref[...].astype(BF16), w_ref[...], preferred_element_type=F32)


def _matmul_residual(x, a, w, tm=512):
    m, d = x.shape
    k = a.shape[1]
    tm = min(tm, m)
    return pl.pallas_call(
        _mm_res_body,
        grid=(m // tm,),
        in_specs=[
            pl.BlockSpec((tm, d), lambda i: (i, 0)),
            pl.BlockSpec((tm, k), lambda i: (i, 0)),
            pl.BlockSpec((k, d), lambda i: (0, 0)),
        ],
        out_specs=pl.BlockSpec((tm, d), lambda i: (i, 0)),
        out_shape=jax.ShapeDtypeStruct((m, d), F32),
        compiler_params=_params(1),
        name="matmul_residual",
    )(x, a, w)


def _ds(start, size, stride):
    return pl.ds(start, size) if stride == 1 else pl.ds(start, size, stride=stride)


def _attn_prompt_body(q0, q1, q2, k0, v0, k1, v1, k2, v2, o_ref, acc, m_s, l_s, *, seq):
    scale = A_HEAD_DIM ** -0.5
    blk = A_BLOCK
    for g, (q_ref, k_ref, v_ref) in enumerate(((q0, k0, v0), (q1, k1, v1), (q2, k2, v2))):
        dil = DIL_RATES[g]
        n = seq // dil
        nb = n // blk
        width = 2 * blk if nb > 1 else blk
        ii = lax.broadcasted_iota(jnp.int32, (blk, width), 0)
        jj = lax.broadcasted_iota(jnp.int32, (blk, width), 1)
        if nb > 1:
            in_cur = (jj >= blk) & (jj - blk <= ii)
            in_prev = (jj < blk) & (jj >= ii)
        else:
            in_cur, in_prev = jj <= ii, None

        def body(it, carry, g=g, dil=dil, nb=nb, q_ref=q_ref, k_ref=k_ref, v_ref=v_ref, in_cur=in_cur,
                 in_prev=in_prev):
            rows, ss, vws = [], [], []
            for uu in range(A_UNROLL):
                idx = it * A_UNROLL + uu
                r = idx // nb
                mb = idx % nb
                rw = _ds(r + dil * blk * mb, blk, dil)
                kc = k_ref[rw, :].astype(BF16)
                vc = v_ref[rw, :].astype(BF16)
                if nb > 1:
                    prev = _ds(r + dil * blk * jnp.maximum(mb - 1, 0), blk, dil)
                    kw = jnp.concatenate([k_ref[prev, :].astype(BF16), kc], axis=0)
                    vw = jnp.concatenate([v_ref[prev, :].astype(BF16), vc], axis=0)
                    prev_bias = jnp.where(mb > 0, 0.0, -jnp.inf)
                    bias = jnp.where(in_cur, 0.0, jnp.where(in_prev, prev_bias, -jnp.inf))
                else:
                    kw, vw = kc, vc
                    bias = jnp.where(in_cur, 0.0, -jnp.inf)
                rows.append(rw)
                vws.append(vw)
                ss.append(_dot_nt(q_ref[rw, :], kw) * scale + bias)
            ms = [jnp.max(s, axis=-1, keepdims=True) for s in ss]
            ps = [jnp.exp(s - m) for s, m in zip(ss, ms)]
            ls = [jnp.sum(p, axis=-1, keepdims=True) for p in ps]
            os_ = [_dot(p, vw) for p, vw in zip(ps, vws)]
            m_bs = [jnp.broadcast_to(m, (blk, LANES)) for m in ms]
            l_bs = [jnp.broadcast_to(l, (blk, LANES)) for l in ls]
            if g == 0:
                for rw, o, m_b, l_b in zip(rows, os_, m_bs, l_bs):
                    acc[rw, :] = o
                    m_s[rw, :] = m_b
                    l_s[rw, :] = l_b
            else:
                olds = [(m_s[rw, :], l_s[rw, :], acc[rw, :]) for rw in rows]
                for rw, o, m_b, l_b, (m_old, l_old, acc_old) in zip(rows, os_, m_bs, l_bs, olds):
                    m_new = jnp.maximum(m_old, m_b)
                    a_old = jnp.exp(m_old - m_new)
                    a_cur = jnp.exp(m_b - m_new)
                    acc[rw, :] = acc_old * a_old + o * a_cur
                    l_s[rw, :] = l_old * a_old + l_b * a_cur
                    m_s[rw, :] = m_new
            return carry

        assert (dil * nb) % A_UNROLL == 0
        lax.fori_loop(0, dil * nb // A_UNROLL, body, 0)
    o_ref[...] = acc[...] / l_s[...]


def _attn_prompt(qkv, batch, seq):
    m = batch * seq
    assert seq % (A_BLOCK * max(DIL_RATES)) == 0 and A_BLOCK == A_SPAN
    nq = N_GROUPS * A_HEADS

    def spec(col_fn):
        return pl.BlockSpec((seq, A_HEAD_DIM), lambda b, h: (b, col_fn(h)))

    in_specs = [spec(lambda h, g=g: g * A_HEADS + h) for g in range(N_GROUPS)]
    for g in range(N_GROUPS):
        in_specs.append(spec(lambda h, g=g: nq + g * 2 * A_HEADS + h))
        in_specs.append(spec(lambda h, g=g: nq + g * 2 * A_HEADS + A_HEADS + h))
    return pl.pallas_call(
        functools.partial(_attn_prompt_body, seq=seq),
        grid=(batch, A_HEADS),
        in_specs=in_specs,
        out_specs=pl.BlockSpec((seq, A_HEAD_DIM), lambda b, h: (b, h)),
        out_shape=jax.ShapeDtypeStruct((m, A_WIDTH), F32),
        scratch_shapes=[pltpu.VMEM((seq, A_HEAD_DIM), F32), pltpu.VMEM((seq, LANES), F32),
                        pltpu.VMEM((seq, LANES), F32)],
        compiler_params=_params(2),
        name="dilated_attn_prompt",
    )(*([qkv] * 9))


def _attn_sample_body(*refs, t_new, n_alias):
    qkv_ref, bufs = refs[0], refs[1:4]
    o_ref, outs = refs[4 + n_alias], refs[5 + n_alias:8 + n_alias]
    scale = A_HEAD_DIM ** -0.5
    pad = jnp.zeros((LANES - t_new, A_HEAD_DIM), F32)
    nq = N_GROUPS * A_WIDTH

    def new_rows(g, kv, h):
        base = nq + (2 * g + kv) * A_WIDTH + h * A_HEAD_DIM
        return qkv_ref[:, base:base + A_HEAD_DIM]

    for g in range(N_GROUPS):
        length = bufs[g].shape[0] // KV_ROWS
        keep = (length - t_new) * KV_ROWS
        outs[g][0:keep, :] = bufs[g][t_new * KV_ROWS:length * KV_ROWS, :]
        for kv in range(2):
            for h in range(A_HEADS):
                outs[g][pl.ds(keep + kv * A_HEADS + h, t_new, stride=KV_ROWS), :] = new_rows(g, kv, h)

    for h in range(A_HEADS):
        parts, vals = [], []
        for g in range(N_GROUPS):
            dil = DIL_RATES[g]
            length = bufs[g].shape[0] // KV_ROWS
            q = qkv_ref[:, g * A_WIDTH + h * A_HEAD_DIM: g * A_WIDTH + (h + 1) * A_HEAD_DIM]
            s_buf = _dot_nt(q, bufs[g][pl.ds(h, length, stride=KV_ROWS), :]) * scale
            s_new = _dot_nt(q, jnp.concatenate([new_rows(g, 0, h), pad], axis=0)) * scale
            for s, base in ((s_buf, 0), (s_new, length)):
                ii = lax.broadcasted_iota(jnp.int32, s.shape, 0)
                jj = lax.broadcasted_iota(jnp.int32, s.shape, 1) + base
                dd = jj - ii
                valid = (dd >= 0) & (dd <= A_SPAN * dil) & ((dd & (dil - 1)) == 0) & (jj < length + t_new)
                parts.append(jnp.where(valid, s, -jnp.inf))
            vals.append(bufs[g][pl.ds(A_HEADS + h, length, stride=KV_ROWS), :])
            vals.append(jnp.concatenate([new_rows(g, 1, h), pad], axis=0))
        mx = functools.reduce(jnp.maximum, [jnp.max(s, axis=-1, keepdims=True) for s in parts])
        exps = [jnp.exp(s - mx) for s in parts]
        den = functools.reduce(lambda a, b: a + b, [jnp.sum(e, axis=-1, keepdims=True) for e in exps])
        o = functools.reduce(lambda a, b: a + b, [_dot(e / den, vv) for e, vv in zip(exps, vals)])
        o_ref[:, h * A_HEAD_DIM:(h + 1) * A_HEAD_DIM] = o


def _attn_sample(qkv, caches, prev_out, j, batch, t_new):
    assert t_new % 8 == 0 and t_new <= LANES
    lens = [cch.shape[2] // KV_ROWS for cch in caches]
    for g in range(N_GROUPS):
        assert lens[g] == A_SPAN * DIL_RATES[g] and (DIL_RATES[g] & (DIL_RATES[g] - 1)) == 0

    def cache_spec(g):
        return pl.BlockSpec((None, None, lens[g] * KV_ROWS, A_HEAD_DIM), lambda b: (j, b, 0, 0))

    in_specs = [pl.BlockSpec((t_new, qkv.shape[1]), lambda b: (b, 0))] + [cache_spec(g) for g in range(N_GROUPS)]
    n_alias = 0 if prev_out is None else N_GROUPS
    aliases = {}
    if prev_out is not None:
        in_specs += [pl.BlockSpec(memory_space=pl.ANY)] * N_GROUPS
        aliases = {4 + g: 1 + g for g in range(N_GROUPS)}
    out_specs = [pl.BlockSpec((t_new, A_WIDTH), lambda b: (b, 0))] + [cache_spec(g) for g in range(N_GROUPS)]
    out_shape = [jax.ShapeDtypeStruct((batch * t_new, A_WIDTH), F32)]
    out_shape += [jax.ShapeDtypeStruct(cch.shape, F32) for cch in caches]
    return pl.pallas_call(
        functools.partial(_attn_sample_body, t_new=t_new, n_alias=n_alias),
        grid=(batch,),
        in_specs=in_specs,
        out_specs=out_specs,
        out_shape=out_shape,
        input_output_aliases=aliases,
        compiler_params=_params(1),
        name="dilated_attn_sample",
    )(qkv, *caches, *(prev_out or ()))


def _kv_cache_body(*refs):
    k_ref, v_ref, o_ref = refs[0], refs[1], refs[-1]
    tr = k_ref.shape[0]
    for kv, x_ref in enumerate((k_ref, v_ref)):
        for h in range(A_HEADS):
            o_ref[pl.ds(kv * A_HEADS + h, tr, stride=KV_ROWS), :] = x_ref[:, h * A_HEAD_DIM:(h + 1) * A_HEAD_DIM]


def _kv_cache_out(qkv, prev_out, g, j, n_layers, batch, seq):
    win = min(DIL_WINDOWS[g], seq)
    tr = min(win, 512)
    first = (seq - win) // tr
    row = lambda b, t: b * (seq // tr) + first + t
    in_specs = [pl.BlockSpec((tr, A_WIDTH), lambda b, t: (row(b, t), N_GROUPS + 2 * g)),
                pl.BlockSpec((tr, A_WIDTH), lambda b, t: (row(b, t), N_GROUPS + 2 * g + 1))]
    aliases = {}
    if prev_out is not None:
        in_specs.append(pl.BlockSpec(memory_space=pl.ANY))
        aliases = {2: 0}
    return pl.pallas_call(
        _kv_cache_body,
        grid=(batch, win // tr),
        in_specs=in_specs,
        out_specs=pl.BlockSpec((None, None, tr * KV_ROWS, A_HEAD_DIM), lambda b, t: (j, b, t, 0)),
        out_shape=jax.ShapeDtypeStruct((n_layers, batch, win * KV_ROWS, A_HEAD_DIM), F32),
        input_output_aliases=aliases,
        compiler_params=_params(2),
        name="kv_cache_out",
    )(qkv, qkv, *(() if prev_out is None else (prev_out,)))


def _hgrn_consts(c):
    t = np.arange(c)[:, None]
    u = np.arange(c)[None, :]
    mats = [(u <= t), (u > t)]
    masks = [(u == t)]
    s = 1
    while s < c:
        upper = (t // s) % 2 == 1
        mid = (t // (2 * s)) * 2 * s + s - 1
        mats.append(np.where(upper, (u > mid) & (u <= t), (u > t) & (u <= mid)))
        lower_u = (u // s) % 2 == 0
        masks.append(upper & lower_u & (t // (2 * s) == u // (2 * s)))
        s *= 2
    mats = np.concatenate([m.astype(np.float32) for m in mats], axis=0)
    masks = np.stack([m.astype(np.float32) for m in masks], axis=0)
    return jnp.asarray(mats, BF16), jnp.asarray(masks, F32)


def _hgrn_body(q_ref, z_ref, i_ref, g_ref, lbl_ref, og_ref, s0_ref, mat_ref, msk_ref, o_ref, st_ref, st_scr,
               *, layer, t_len, c, hp):
    n_lvl = msk_ref.shape[0] - 1
    width = hp * HG_DK
    hr = range(hp)
    lane = lambda x, hd: x[:, hd * HG_DK:(hd + 1) * HG_DK]
    lg = lbl_ref[...]
    e = jnp.exp(lg - jnp.max(lg, axis=0, keepdims=True))
    prob = e / jnp.sum(e, axis=0, keepdims=True)
    lb = jnp.sum(prob[0:layer + 1], axis=0, keepdims=True) - prob[0:1]
    log_lb = jnp.log(lb)
    log_1m = jnp.log1p(-lb)
    @pl.when(pl.program_id(2) == 0)
    def _():
        for hd in hr:
            st_scr[hd] = s0_ref[hd].T

    row = lax.broadcasted_iota(jnp.int32, (c, width), 0)
    out_gain = jnp.concatenate([og_ref[...]] * hp, axis=1)

    def chunk(ci, carry):
        if t_len >= c:
            rows = pl.ds(pl.multiple_of(ci * c, c), c)
            ld = lambda ref: ref[rows, :]
            live = None
        else:
            padz = jnp.zeros((c - t_len, width), F32)
            ld = lambda ref: jnp.concatenate([ref[...], padz], axis=0)
            live = row < t_len
        z = ld(z_ref)
        lsig = jnp.minimum(z, 0.0) - jnp.log1p(jnp.exp(-jnp.abs(z)))
        a1 = jnp.broadcast_to(log_lb, z.shape)
        a2 = log_1m + lsig
        hi = jnp.maximum(a1, a2)
        lo = jnp.minimum(a1, a2)
        log_f = hi + jnp.log1p(jnp.exp(lo - hi))
        kg = (1.0 - lb) * _sigmoid(-z)
        if live is not None:
            log_f = jnp.where(live, log_f, 0.0)
            kg = jnp.where(live, kg, 0.0)
        qa = _silu(ld(q_ref))
        vv = ld(i_ref)
        ex = _exact_rowmix(mat_ref[...], log_f)
        b = ex[0:c]
        sts = [st_scr[hd] for hd in hr]
        a = [_dot_nt(lane(qa, hd), lane(kg, hd)) * msk_ref[0] for hd in hr]
        for lv in range(n_lvl):
            wv = jnp.exp(ex[(2 + lv) * c:(3 + lv) * c])
            qw = qa * wv
            kw = kg * wv
            a = [a[hd] + _dot_nt(lane(qw, hd), lane(kw, hd)) * msk_ref[1 + lv] for hd in hr]
        qb = qa * jnp.exp(b)
        kd = kg * jnp.exp(ex[c:2 * c])
        dec = jnp.exp(b[c - 1:c])
        o = [_dot_nt(lane(qb, hd), sts[hd]) + _dot(a[hd], lane(vv, hd)) for hd in hr]
        st_scr[...] = jnp.stack([sts[hd] * lane(dec, hd) + _dot_tn(lane(vv, hd), lane(kd, hd)) for hd in hr], axis=0)
        y = jnp.concatenate([_rms(x) for x in o], axis=1) * out_gain * _silu(ld(g_ref))
        if t_len >= c:
            o_ref[rows, :] = y
        else:
            o_ref[...] = y[0:t_len]
        return carry

    lax.fori_loop(0, max(t_len // c, 1), chunk, 0)
    for hd in hr:
        st_ref[hd] = st_scr[hd].T


def _hgrn(proj, lb_logits, out_gain, s0, layer, batch, t_len, tt=512):
    tt = min(tt, t_len)
    if tt >= HG_CHUNK:
        c, hp = HG_CHUNK, 4
        assert tt % c == 0 and t_len % tt == 0
    else:
        c, hp = HG_SHORT_CHUNK, HG_HEADS
        assert tt % 8 == 0 and tt <= c
    mats, masks = _hgrn_consts(c)
    width = hp * HG_DK
    ng = HG_HEADS // hp
    nt = t_len // tt

    def col(part):
        return pl.BlockSpec((tt, width), lambda b, h, t: (b * nt + t, part * ng + h))

    st_spec = pl.BlockSpec((None, hp, HG_DK, HG_DK), lambda b, h, t: (b, h, 0, 0))
    return pl.pallas_call(
        functools.partial(_hgrn_body, layer=layer, t_len=tt, c=c, hp=hp),
        grid=(batch, ng, nt),
        in_specs=[col(0), col(1), col(2), col(3),
                  pl.BlockSpec((DEPTH, width), lambda b, h, t: (0, h)),
                  pl.BlockSpec((1, HG_DK), lambda b, h, t: (0, 0)),
                  st_spec,
                  pl.BlockSpec(mats.shape, lambda b, h, t: (0, 0)),
                  pl.BlockSpec(masks.shape, lambda b, h, t: (0, 0, 0))],
        out_specs=[pl.BlockSpec((tt, width), lambda b, h, t: (b * nt + t, h)), st_spec],
        out_shape=[jax.ShapeDtypeStruct((batch * t_len, HG_HEADS * HG_DK), F32),
                   jax.ShapeDtypeStruct((batch, HG_HEADS, HG_DK, HG_DK), F32)],
        scratch_shapes=[pltpu.VMEM((hp, HG_DK, HG_DK), F32)],
        compiler_params=_params(3),
        name="hgrn2_chunked",
    )(proj, proj, proj, proj, lb_logits, out_gain, s0, mats, masks)


def _rwkv_proj_body(x_ref, g_ref, sh_ref, mu_ref, wrkv_ref, w0_ref, w1_ref, w2_ref, a0_ref, a1_ref, a2_ref,
                    g1_ref, g2_ref, kk_ref, ka_ref,
                    r_o, k_o, v_o, ld_o, kk_o, a_o, g_o, sh_o, carry, *, bb, tt):
    t = pl.program_id(1)
    d = D_MODEL

    @pl.when(t == 0)
    def _():
        carry[...] = sh_ref[...]

    h3 = _rms(x_ref[...]) * g_ref[...]
    h = h3.reshape(bb * tt, d)
    rolled = pltpu.roll(h, 1, 0).reshape(bb, tt, d)
    tpos = lax.broadcasted_iota(jnp.int32, (bb, tt, d), 1)
    prev = jnp.where(tpos == 0, carry[...], rolled).reshape(bb * tt, d)
    last = h3[:, tt - 1:tt, :]
    carry[...] = last
    sh_o[...] = last
    dx = prev - h
    mix = lambda jm: h + dx * mu_ref[jm:jm + 1, :]
    r = _dot(mix(0), wrkv_ref[0])
    k = _dot(mix(1), wrkv_ref[1])
    v = _dot(mix(2), wrkv_ref[2])
    wl = w0_ref[...] + _dot(jnp.tanh(_dot(mix(3), w1_ref[...])), w2_ref[...])
    wlog = -_softplus(-wl) - 0.5
    a = _sigmoid(a0_ref[...] + _dot(_dot(mix(4), a1_ref[...]), a2_ref[...]))
    gate = _dot(_sigmoid(_dot(mix(5), g1_ref[...])), g2_ref[...])
    r_o[...] = r
    k_o[...] = k * (1.0 + (a - 1.0) * ka_ref[...])
    v_o[...] = v
    ld_o[...] = -jnp.exp(wlog)
    kk_o[...] = k * kk_ref[...]
    a_o[...] = a
    g_o[...] = gate


def _rwkv_proj(x3, gain, shift, mu, wrkv, w0, w1, w2, a0, a1, a2, g1, g2, k_k, k_a, bb, tt):
    batch, t_len, d = x3.shape
    m = batch * t_len
    rows = bb * tt
    full = lambda arr: pl.BlockSpec(arr.shape, lambda b, t: (0,) * arr.ndim)
    row_spec = pl.BlockSpec((rows, d), lambda b, t: (b * (t_len // tt) + t, 0))
    outs = pl.pallas_call(
        functools.partial(_rwkv_proj_body, bb=bb, tt=tt),
        grid=(batch // bb, t_len // tt),
        in_specs=[pl.BlockSpec((bb, tt, d), lambda b, t: (b, t, 0)), full(gain),
                  pl.BlockSpec((bb, 1, d), lambda b, t: (b, 0, 0)), full(mu), full(wrkv),
                  full(w0), full(w1), full(w2), full(a0), full(a1), full(a2), full(g1), full(g2),
                  full(k_k), full(k_a)],
        out_specs=[row_spec] * 7 + [pl.BlockSpec((bb, 1, d), lambda b, t: (b, 0, 0))],
        out_shape=[jax.ShapeDtypeStruct((m, d), F32)] * 7 + [jax.ShapeDtypeStruct((batch, 1, d), F32)],
        scratch_shapes=[pltpu.VMEM((bb, 1, d), F32)],
        compiler_params=_params(2),
        name="rwkv_proj",
    )(x3, gain, shift, mu, wrkv, w0, w1, w2, a0, a1, a2, g1, g2, k_k, k_a)
    return outs


def _rwkv_consts(c):
    n = RW_HEAD_DIM
    t = np.arange(c)[:, None]
    u = np.arange(c)[None, :]
    f = lambda mat, dt: jnp.asarray(np.asarray(mat).astype(np.float32), dt)
    tri = f(u <= t, BF16)
    lo = f(np.concatenate([u < t, u <= t], axis=0), F32)
    eye = f(u == t, F32)
    ch = np.arange(D_MODEL)[:, None] // n
    col = np.arange(LANES)[None, :]
    seg = f(ch == col, BF16)
    seg_t = f((ch == col).T, BF16)
    lane = np.arange(LANES)
    bdiag = f((lane[:, None] // n) == (lane[None, :] // n), F32)
    return tri, lo, eye, seg, seg_t, bdiag


def _split3_rows(x):
    p1 = x.astype(BF16)
    r1 = x - p1.astype(F32)
    p2 = r1.astype(BF16)
    p3 = (r1 - p2.astype(F32)).astype(BF16)
    return jnp.concatenate([p1, p2, p3], axis=0)


def _exact_colmix(x, mat_bf16):
    rws = x.shape[0]
    y = jnp.dot(_split3_rows(x), mat_bf16, preferred_element_type=F32)
    return y[0:rws] + y[rws:2 * rws] + y[2 * rws:]


def _rwkv_rec_body(r_ref, k_ref, v_ref, ld_ref, kk_ref, a_ref, g_ref, rk_ref, lng_ref, lnb_ref, s0_ref,
                   tri_ref, lo_ref, eye_ref, seg_ref, segt_ref, bd_ref, o_ref, st_ref, st_scr, *, tt):
    c = RW_CHUNK
    n = RW_HEAD_DIM
    heads = RW_HEADS
    pairs = heads // 2
    w = D_MODEL
    t = pl.program_id(1)

    @pl.when(t == 0)
    def _():
        zero = jnp.zeros((n, n), F32)
        for p in range(pairs):
            top = jnp.concatenate([s0_ref[2 * p], zero], axis=1)
            bot = jnp.concatenate([zero, s0_ref[2 * p + 1]], axis=1)
            st_scr[p] = jnp.concatenate([top, bot], axis=0)

    low_2c = lax.broadcasted_iota(jnp.int32, (2 * c, LANES), 1) < n
    low_c = lax.broadcasted_iota(jnp.int32, (c, LANES), 1) < n
    head_sums = lambda x: _exact_colmix(_exact_colmix(x, seg_ref[...]), segt_ref[...])
    tile = lambda x, p: x[:, p * LANES:(p + 1) * LANES]
    hr = range(heads)
    pr = range(pairs)

    def chunk(ci, carry):
        if tt >= c:
            rows = pl.ds(pl.multiple_of(ci * c, c), c)
            ld = lambda ref: ref[rows, :]
        else:
            padz = jnp.zeros((c - tt, w), F32)
            ld = lambda ref: jnp.concatenate([ref[...], padz], axis=0)
        r, kf, v, lc, kk, a, gate = (ld(ref) for ref in (r_ref, k_ref, v_ref, ld_ref, kk_ref, a_ref, g_ref))
        kk = kk / jnp.maximum(jnp.sqrt(head_sums(kk * kk)), 1e-12)
        bvec = kk * a
        cum = _exact_rowmix(tri_ref[...], lc)
        c_end = cum[c - 1:c]
        e_neg = jnp.exp(-cum)
        e_end = jnp.exp(c_end - cum)
        d_end = jnp.exp(c_end)
        kr = jnp.concatenate([kk * jnp.exp(cum - lc), r * jnp.exp(cum)], axis=0)
        bet = bvec * e_neg
        chi = kf * e_neg
        bk = jnp.concatenate([bvec * e_end, kf * e_end], axis=0)
        lo = lo_ref[...]
        eye = eye_ref[...]
        krm = [jnp.where(low_2c if h % 2 == 0 else jnp.logical_not(low_2c), tile(kr, h // 2), 0.0) for h in hr]
        ab = [_dot_nt(krm[h], tile(bet, h // 2)) * lo for h in hr]
        ak = [_dot_nt(krm[h], tile(chi, h // 2)) * lo for h in hr]
        nn = [x[0:c] for x in ab]
        tm = [eye - x for x in nn]
        npow = [_dot(x, x) for x in nn]
        p2 = 2
        while p2 < c:
            tm = [x + _dot(x, y) for x, y in zip(tm, npow)]
            p2 *= 2
            if p2 < c:
                npow = [_dot(x, x) for x in npow]
        sts = [st_scr[p] for p in pr]
        ks = [_dot_nt(tile(kr, p), sts[p]) for p in pr]
        av = [jnp.where(low_2c, _dot(ak[2 * p], tile(v, p)), _dot(ak[2 * p + 1], tile(v, p))) for p in pr]
        rhs = [ks[p][0:c] + av[p][0:c] for p in pr]
        u = [jnp.where(low_c, -_dot(tm[2 * p], rhs[p]), -_dot(tm[2 * p + 1], rhs[p])) for p in pr]
        y = [ks[p][c:2 * c] + av[p][c:2 * c]
             + jnp.where(low_c, _dot(ab[2 * p][c:2 * c], u[p]), _dot(ab[2 * p + 1][c:2 * c], u[p])) for p in pr]
        new_st = [sts[p] * tile(d_end, p)
                  + _dot_tn(jnp.concatenate([u[p], tile(v, p)], axis=0), tile(bk, p)) * bd_ref[...] for p in pr]
        st_scr[...] = jnp.stack(new_st, axis=0)
        yy = jnp.concatenate(y, axis=1)
        yc = yy - head_sums(yy) * (1.0 / n)
        var = head_sums(yc * yc) * (1.0 / n)
        yn = yc * lax.rsqrt(var + RW_GN_EPS) * lng_ref[...] + lnb_ref[...]
        z = (yn + head_sums(r * kf * rk_ref[...]) * v) * gate
        if tt >= c:
            o_ref[rows, :] = z
        else:
            o_ref[...] = z[0:tt]
        return carry

    lax.fori_loop(0, max(tt // c, 1), chunk, 0)
    for p in range(pairs):
        blk = st_scr[p]
        st_ref[2 * p] = blk[0:n, 0:n]
        st_ref[2 * p + 1] = blk[n:2 * n, n:2 * n]


def _rwkv_rec(proj, r_k, ln_g, ln_b, s0, batch, t_len, tt=256):
    r, kf, v, ld, kk, a, gate = proj
    tt = min(tt, t_len)
    assert tt % RW_CHUNK == 0 or (tt < RW_CHUNK and tt % 8 == 0)
    assert RW_HEADS * RW_HEAD_DIM == D_MODEL and 2 * RW_HEAD_DIM == LANES and 2 * RW_CHUNK == LANES
    consts = _rwkv_consts(RW_CHUNK)
    nt = t_len // tt
    row_spec = pl.BlockSpec((tt, D_MODEL), lambda b, t: (b * nt + t, 0))
    par_spec = pl.BlockSpec((1, D_MODEL), lambda b, t: (0, 0))
    st_spec = pl.BlockSpec((None, RW_HEADS, RW_HEAD_DIM, RW_HEAD_DIM), lambda b, t: (b, 0, 0, 0))
    const = lambda arr: pl.BlockSpec(arr.shape, lambda b, t: (0, 0))
    return pl.pallas_call(
        functools.partial(_rwkv_rec_body, tt=tt),
        grid=(batch, nt),
        in_specs=[row_spec] * 7 + [par_spec] * 3 + [st_spec] + [const(x) for x in consts],
        out_specs=[row_spec, st_spec],
        out_shape=[jax.ShapeDtypeStruct((batch * t_len, D_MODEL), F32),
                   jax.ShapeDtypeStruct((batch, RW_HEADS, RW_HEAD_DIM, RW_HEAD_DIM), F32)],
        scratch_shapes=[pltpu.VMEM((RW_HEADS // 2, LANES, LANES), F32)],
        compiler_params=_params(2),
        name="rwkv7_chunked",
    )(r, kf, v, ld, kk, a, gate, r_k, ln_g, ln_b, s0, *consts)


def _xattn_body(x_ref, g_ref, wq_ref, qg_ref, kv_ref, wo_ref, o_ref, *, bb, tt, native):
    d = D_MODEL
    e = XA_HEAD_DIM
    scale = XA_HEAD_DIM ** -0.5
    x = x_ref[...].reshape(bb * tt, d)
    q = jnp.dot((_rms(x) * g_ref[...]).astype(BF16), wq_ref[...], preferred_element_type=F32)
    qn = lambda b, h: _rms(q[b * tt:(b + 1) * tt, h * e:(h + 1) * e]) * qg_ref[...]
    per_batch = []
    for b in range(bb):
        if native:
            rows = N_MEM * XA_HEADS
            s = _dot_nt(jnp.concatenate([qn(b, h) for h in range(XA_HEADS)], axis=0),
                        kv_ref[b, :, 0].reshape(rows, e)) * scale
            own = (lax.broadcasted_iota(jnp.int32, s.shape, 0) // tt
                   == lax.broadcasted_iota(jnp.int32, s.shape, 1) % XA_HEADS)
            s = jnp.where(own, s, -jnp.inf)
            p = jnp.exp(s - jnp.max(s, axis=-1, keepdims=True))
            p = p / jnp.sum(p, axis=-1, keepdims=True)
            oh = _dot(p, kv_ref[b, :, 1].reshape(rows, e))
            per_batch.append(jnp.concatenate([oh[h * tt:(h + 1) * tt] for h in range(XA_HEADS)], axis=-1))
            continue
        heads = []
        for h in range(XA_HEADS):
            s = _dot_nt(qn(b, h), kv_ref[b, :, h * e:(h + 1) * e]) * scale
            p = jnp.exp(s - jnp.max(s, axis=-1, keepdims=True))
            p = p / jnp.sum(p, axis=-1, keepdims=True)
            heads.append(_dot(p, kv_ref[b, :, d + h * e:d + (h + 1) * e]))
        per_batch.append(jnp.concatenate(heads, axis=-1))
    o = jnp.concatenate(per_batch, axis=0) if bb > 1 else per_batch[0]
    y = x + jnp.dot(o.astype(BF16), wo_ref[...], preferred_element_type=F32)
    o_ref[...] = y.reshape(bb, tt, d)


def _xattn(x3, gain, wq, q_gain, kv_all, layer, wo, bb, tt):
    batch, t_len, d = x3.shape
    native = kv_all.ndim == 6
    if native:
        kv_spec = pl.BlockSpec((None, bb, N_MEM, 2, XA_HEADS, XA_HEAD_DIM), lambda b, t: (layer, b, 0, 0, 0, 0))
    else:
        kv_spec = pl.BlockSpec((None, bb, N_MEM, 2 * d), lambda b, t: (layer, b, 0, 0))
    return pl.pallas_call(
        functools.partial(_xattn_body, bb=bb, tt=tt, native=native),
        grid=(batch // bb, t_len // tt),
        in_specs=[pl.BlockSpec((bb, tt, d), lambda b, t: (b, t, 0)),
                  pl.BlockSpec((1, d), lambda b, t: (0, 0)),
                  pl.BlockSpec((d, d), lambda b, t: (0, 0)),
                  pl.BlockSpec((1, XA_HEAD_DIM), lambda b, t: (0, 0)),
                  kv_spec,
                  pl.BlockSpec((d, d), lambda b, t: (0, 0))],
        out_specs=pl.BlockSpec((bb, tt, d), lambda b, t: (b, t, 0)),
        out_shape=jax.ShapeDtypeStruct((batch, t_len, d), F32),
        compiler_params=_params(2),
        name="memory_xattn",
    )(x3, gain, wq, q_gain, kv_all, wo)


def _memkv_body(x_ref, g_ref, w_ref, kg_ref, o_ref, flat_ref, h_scr):
    c = pl.program_id(2)
    e = XA_HEAD_DIM

    @pl.when(c == 0)
    def _():
        h_scr[...] = (_rms(x_ref[...]) * g_ref[...]).astype(BF16)

    y = jnp.dot(h_scr[...], w_ref[...], preferred_element_type=F32)

    @pl.when(c == 0)
    def _():
        for h in range(XA_HEADS):
            kn = _rms(y[:, h * e:(h + 1) * e]) * kg_ref[...]
            o_ref[:, h, :] = kn
            flat_ref[:, h * e:(h + 1) * e] = kn

    @pl.when(c == 1)
    def _():
        flat_ref[...] = y
        for h in range(XA_HEADS):
            o_ref[:, h, :] = y[:, h * e:(h + 1) * e]


def _memory_kv(mem2, mem_norm, w_kv, k_gain, batch):
    d = D_MODEL
    return pl.pallas_call(
        _memkv_body,
        grid=(DEPTH, batch, 2),
        in_specs=[pl.BlockSpec((N_MEM, d), lambda l, b, c: (b, 0)),
                  pl.BlockSpec((None, 1, d), lambda l, b, c: (l, 0, 0)),
                  pl.BlockSpec((None, d, d), lambda l, b, c: (l, 0, c)),
                  pl.BlockSpec((None, 1, XA_HEAD_DIM), lambda l, b, c: (l, 0, 0))],
        out_specs=[pl.BlockSpec((None, None, N_MEM, None, XA_HEADS, XA_HEAD_DIM), lambda l, b, c: (l, b, 0, c, 0, 0)),
                   pl.BlockSpec((None, None, N_MEM, d), lambda l, b, c: (l, b, 0, c))],
        out_shape=[jax.ShapeDtypeStruct((DEPTH, batch, N_MEM, 2, XA_HEADS, XA_HEAD_DIM), F32),
                   jax.ShapeDtypeStruct((DEPTH, batch, N_MEM, 2 * d), F32)],
        scratch_shapes=[pltpu.VMEM((N_MEM, d), BF16)],
        compiler_params=_params(3),
        name="memory_kv",
    )(mem2, mem_norm, w_kv, k_gain)


def _ffn_body(x_ref, g_ref, win_ref, cw_ref, cb_ref, wd_ref, pin_ref, o_ref, st_ref, acc, carry, *, bb, tt, tf):
    t = pl.program_id(1)
    d = D_MODEL

    @pl.when(t == 0)
    def _():
        carry[...] = pin_ref[...]

    x = x_ref[...].reshape(bb * tt, d)
    h = (_rms(x) * g_ref[...]).astype(BF16)
    acc[...] = x
    tpos = lax.broadcasted_iota(jnp.int32, (bb, tt, tf), 1)
    tails = []
    for f in range(D_FF // tf):
        fs = slice(f * tf, (f + 1) * tf)
        u = jnp.dot(h, win_ref[:, fs], preferred_element_type=F32)
        gate = jnp.dot(h, win_ref[:, D_FF + f * tf:D_FF + (f + 1) * tf], preferred_element_type=F32)
        prev = carry[:, :, fs]
        u3 = u.reshape(bb, tt, tf)
        u1 = jnp.where(tpos == 0, prev[:, 1:2, :], pltpu.roll(u, 1, 0).reshape(bb, tt, tf))
        u2 = pltpu.roll(u, 2, 0).reshape(bb, tt, tf)
        u2 = jnp.where(tpos == 0, prev[:, 0:1, :], jnp.where(tpos == 1, prev[:, 1:2, :], u2))
        conv = cb_ref[:, fs] + cw_ref[0:1, fs] * u2 + cw_ref[1:2, fs] * u1 + cw_ref[2:3, fs] * u3
        act = (_silu(conv) * gate.reshape(bb, tt, tf)).reshape(bb * tt, tf)
        acc[...] += jnp.dot(act.astype(BF16), wd_ref[fs, :], preferred_element_type=F32)
        tails.append(u3[:, tt - 2:tt, :])
    tail = jnp.concatenate(tails, axis=-1)
    carry[...] = tail
    st_ref[...] = tail
    o_ref[...] = acc[...].reshape(bb, tt, d)


def _ffn(x3, gain, w_in, conv_w, conv_b, w_down, state_all, layer, bb, tt, tf=256):
    batch, t_len, d = x3.shape
    assert t_len >= CONV_W - 1 and tt >= CONV_W - 1 and D_FF % tf == 0
    const = lambda arr: pl.BlockSpec(arr.shape, lambda b, t: (0, 0), pipeline_mode=pl.Buffered(1))
    return pl.pallas_call(
        functools.partial(_ffn_body, bb=bb, tt=tt, tf=tf),
        grid=(batch // bb, t_len // tt),
        in_specs=[pl.BlockSpec((bb, tt, d), lambda b, t: (b, t, 0)),
                  const(gain), const(w_in), const(conv_w), const(conv_b), const(w_down),
                  pl.BlockSpec((None, bb, CONV_W - 1, D_FF), lambda b, t: (layer, b, 0, 0))],
        out_specs=[pl.BlockSpec((bb, tt, d), lambda b, t: (b, t, 0)),
                   pl.BlockSpec((bb, CONV_W - 1, D_FF), lambda b, t: (b, 0, 0))],
        out_shape=[jax.ShapeDtypeStruct((batch, t_len, d), F32),
                   jax.ShapeDtypeStruct((batch, CONV_W - 1, D_FF), F32)],
        scratch_shapes=[pltpu.VMEM((bb * tt, d), F32), pltpu.VMEM((bb, CONV_W - 1, D_FF), F32)],
        compiler_params=_params(2),
        name="conv_ffn",
    )(x3, gain, w_in, conv_w, conv_b, w_down, state_all)


def _trunk(x3, mem_kv, a_bufs, hg_s, rw_s, rw_shift, ffn_buf, w, prompt):
    batch, t_len, d = x3.shape
    m = batch * t_len
    n_a = w['attn_w_qkv'].shape[0]
    new_a = None
    new_hg, new_rw, new_sh, new_ffn = [], [], [], []
    if prompt:
        xa_bb, xa_tt = 1, 512
        ff_bb, ff_tt = 1, 512
        rw_bb, rw_tt = 1, 256
    else:
        xa_bb, xa_tt = 4, t_len
        ff_bb, ff_tt = batch, t_len
        rw_bb, rw_tt = batch, t_len
    for i in range(DEPTH):
        kind, j = i % N_MIXERS, i // N_MIXERS
        x2 = x3.reshape(m, d)
        g_mix = w['norm_mix'][i][None]
        if kind == 0:
            qkv = _norm_matmul(x2, g_mix, w['attn_w_qkv'][j], w['attn_head_gain'][j], hn_width=A_HEAD_DIM,
                               hn_tiles=(True,) * N_GROUPS + (True, False) * N_GROUPS, tn=A_WIDTH)
            if prompt:
                o = _attn_prompt(qkv, batch, t_len)
                new_a = tuple(_kv_cache_out(qkv, None if new_a is None else new_a[gi], gi, j, n_a, batch, t_len)
                              for gi in range(N_GROUPS))
            else:
                o, c0, c1, c2 = _attn_sample(qkv, a_bufs, new_a, j, batch, t_len)
                new_a = (c0, c1, c2)
            x2 = _matmul_residual(x2, o, w['attn_w_o'][j])
        elif kind == 1:
            proj = _norm_matmul(x2, g_mix, w['hg_w_in'][j])
            o, st = _hgrn(proj, w['hg_lb_logits'], w['hg_out_gain'][j][None], hg_s[j], i, batch, t_len)
            new_hg.append(st)
            x2 = _matmul_residual(x2, o, w['hg_w_o'][j])
        else:
            proj = _rwkv_proj(x3, w['norm_mix'][i][None], rw_shift[j][:, None, :], w['rw_mu'][j], w['rw_w_rkv'][j],
                              w['rw_w0'][j][None], w['rw_w1'][j], w['rw_w2'][j], w['rw_a0'][j][None],
                              w['rw_a1'][j], w['rw_a2'][j], w['rw_g1'][j], w['rw_g2'][j],
                              w['rw_k_k'][j][None], w['rw_k_a'][j][None], rw_bb, rw_tt)
            o, st = _rwkv_rec(proj[:7], w['rw_r_k'][j].reshape(1, d), w['rw_ln_g'][j][None], w['rw_ln_b'][j][None],
                              rw_s[j], batch, t_len)
            new_rw.append(st)
            new_sh.append(proj[7][:, 0, :])
            x2 = _matmul_residual(x2, o, w['rw_w_o'][j])
        x3 = x2.reshape(batch, t_len, d)
        x3 = _xattn(x3, w['norm_mem'][i][None], w['xa_w_q'][i], w['xa_q_gain'][i][None], mem_kv, i,
                    w['xa_w_o'][i], xa_bb, xa_tt)
        x3, fb = _ffn(x3, w['norm_ffn'][i][None], w['ffn_w_in'][i], w['ffn_conv_w'][i], w['ffn_conv_b'][i][None],
                      w['ffn_w_down'][i], ffn_buf, i, ff_bb, ff_tt)
        new_ffn.append(fb)
    return (x3, new_a, jnp.stack(new_hg), jnp.stack(new_rw), jnp.stack(new_sh), jnp.stack(new_ffn))


def kernel(x_prompt, x_sample, mem_prompt, cache_attn_kv_w128, cache_attn_kv_w512, cache_attn_kv_w2048, state_hgrn, state_rwkv, state_rwkv_shift, state_ffn_conv, cache_mem_kv, norm_mix, norm_mem, norm_ffn, mem_norm, attn_w_qkv, attn_q_gain, attn_k_gain, attn_w_o, hg_w_in, hg_lb_logits, hg_out_gain, hg_w_o, rw_mu, rw_w_rkv, rw_w0, rw_w1, rw_w2, rw_a0, rw_a1, rw_a2, rw_g1, rw_g2, rw_k_k, rw_k_a, rw_r_k, rw_ln_g, rw_ln_b, rw_w_o, xa_w_q, xa_w_kv, xa_q_gain, xa_k_gain, xa_w_o, ffn_w_in, ffn_conv_w, ffn_conv_b, ffn_w_down):
    d = D_MODEL
    bp = x_prompt.shape[0]
    bs = x_sample.shape[0]
    n_a, n_b, n_c = attn_w_qkv.shape[0], hg_w_in.shape[0], rw_w_rkv.shape[0]
    bf = lambda a: a.astype(BF16)

    wq4 = attn_w_qkv.reshape(n_a, d, 3, N_GROUPS, A_WIDTH)
    w_qkv = jnp.concatenate([wq4[:, :, 0].reshape(n_a, d, N_GROUPS * A_WIDTH),
                             jnp.swapaxes(wq4[:, :, 1:3], 2, 3).reshape(n_a, d, N_GROUPS * 2 * A_WIDTH)], axis=-1)
    tile_h = lambda gn: jnp.broadcast_to(gn[:, :, None, :], (n_a, N_GROUPS, A_HEADS, A_HEAD_DIM))
    kgain = jnp.stack([tile_h(attn_k_gain), jnp.ones((n_a, N_GROUPS, A_HEADS, A_HEAD_DIM), F32)], axis=2)
    head_gain = jnp.concatenate([tile_h(attn_q_gain).reshape(n_a, 1, -1), kgain.reshape(n_a, 1, -1)], axis=-1)

    w = {
        'norm_mix': norm_mix, 'norm_mem': norm_mem, 'norm_ffn': norm_ffn,
        'attn_w_qkv': bf(w_qkv), 'attn_head_gain': head_gain, 'attn_w_o': bf(attn_w_o),
        'hg_w_in': bf(hg_w_in), 'hg_lb_logits': hg_lb_logits, 'hg_out_gain': hg_out_gain, 'hg_w_o': bf(hg_w_o),
        'rw_mu': rw_mu, 'rw_w_rkv': bf(rw_w_rkv), 'rw_w0': rw_w0, 'rw_w1': bf(rw_w1), 'rw_w2': bf(rw_w2),
        'rw_a0': rw_a0, 'rw_a1': bf(rw_a1), 'rw_a2': bf(rw_a2), 'rw_g1': bf(rw_g1), 'rw_g2': bf(rw_g2),
        'rw_k_k': rw_k_k, 'rw_k_a': rw_k_a, 'rw_r_k': rw_r_k, 'rw_ln_g': rw_ln_g, 'rw_ln_b': rw_ln_b,
        'rw_w_o': bf(rw_w_o),
        'xa_w_q': bf(xa_w_q), 'xa_q_gain': xa_q_gain, 'xa_w_o': bf(xa_w_o),
        'ffn_w_in': bf(ffn_w_in), 'ffn_conv_w': ffn_conv_w, 'ffn_conv_b': ffn_conv_b, 'ffn_w_down': bf(ffn_w_down),
    }

    mem_kv_prompt, mem_kv_flat = _memory_kv(mem_prompt.reshape(bp * N_MEM, d), mem_norm[:, None, :], bf(xa_w_kv),
                                            xa_k_gain[:, None, :], bp)

    as_rows = lambda cc: cc.reshape(cc.shape[0], cc.shape[1], cc.shape[2] * cc.shape[3] * cc.shape[4], cc.shape[5])
    as_cache = lambda rr: rr.reshape(rr.shape[0], rr.shape[1], rr.shape[2] // KV_ROWS, 2, A_HEADS, A_HEAD_DIM)
    y_prompt, a_p, hg_p, rw_p, sh_p, ffn_p = _trunk(
        x_prompt, mem_kv_flat, None,
        jnp.zeros((n_b, bp, HG_HEADS, HG_DK, HG_DK), F32),
        jnp.zeros((n_c, bp, RW_HEADS, RW_HEAD_DIM, RW_HEAD_DIM), F32),
        jnp.zeros((n_c, bp, d), F32),
        jnp.zeros((DEPTH, bp, CONV_W - 1, D_FF), F32), w, True)
    y_sample, a_s, hg_s, rw_s, sh_s, ffn_s = _trunk(
        x_sample, cache_mem_kv,
        tuple(as_rows(cc) for cc in (cache_attn_kv_w128, cache_attn_kv_w512, cache_attn_kv_w2048)),
        state_hgrn, state_rwkv, state_rwkv_shift, state_ffn_conv, w, False)
    a_p = tuple(as_cache(rr) for rr in a_p)
    a_s = tuple(as_cache(rr) for rr in a_s)
    return (y_prompt, y_sample, a_p[0], a_p[1], a_p[2], hg_p, rw_p, sh_p, ffn_p, mem_kv_prompt,
            a_s[0], a_s[1], a_s[2], hg_s, rw_s, sh_s, ffn_s)
```

```python
import functools
import math

import numpy as np
import jax
import jax.numpy as jnp
from jax import lax
from jax.experimental import pallas as pl
from jax.experimental.pallas import tpu as pltpu

F32 = jnp.float32
BF16 = jnp.bfloat16

D_MODEL = 1024
DEPTH = 4
N_MIXERS = 3
DIL_WINDOWS = (128, 512, 2048)
DIL_RATES = (1, 4, 16)
N_GROUPS = 3
A_SPAN = 128
A_HEADS = 4
A_HEAD_DIM = 128
A_BLOCK = 128
A_UNROLL = 4
A_WIDTH = A_HEADS * A_HEAD_DIM
KV_ROWS = 2 * A_HEADS
HG_HEADS = 8
HG_DK = 128
HG_CHUNK = 128
HG_SHORT_CHUNK = 16
RW_HEADS = 16
RW_HEAD_DIM = 64
RW_CHUNK = 64
RW_GN_EPS = 64e-5
N_MEM = 256
XA_HEADS = 4
XA_HEAD_DIM = D_MODEL // XA_HEADS
D_FF = 2816
CONV_W = 3
FF_GROUP = 4
RMS_EPS = 1e-6
LANES = 128
VMEM_LIMIT = 52 * 1024 * 1024

NT_DIMS = (((1,), (1,)), ((), ()))
TN_DIMS = (((0,), (0,)), ((), ()))


def _params(n_axes):
    return pltpu.CompilerParams(dimension_semantics=("arbitrary",) * n_axes, vmem_limit_bytes=VMEM_LIMIT)


def _rms(x):
    return x * lax.rsqrt(jnp.mean(x * x, axis=-1, keepdims=True) + RMS_EPS)


def _dot(a, b):
    return jnp.dot(a.astype(BF16), b.astype(BF16), preferred_element_type=F32)


def _dot_nt(a, b):
    return lax.dot_general(a.astype(BF16), b.astype(BF16), NT_DIMS, preferred_element_type=F32)


def _dot_tn(a, b):
    return lax.dot_general(a.astype(BF16), b.astype(BF16), TN_DIMS, preferred_element_type=F32)


def _split3(x):
    p1 = x.astype(BF16)
    r1 = x - p1.astype(F32)
    p2 = r1.astype(BF16)
    p3 = (r1 - p2.astype(F32)).astype(BF16)
    return jnp.concatenate([p1, p2, p3], axis=1)


def _exact_rowmix(mat_bf16, x):
    w = x.shape[1]
    y = jnp.dot(mat_bf16, _split3(x), preferred_element_type=F32)
    return y[:, :w] + y[:, w:2 * w] + y[:, 2 * w:]


def _softplus(x):
    return jnp.maximum(x, 0.0) + jnp.log1p(jnp.exp(-jnp.abs(x)))


def _sigmoid(x):
    return 1.0 / (1.0 + jnp.exp(-x))


def _silu(x):
    return x * _sigmoid(x)


def _nmm_body(x_ref, g_ref, w_ref, hg_ref, o_ref, *, hn_width, hn_tiles, tn):
    h = (_rms(x_ref[...]) * g_ref[...]).astype(BF16)
    for c in range(w_ref.shape[1] // tn):
        y = jnp.dot(h, w_ref[:, c * tn:(c + 1) * tn], preferred_element_type=F32)
        if hn_width is not None and hn_tiles[c]:
            for cc in range(tn // hn_width):
                sl = slice(c * tn + cc * hn_width, c * tn + (cc + 1) * hn_width)
                o_ref[:, sl] = _rms(y[:, cc * hn_width:(cc + 1) * hn_width]) * hg_ref[:, sl]
        else:
            o_ref[:, c * tn:(c + 1) * tn] = y


def _norm_matmul(x, gain, w, head_gain=None, hn_width=None, hn_tiles=None, tm=256, tn=512):
    m, d = x.shape
    n = w.shape[1]
    tm = min(tm, m)
    if head_gain is None:
        head_gain = jnp.ones((1, n), F32)
    return pl.pallas_call(
        functools.partial(_nmm_body, hn_width=hn_width, hn_tiles=hn_tiles, tn=tn),
        grid=(m // tm,),
        in_specs=[
            pl.BlockSpec((tm, d), lambda i: (i, 0)),
            pl.BlockSpec((1, d), lambda i: (0, 0)),
            pl.BlockSpec((d, n), lambda i: (0, 0)),
            pl.BlockSpec((1, n), lambda i: (0, 0)),
        ],
        out_specs=pl.BlockSpec((tm, n), lambda i: (i, 0)),
        out_shape=jax.ShapeDtypeStruct((m, n), F32),
        compiler_params=_params(1),
        name="norm_matmul",
    )(x, gain, w, head_gain)


def _ds(start, size, stride):
    return pl.ds(start, size) if stride == 1 else pl.ds(start, size, stride=stride)


def _attn_prompt_body(q0, q1, q2, k0, v0, k1, v1, k2, v2, o_ref, acc, m_s, l_s, *, seq):
    scale = A_HEAD_DIM ** -0.5
    blk = A_BLOCK
    for g, (q_ref, k_ref, v_ref) in enumerate(((q0, k0, v0), (q1, k1, v1), (q2, k2, v2))):
        dil = DIL_RATES[g]
        n = seq // dil
        nb = n // blk
        width = 2 * blk if nb > 1 else blk
        ii = lax.broadcasted_iota(jnp.int32, (blk, width), 0)
        jj = lax.broadcasted_iota(jnp.int32, (blk, width), 1)
        if nb > 1:
            in_cur = (jj >= blk) & (jj - blk <= ii)
            in_prev = (jj < blk) & (jj >= ii)
        else:
            in_cur, in_prev = jj <= ii, None

        def body(it, carry, g=g, dil=dil, nb=nb, q_ref=q_ref, k_ref=k_ref, v_ref=v_ref, in_cur=in_cur,
                 in_prev=in_prev):
            rows, ss, vws = [], [], []
            for uu in range(A_UNROLL):
                idx = it * A_UNROLL + uu
                r = idx // nb
                mb = idx % nb
                rw = _ds(r + dil * blk * mb, blk, dil)
                kc = k_ref[rw, :].astype(BF16)
                vc = v_ref[rw, :].astype(BF16)
                if nb > 1:
                    prev = _ds(r + dil * blk * jnp.maximum(mb - 1, 0), blk, dil)
                    kw = jnp.concatenate([k_ref[prev, :].astype(BF16), kc], axis=0)
                    vw = jnp.concatenate([v_ref[prev, :].astype(BF16), vc], axis=0)
                    prev_bias = jnp.where(mb > 0, 0.0, -jnp.inf)
                    bias = jnp.where(in_cur, 0.0, jnp.where(in_prev, prev_bias, -jnp.inf))
                else:
                    kw, vw = kc, vc
                    bias = jnp.where(in_cur, 0.0, -jnp.inf)
                rows.append(rw)
                vws.append(vw)
                ss.append(_dot_nt(q_ref[rw, :], kw) * scale + bias)
            ms = [jnp.max(s, axis=-1, keepdims=True) for s in ss]
            ps = [jnp.exp(s - m) for s, m in zip(ss, ms)]
            ls = [jnp.sum(p, axis=-1, keepdims=True) for p in ps]
            os_ = [_dot(p, vw) for p, vw in zip(ps, vws)]
            m_bs = [jnp.broadcast_to(m, (blk, LANES)) for m in ms]
            l_bs = [jnp.broadcast_to(l, (blk, LANES)) for l in ls]
            if g == 0:
                for rw, o, m_b, l_b in zip(rows, os_, m_bs, l_bs):
                    acc[rw, :] = o
                    m_s[rw, :] = m_b
                    l_s[rw, :] = l_b
            else:
                olds = [(m_s[rw, :], l_s[rw, :], acc[rw, :]) for rw in rows]
                for rw, o, m_b, l_b, (m_old, l_old, acc_old) in zip(rows, os_, m_bs, l_bs, olds):
                    m_new = jnp.maximum(m_old, m_b)
                    a_old = jnp.exp(m_old - m_new)
                    a_cur = jnp.exp(m_b - m_new)
                    acc[rw, :] = acc_old * a_old + o * a_cur
                    l_s[rw, :] = l_old * a_old + l_b * a_cur
                    m_s[rw, :] = m_new
            return carry

        assert (dil * nb) % A_UNROLL == 0
        lax.fori_loop(0, dil * nb // A_UNROLL, body, 0)
    o_ref[...] = acc[...] / l_s[...]


def _attn_prompt(qkv, batch, seq):
    m = batch * seq
    assert seq % (A_BLOCK * max(DIL_RATES)) == 0 and A_BLOCK == A_SPAN
    nq = N_GROUPS * A_HEADS

    def spec(col_fn):
        return pl.BlockSpec((seq, A_HEAD_DIM), lambda b, h: (b, col_fn(h)))

    in_specs = [spec(lambda h, g=g: g * A_HEADS + h) for g in range(N_GROUPS)]
    for g in range(N_GROUPS):
        in_specs.append(spec(lambda h, g=g: nq + g * 2 * A_HEADS + h))
        in_specs.append(spec(lambda h, g=g: nq + g * 2 * A_HEADS + A_HEADS + h))
    return pl.pallas_call(
        functools.partial(_attn_prompt_body, seq=seq),
        grid=(batch, A_HEADS),
        in_specs=in_specs,
        out_specs=pl.BlockSpec((seq, A_HEAD_DIM), lambda b, h: (b, h)),
        out_shape=jax.ShapeDtypeStruct((m, A_WIDTH), F32),
        scratch_shapes=[pltpu.VMEM((seq, A_HEAD_DIM), F32), pltpu.VMEM((seq, LANES), F32),
                        pltpu.VMEM((seq, LANES), F32)],
        compiler_params=_params(2),
        name="dilated_attn_prompt",
    )(*([qkv] * 9))


def _attn_sample_body(*refs, t_new, n_alias):
    qkv_ref, bufs = refs[0], refs[1:4]
    o_ref, outs = refs[4 + n_alias], refs[5 + n_alias:8 + n_alias]
    scale = A_HEAD_DIM ** -0.5
    pad = jnp.zeros((LANES - t_new, A_HEAD_DIM), F32)
    nq = N_GROUPS * A_WIDTH

    def new_rows(g, kv, h):
        base = nq + (2 * g + kv) * A_WIDTH + h * A_HEAD_DIM
        return qkv_ref[:, base:base + A_HEAD_DIM]

    for g in range(N_GROUPS):
        length = bufs[g].shape[0] // KV_ROWS
        keep = (length - t_new) * KV_ROWS
        outs[g][0:keep, :] = bufs[g][t_new * KV_ROWS:length * KV_ROWS, :]
        for kv in range(2):
            for h in range(A_HEADS):
                outs[g][pl.ds(keep + kv * A_HEADS + h, t_new, stride=KV_ROWS), :] = new_rows(g, kv, h)

    for h in range(A_HEADS):
        parts, vals = [], []
        for g in range(N_GROUPS):
            dil = DIL_RATES[g]
            length = bufs[g].shape[0] // KV_ROWS
            q = qkv_ref[:, g * A_WIDTH + h * A_HEAD_DIM: g * A_WIDTH + (h + 1) * A_HEAD_DIM]
            s_buf = _dot_nt(q, bufs[g][pl.ds(h, length, stride=KV_ROWS), :]) * scale
            s_new = _dot_nt(q, jnp.concatenate([new_rows(g, 0, h), pad], axis=0)) * scale
            for s, base in ((s_buf, 0), (s_new, length)):
                ii = lax.broadcasted_iota(jnp.int32, s.shape, 0)
                jj = lax.broadcasted_iota(jnp.int32, s.shape, 1) + base
                dd = jj - ii
                valid = (dd >= 0) & (dd <= A_SPAN * dil) & ((dd & (dil - 1)) == 0) & (jj < length + t_new)
                parts.append(jnp.where(valid, s, -jnp.inf))
            vals.append(bufs[g][pl.ds(A_HEADS + h, length, stride=KV_ROWS), :])
            vals.append(jnp.concatenate([new_rows(g, 1, h), pad], axis=0))
        mx = functools.reduce(jnp.maximum, [jnp.max(s, axis=-1, keepdims=True) for s in parts])
        exps = [jnp.exp(s - mx) for s in parts]
        den = functools.reduce(lambda a, b: a + b, [jnp.sum(e, axis=-1, keepdims=True) for e in exps])
        o = functools.reduce(lambda a, b: a + b, [_dot(e / den, vv) for e, vv in zip(exps, vals)])
        o_ref[:, h * A_HEAD_DIM:(h + 1) * A_HEAD_DIM] = o


def _attn_sample(qkv, caches, prev_out, j, batch, t_new):
    assert t_new % 8 == 0 and t_new <= LANES
    lens = [cch.shape[2] // KV_ROWS for cch in caches]
    for g in range(N_GROUPS):
        assert lens[g] == A_SPAN * DIL_RATES[g] and (DIL_RATES[g] & (DIL_RATES[g] - 1)) == 0

    def cache_spec(g):
        return pl.BlockSpec((None, None, lens[g] * KV_ROWS, A_HEAD_DIM), lambda b: (j, b, 0, 0))

    in_specs = [pl.BlockSpec((t_new, qkv.shape[1]), lambda b: (b, 0))] + [cache_spec(g) for g in range(N_GROUPS)]
    n_alias = 0 if prev_out is None else N_GROUPS
    aliases = {}
    if prev_out is not None:
        in_specs += [pl.BlockSpec(memory_space=pl.ANY)] * N_GROUPS
        aliases = {4 + g: 1 + g for g in range(N_GROUPS)}
    out_specs = [pl.BlockSpec((t_new, A_WIDTH), lambda b: (b, 0))] + [cache_spec(g) for g in range(N_GROUPS)]
    out_shape = [jax.ShapeDtypeStruct((batch * t_new, A_WIDTH), F32)]
    out_shape += [jax.ShapeDtypeStruct(cch.shape, F32) for cch in caches]
    return pl.pallas_call(
        functools.partial(_attn_sample_body, t_new=t_new, n_alias=n_alias),
        grid=(batch,),
        in_specs=in_specs,
        out_specs=out_specs,
        out_shape=out_shape,
        input_output_aliases=aliases,
        compiler_params=_params(1),
        name="dilated_attn_sample",
    )(qkv, *caches, *(prev_out or ()))


def _kv_cache_body(*refs):
    k_ref, v_ref, o_ref = refs[0], refs[1], refs[-1]
    tr = k_ref.shape[0]
    for kv, x_ref in enumerate((k_ref, v_ref)):
        for h in range(A_HEADS):
            o_ref[pl.ds(kv * A_HEADS + h, tr, stride=KV_ROWS), :] = x_ref[:, h * A_HEAD_DIM:(h + 1) * A_HEAD_DIM]


def _kv_cache_out(qkv, prev_out, g, j, n_layers, batch, seq):
    win = min(DIL_WINDOWS[g], seq)
    tr = min(win, 512)
    first = (seq - win) // tr
    row = lambda b, t: b * (seq // tr) + first + t
    in_specs = [pl.BlockSpec((tr, A_WIDTH), lambda b, t: (row(b, t), N_GROUPS + 2 * g)),
                pl.BlockSpec((tr, A_WIDTH), lambda b, t: (row(b, t), N_GROUPS + 2 * g + 1))]
    aliases = {}
    if prev_out is not None:
        in_specs.append(pl.BlockSpec(memory_space=pl.ANY))
        aliases = {2: 0}
    return pl.pallas_call(
        _kv_cache_body,
        grid=(batch, win // tr),
        in_specs=in_specs,
        out_specs=pl.BlockSpec((None, None, tr * KV_ROWS, A_HEAD_DIM), lambda b, t: (j, b, t, 0)),
        out_shape=jax.ShapeDtypeStruct((n_layers, batch, win * KV_ROWS, A_HEAD_DIM), F32),
        input_output_aliases=aliases,
        compiler_params=_params(2),
        name="kv_cache_out",
    )(qkv, qkv, *(() if prev_out is None else (prev_out,)))


def _hgrn_consts(c):
    t = np.arange(c)[:, None]
    u = np.arange(c)[None, :]
    mats = [(u <= t), (u > t)]
    masks = [(u == t)]
    s = 1
    while s < c:
        upper = (t // s) % 2 == 1
        mid = (t // (2 * s)) * 2 * s + s - 1
        mats.append(np.where(upper, (u > mid) & (u <= t), (u > t) & (u <= mid)))
        lower_u = (u // s) % 2 == 0
        masks.append(upper & lower_u & (t // (2 * s) == u // (2 * s)))
        s *= 2
    mats = np.concatenate([m.astype(np.float32) for m in mats], axis=0)
    masks = np.stack([m.astype(np.float32) for m in masks], axis=0)
    return jnp.asarray(mats, BF16), jnp.asarray(masks, F32)


def _hgrn_body(q_ref, z_ref, i_ref, g_ref, lbl_ref, og_ref, s0_ref, mat_ref, msk_ref, o_ref, st_ref, st_scr,
               *, layer, t_len, c, hp):
    n_lvl = msk_ref.shape[0] - 1
    width = hp * HG_DK
    hr = range(hp)
    lane = lambda x, hd: x[:, hd * HG_DK:(hd + 1) * HG_DK]
    lg = lbl_ref[...]
    e = jnp.exp(lg - jnp.max(lg, axis=0, keepdims=True))
    prob = e / jnp.sum(e, axis=0, keepdims=True)
    lb = jnp.sum(prob[0:layer + 1], axis=0, keepdims=True) - prob[0:1]
    log_lb = jnp.log(lb)
    log_1m = jnp.log1p(-lb)
    @pl.when(pl.program_id(2) == 0)
    def _():
        for hd in hr:
            st_scr[hd] = s0_ref[hd].T

    row = lax.broadcasted_iota(jnp.int32, (c, width), 0)
    out_gain = jnp.concatenate([og_ref[...]] * hp, axis=1)

    def chunk(ci, carry):
        if t_len >= c:
            rows = pl.ds(pl.multiple_of(ci * c, c), c)
            ld = lambda ref: ref[rows, :]
            live = None
        else:
            padz = jnp.zeros((c - t_len, width), F32)
            ld = lambda ref: jnp.concatenate([ref[...], padz], axis=0)
            live = row < t_len
        z = ld(z_ref)
        lsig = jnp.minimum(z, 0.0) - jnp.log1p(jnp.exp(-jnp.abs(z)))
        a1 = jnp.broadcast_to(log_lb, z.shape)
        a2 = log_1m + lsig
        hi = jnp.maximum(a1, a2)
        lo = jnp.minimum(a1, a2)
        log_f = hi + jnp.log1p(jnp.exp(lo - hi))
        kg = (1.0 - lb) * _sigmoid(-z)
        if live is not None:
            log_f = jnp.where(live, log_f, 0.0)
            kg = jnp.where(live, kg, 0.0)
        qa = _silu(ld(q_ref))
        vv = ld(i_ref)
        ex = _exact_rowmix(mat_ref[...], log_f)
        b = ex[0:c]
        sts = [st_scr[hd] for hd in hr]
        a = [_dot_nt(lane(qa, hd), lane(kg, hd)) * msk_ref[0] for hd in hr]
        for lv in range(n_lvl):
            wv = jnp.exp(ex[(2 + lv) * c:(3 + lv) * c])
            qw = qa * wv
            kw = kg * wv
            a = [a[hd] + _dot_nt(lane(qw, hd), lane(kw, hd)) * msk_ref[1 + lv] for hd in hr]
        qb = qa * jnp.exp(b)
        kd = kg * jnp.exp(ex[c:2 * c])
        dec = jnp.exp(b[c - 1:c])
        o = [_dot_nt(lane(qb, hd), sts[hd]) + _dot(a[hd], lane(vv, hd)) for hd in hr]
        st_scr[...] = jnp.stack([sts[hd] * lane(dec, hd) + _dot_tn(lane(vv, hd), lane(kd, hd)) for hd in hr], axis=0)
        y = jnp.concatenate([_rms(x) for x in o], axis=1) * out_gain * _silu(ld(g_ref))
        if t_len >= c:
            o_ref[rows, :] = y
        else:
            o_ref[...] = y[0:t_len]
        return carry

    lax.fori_loop(0, max(t_len // c, 1), chunk, 0)
    for hd in hr:
        st_ref[hd] = st_scr[hd].T


def _hgrn(proj, lb_logits, out_gain, s0, layer, batch, t_len, tt=512):
    tt = min(tt, t_len)
    if tt >= HG_CHUNK:
        c, hp = HG_CHUNK, 4
        assert tt % c == 0 and t_len % tt == 0
    else:
        c, hp = HG_SHORT_CHUNK, HG_HEADS
        assert tt % 8 == 0 and tt <= c
    mats, masks = _hgrn_consts(c)
    width = hp * HG_DK
    ng = HG_HEADS // hp
    nt = t_len // tt

    def col(part):
        return pl.BlockSpec((tt, width), lambda b, h, t: (b * nt + t, part * ng + h))

    st_spec = pl.BlockSpec((None, hp, HG_DK, HG_DK), lambda b, h, t: (b, h, 0, 0))
    return pl.pallas_call(
        functools.partial(_hgrn_body, layer=layer, t_len=tt, c=c, hp=hp),
        grid=(batch, ng, nt),
        in_specs=[col(0), col(1), col(2), col(3),
                  pl.BlockSpec((DEPTH, width), lambda b, h, t: (0, h)),
                  pl.BlockSpec((1, HG_DK), lambda b, h, t: (0, 0)),
                  st_spec,
                  pl.BlockSpec(mats.shape, lambda b, h, t: (0, 0)),
                  pl.BlockSpec(masks.shape, lambda b, h, t: (0, 0, 0))],
        out_specs=[pl.BlockSpec((tt, width), lambda b, h, t: (b * nt + t, h)), st_spec],
        out_shape=[jax.ShapeDtypeStruct((batch * t_len, HG_HEADS * HG_DK), F32),
                   jax.ShapeDtypeStruct((batch, HG_HEADS, HG_DK, HG_DK), F32)],
        scratch_shapes=[pltpu.VMEM((hp, HG_DK, HG_DK), F32)],
        compiler_params=_params(3),
        name="hgrn2_chunked",
    )(proj, proj, proj, proj, lb_logits, out_gain, s0, mats, masks)


def _rwkv_proj_body(x_ref, g_ref, sh_ref, mu_ref, wrkv_ref, w0_ref, w1_ref, w2_ref, a0_ref, a1_ref, a2_ref,
                    g1_ref, g2_ref, kk_ref, ka_ref,
                    r_o, k_o, v_o, ld_o, kk_o, a_o, g_o, sh_o, carry, *, bb, tt):
    t = pl.program_id(1)
    d = D_MODEL

    @pl.when(t == 0)
    def _():
        carry[...] = sh_ref[...]

    h3 = _rms(x_ref[...]) * g_ref[...]
    h = h3.reshape(bb * tt, d)
    rolled = pltpu.roll(h, 1, 0).reshape(bb, tt, d)
    tpos = lax.broadcasted_iota(jnp.int32, (bb, tt, d), 1)
    prev = jnp.where(tpos == 0, carry[...], rolled).reshape(bb * tt, d)
    last = h3[:, tt - 1:tt, :]
    carry[...] = last
    sh_o[...] = last
    dx = prev - h
    mix = lambda jm: h + dx * mu_ref[jm:jm + 1, :]
    r = _dot(mix(0), wrkv_ref[0])
    k = _dot(mix(1), wrkv_ref[1])
    v = _dot(mix(2), wrkv_ref[2])
    wl = w0_ref[...] + _dot(jnp.tanh(_dot(mix(3), w1_ref[...])), w2_ref[...])
    wlog = -_softplus(-wl) - 0.5
    a = _sigmoid(a0_ref[...] + _dot(_dot(mix(4), a1_ref[...]), a2_ref[...]))
    gate = _dot(_sigmoid(_dot(mix(5), g1_ref[...])), g2_ref[...])
    r_o[...] = r
    k_o[...] = k * (1.0 + (a - 1.0) * ka_ref[...])
    v_o[...] = v
    ld_o[...] = -jnp.exp(wlog)
    kk_o[...] = k * kk_ref[...]
    a_o[...] = a
    g_o[...] = gate


def _rwkv_proj(x3, gain, shift, mu, wrkv, w0, w1, w2, a0, a1, a2, g1, g2, k_k, k_a, bb, tt):
    batch, t_len, d = x3.shape
    m = batch * t_len
    rows = bb * tt
    full = lambda arr: pl.BlockSpec(arr.shape, lambda b, t: (0,) * arr.ndim)
    row_spec = pl.BlockSpec((rows, d), lambda b, t: (b * (t_len // tt) + t, 0))
    outs = pl.pallas_call(
        functools.partial(_rwkv_proj_body, bb=bb, tt=tt),
        grid=(batch // bb, t_len // tt),
        in_specs=[pl.BlockSpec((bb, tt, d), lambda b, t: (b, t, 0)), full(gain),
                  pl.BlockSpec((bb, 1, d), lambda b, t: (b, 0, 0)), full(mu), full(wrkv),
                  full(w0), full(w1), full(w2), full(a0), full(a1), full(a2), full(g1), full(g2),
                  full(k_k), full(k_a)],
        out_specs=[row_spec] * 7 + [pl.BlockSpec((bb, 1, d), lambda b, t: (b, 0, 0))],
        out_shape=[jax.ShapeDtypeStruct((m, d), F32)] * 7 + [jax.ShapeDtypeStruct((batch, 1, d), F32)],
        scratch_shapes=[pltpu.VMEM((bb, 1, d), F32)],
        compiler_params=_params(2),
        name="rwkv_proj",
    )(x3, gain, shift, mu, wrkv, w0, w1, w2, a0, a1, a2, g1, g2, k_k, k_a)
    return outs


def _rwkv_consts(c):
    n = RW_HEAD_DIM
    t = np.arange(c)[:, None]
    u = np.arange(c)[None, :]
    f = lambda mat, dt: jnp.asarray(np.asarray(mat).astype(np.float32), dt)
    tri = f(u <= t, BF16)
    half = np.concatenate([u < t, u <= t], axis=0)
    lo = f(np.concatenate([half, half], axis=1), F32)
    eye = f(u == t, F32)
    lane = np.arange(LANES)
    bdiag = f((lane[:, None] // n) == (lane[None, :] // n), F32)
    return tri, lo, eye, bdiag


def _rwkv_rec_body(r_ref, k_ref, v_ref, ld_ref, kk_ref, a_ref, g_ref, rk_ref, lng_ref, lnb_ref, s0_ref,
                   tri_ref, lo_ref, eye_ref, bd_ref, o_ref, st_ref, st_scr, *, tt):
    c = RW_CHUNK
    n = RW_HEAD_DIM
    heads = RW_HEADS
    pairs = heads // 2
    w = D_MODEL
    t = pl.program_id(1)

    @pl.when(t == 0)
    def _():
        zero = jnp.zeros((n, n), F32)
        for p in range(pairs):
            top = jnp.concatenate([s0_ref[2 * p], zero], axis=1)
            bot = jnp.concatenate([zero, s0_ref[2 * p + 1]], axis=1)
            st_scr[p] = jnp.concatenate([top, bot], axis=0)

    low_2c = lax.broadcasted_iota(jnp.int32, (2 * c, LANES), 1) < n
    low_c = lax.broadcasted_iota(jnp.int32, (c, LANES), 1) < n
    tile = lambda x, p: x[:, p * LANES:(p + 1) * LANES]
    hr = range(heads)
    pr = range(pairs)
    zeros_c = jnp.zeros((c, LANES), F32)

    def head_sums(x):
        out = []
        for p in pr:
            xt = tile(x, p)
            lo_s = jnp.sum(jnp.where(low_c, xt, 0.0), axis=-1, keepdims=True)
            hi_s = jnp.sum(jnp.where(low_c, 0.0, xt), axis=-1, keepdims=True)
            out.append(jnp.where(low_c, lo_s, hi_s))
        return jnp.concatenate(out, axis=1)

    def chunk(ci, carry):
        if tt >= c:
            rows = pl.ds(pl.multiple_of(ci * c, c), c)
            ld = lambda ref: ref[rows, :]
        else:
            padz = jnp.zeros((c - tt, w), F32)
            ld = lambda ref: jnp.concatenate([ref[...], padz], axis=0)
        r, kf, v, lc, kk, a, gate = (ld(ref) for ref in (r_ref, k_ref, v_ref, ld_ref, kk_ref, a_ref, g_ref))
        kk = kk / jnp.maximum(jnp.sqrt(head_sums(kk * kk)), 1e-12)
        bvec = kk * a
        cum = _exact_rowmix(tri_ref[...], lc)
        c_end = cum[c - 1:c]
        e_neg = jnp.exp(-cum)
        e_end = jnp.exp(c_end - cum)
        d_end = jnp.exp(c_end)
        kr = jnp.concatenate([kk * jnp.exp(cum - lc), r * jnp.exp(cum)], axis=0)
        bx = jnp.concatenate([bvec * e_neg, kf * e_neg], axis=0)
        bk = jnp.concatenate([bvec * e_end, kf * e_end], axis=0)
        lo = lo_ref[...]
        eye = eye_ref[...]
        krm = [jnp.where(low_2c if h % 2 == 0 else jnp.logical_not(low_2c), tile(kr, h // 2), 0.0) for h in hr]
        gm = [_dot_nt(krm[h], tile(bx, h // 2)) * lo for h in hr]
        nn = [x[0:c, 0:c] for x in gm]
        tm = [eye - x for x in nn]
        npow = [_dot(x, x) for x in nn]
        p2 = 2
        while p2 < c:
            tm = [x + _dot(x, y) for x, y in zip(tm, npow)]
            p2 *= 2
            if p2 < c:
                npow = [_dot(x, x) for x in npow]
        sts = [st_scr[p] for p in pr]
        ks = [_dot_nt(tile(kr, p), sts[p]) for p in pr]
        vz = [jnp.concatenate([zeros_c, tile(v, p)], axis=0) for p in pr]
        av = [jnp.where(low_2c, _dot(gm[2 * p], vz[p]), _dot(gm[2 * p + 1], vz[p])) for p in pr]
        rhs = [ks[p][0:c] + av[p][0:c] for p in pr]
        u = [jnp.where(low_c, -_dot(tm[2 * p], rhs[p]), -_dot(tm[2 * p + 1], rhs[p])) for p in pr]
        uz = [jnp.concatenate([u[p], zeros_c], axis=0) for p in pr]
        y = [ks[p][c:2 * c] + av[p][c:2 * c]
             + jnp.where(low_c, _dot(gm[2 * p][c:2 * c], uz[p]), _dot(gm[2 * p + 1][c:2 * c], uz[p])) for p in pr]
        new_st = [sts[p] * tile(d_end, p)
                  + _dot_tn(jnp.concatenate([u[p], tile(v, p)], axis=0), tile(bk, p)) * bd_ref[...] for p in pr]
        st_scr[...] = jnp.stack(new_st, axis=0)
        yy = jnp.concatenate(y, axis=1)
        yc = yy - head_sums(yy) * (1.0 / n)
        var = head_sums(yc * yc) * (1.0 / n)
        yn = yc * lax.rsqrt(var + RW_GN_EPS) * lng_ref[...] + lnb_ref[...]
        z = (yn + head_sums(r * kf * rk_ref[...]) * v) * gate
        if tt >= c:
            o_ref[rows, :] = z
        else:
            o_ref[...] = z[0:tt]
        return carry

    lax.fori_loop(0, max(tt // c, 1), chunk, 0)
    for p in range(pairs):
        blk = st_scr[p]
        st_ref[2 * p] = blk[0:n, 0:n]
        st_ref[2 * p + 1] = blk[n:2 * n, n:2 * n]


def _rwkv_rec(proj, r_k, ln_g, ln_b, s0, batch, t_len, tt=256):
    r, kf, v, ld, kk, a, gate = proj
    tt = min(tt, t_len)
    assert tt % RW_CHUNK == 0 or (tt < RW_CHUNK and tt % 8 == 0)
    assert RW_HEADS * RW_HEAD_DIM == D_MODEL and 2 * RW_HEAD_DIM == LANES and 2 * RW_CHUNK == LANES
    consts = _rwkv_consts(RW_CHUNK)
    nt = t_len // tt
    row_spec = pl.BlockSpec((tt, D_MODEL), lambda b, t: (b * nt + t, 0))
    par_spec = pl.BlockSpec((1, D_MODEL), lambda b, t: (0, 0))
    st_spec = pl.BlockSpec((None, RW_HEADS, RW_HEAD_DIM, RW_HEAD_DIM), lambda b, t: (b, 0, 0, 0))
    const = lambda arr: pl.BlockSpec(arr.shape, lambda b, t: (0, 0))
    return pl.pallas_call(
        functools.partial(_rwkv_rec_body, tt=tt),
        grid=(batch, nt),
        in_specs=[row_spec] * 7 + [par_spec] * 3 + [st_spec] + [const(x) for x in consts],
        out_specs=[row_spec, st_spec],
        out_shape=[jax.ShapeDtypeStruct((batch * t_len, D_MODEL), F32),
                   jax.ShapeDtypeStruct((batch, RW_HEADS, RW_HEAD_DIM, RW_HEAD_DIM), F32)],
        scratch_shapes=[pltpu.VMEM((RW_HEADS // 2, LANES, LANES), F32)],
        compiler_params=_params(2),
        name="rwkv7_chunked",
    )(r, kf, v, ld, kk, a, gate, r_k, ln_g, ln_b, s0, *consts)


def _xattn_body(x_ref, a_ref, wm_ref, g_ref, wq_ref, qg_ref, kv_ref, wo_ref, o_ref, *, bb, tt, native):
    d = D_MODEL
    e = XA_HEAD_DIM
    scale = XA_HEAD_DIM ** -0.5
    x = x_ref[...].reshape(bb * tt, d) + jnp.dot(a_ref[...].astype(BF16), wm_ref[...], preferred_element_type=F32)
    q = jnp.dot((_rms(x) * g_ref[...]).astype(BF16), wq_ref[...], preferred_element_type=F32)
    qn = lambda b, h: _rms(q[b * tt:(b + 1) * tt, h * e:(h + 1) * e]) * qg_ref[...]
    per_batch = []
    for b in range(bb):
        if native:
            rows = N_MEM * XA_HEADS
            s = _dot_nt(jnp.concatenate([qn(b, h) for h in range(XA_HEADS)], axis=0),
                        kv_ref[b, :, 0].reshape(rows, e)) * scale
            own = (lax.broadcasted_iota(jnp.int32, s.shape, 0) // tt
                   == lax.broadcasted_iota(jnp.int32, s.shape, 1) % XA_HEADS)
            s = jnp.where(own, s, -jnp.inf)
            p = jnp.exp(s - jnp.max(s, axis=-1, keepdims=True))
            p = p / jnp.sum(p, axis=-1, keepdims=True)
            oh = _dot(p, kv_ref[b, :, 1].reshape(rows, e))
            per_batch.append(jnp.concatenate([oh[h * tt:(h + 1) * tt] for h in range(XA_HEADS)], axis=-1))
            continue
        heads = []
        for h in range(XA_HEADS):
            s = _dot_nt(qn(b, h), kv_ref[b, :, h * e:(h + 1) * e]) * scale
            p = jnp.exp(s - jnp.max(s, axis=-1, keepdims=True))
            p = p / jnp.sum(p, axis=-1, keepdims=True)
            heads.append(_dot(p, kv_ref[b, :, d + h * e:d + (h + 1) * e]))
        per_batch.append(jnp.concatenate(heads, axis=-1))
    o = jnp.concatenate(per_batch, axis=0) if bb > 1 else per_batch[0]
    y = x + jnp.dot(o.astype(BF16), wo_ref[...], preferred_element_type=F32)
    o_ref[...] = y.reshape(bb, tt, d)


def _xattn(x3, mix_out, w_mix, gain, wq, q_gain, kv_all, layer, wo, bb, tt):
    batch, t_len, d = x3.shape
    km = mix_out.shape[1]
    nt = t_len // tt
    assert bb == 1 or nt == 1
    native = kv_all.ndim == 6
    if native:
        kv_spec = pl.BlockSpec((None, bb, N_MEM, 2, XA_HEADS, XA_HEAD_DIM), lambda b, t: (layer, b, 0, 0, 0, 0))
    else:
        kv_spec = pl.BlockSpec((None, bb, N_MEM, 2 * d), lambda b, t: (layer, b, 0, 0))
    return pl.pallas_call(
        functools.partial(_xattn_body, bb=bb, tt=tt, native=native),
        grid=(batch // bb, t_len // tt),
        in_specs=[pl.BlockSpec((bb, tt, d), lambda b, t: (b, t, 0)),
                  pl.BlockSpec((bb * tt, km), lambda b, t: (b * nt + t, 0)),
                  pl.BlockSpec((km, d), lambda b, t: (0, 0)),
                  pl.BlockSpec((1, d), lambda b, t: (0, 0)),
                  pl.BlockSpec((d, d), lambda b, t: (0, 0)),
                  pl.BlockSpec((1, XA_HEAD_DIM), lambda b, t: (0, 0)),
                  kv_spec,
                  pl.BlockSpec((d, d), lambda b, t: (0, 0))],
        out_specs=pl.BlockSpec((bb, tt, d), lambda b, t: (b, t, 0)),
        out_shape=jax.ShapeDtypeStruct((batch, t_len, d), F32),
        compiler_params=_params(2),
        name="memory_xattn",
    )(x3, mix_out, w_mix, gain, wq, q_gain, kv_all, wo)


def _memkv_body(x_ref, g_ref, w_ref, kg_ref, o_ref, flat_ref, h_scr):
    c = pl.program_id(2)
    e = XA_HEAD_DIM

    @pl.when(c == 0)
    def _():
        h_scr[...] = (_rms(x_ref[...]) * g_ref[...]).astype(BF16)

    y = jnp.dot(h_scr[...], w_ref[...], preferred_element_type=F32)

    @pl.when(c == 0)
    def _():
        for h in range(XA_HEADS):
            kn = _rms(y[:, h * e:(h + 1) * e]) * kg_ref[...]
            o_ref[:, h, :] = kn
            flat_ref[:, h * e:(h + 1) * e] = kn

    @pl.when(c == 1)
    def _():
        flat_ref[...] = y
        for h in range(XA_HEADS):
            o_ref[:, h, :] = y[:, h * e:(h + 1) * e]


def _memory_kv(mem2, mem_norm, w_kv, k_gain, batch):
    d = D_MODEL
    return pl.pallas_call(
        _memkv_body,
        grid=(DEPTH, batch, 2),
        in_specs=[pl.BlockSpec((N_MEM, d), lambda l, b, c: (b, 0)),
                  pl.BlockSpec((None, 1, d), lambda l, b, c: (l, 0, 0)),
                  pl.BlockSpec((None, d, d), lambda l, b, c: (l, 0, c)),
                  pl.BlockSpec((None, 1, XA_HEAD_DIM), lambda l, b, c: (l, 0, 0))],
        out_specs=[pl.BlockSpec((None, None, N_MEM, None, XA_HEADS, XA_HEAD_DIM), lambda l, b, c: (l, b, 0, c, 0, 0)),
                   pl.BlockSpec((None, None, N_MEM, d), lambda l, b, c: (l, b, 0, c))],
        out_shape=[jax.ShapeDtypeStruct((DEPTH, batch, N_MEM, 2, XA_HEADS, XA_HEAD_DIM), F32),
                   jax.ShapeDtypeStruct((DEPTH, batch, N_MEM, 2 * d), F32)],
        scratch_shapes=[pltpu.VMEM((N_MEM, d), BF16)],
        compiler_params=_params(3),
        name="memory_kv",
    )(mem2, mem_norm, w_kv, k_gain)


def _ffn_body(x_ref, g_ref, win_ref, cw_ref, cb_ref, wd_ref, pin_ref, o_ref, st_ref, acc, carry, *, bb, tt, tf):
    t = pl.program_id(1)
    d = D_MODEL

    @pl.when(t == 0)
    def _():
        carry[...] = pin_ref[...]

    x = x_ref[...].reshape(bb * tt, d)
    h = (_rms(x) * g_ref[...]).astype(BF16)
    acc[...] = x
    tpos = lax.broadcasted_iota(jnp.int32, (bb, tt, tf), 1)
    tails, acts = [], []
    nf = D_FF // tf
    for f in range(nf):
        fs = slice(f * tf, (f + 1) * tf)
        u = jnp.dot(h, win_ref[:, fs], preferred_element_type=F32)
        gate = jnp.dot(h, win_ref[:, D_FF + f * tf:D_FF + (f + 1) * tf], preferred_element_type=F32)
        prev = carry[:, :, fs]
        u3 = u.reshape(bb, tt, tf)
        u1 = jnp.where(tpos == 0, prev[:, 1:2, :], pltpu.roll(u, 1, 0).reshape(bb, tt, tf))
        u2 = pltpu.roll(u, 2, 0).reshape(bb, tt, tf)
        u2 = jnp.where(tpos == 0, prev[:, 0:1, :], jnp.where(tpos == 1, prev[:, 1:2, :], u2))
        conv = cb_ref[:, fs] + cw_ref[0:1, fs] * u2 + cw_ref[1:2, fs] * u1 + cw_ref[2:3, fs] * u3
        acts.append((_silu(conv) * gate.reshape(bb, tt, tf)).reshape(bb * tt, tf).astype(BF16))
        tails.append(u3[:, tt - 2:tt, :])
        if len(acts) == FF_GROUP or f == nf - 1:
            lo_f = f + 1 - len(acts)
            acc[...] += jnp.dot(jnp.concatenate(acts, axis=1) if len(acts) > 1 else acts[0],
                                wd_ref[lo_f * tf:(f + 1) * tf, :], preferred_element_type=F32)
            acts = []
    tail = jnp.concatenate(tails, axis=-1)
    carry[...] = tail
    st_ref[...] = tail
    o_ref[...] = acc[...].reshape(bb, tt, d)


def _ffn(x3, gain, w_in, conv_w, conv_b, w_down, state_all, layer, bb, tt, tf=256):
    batch, t_len, d = x3.shape
    assert t_len >= CONV_W - 1 and tt >= CONV_W - 1 and D_FF % tf == 0
    const = lambda arr: pl.BlockSpec(arr.shape, lambda b, t: (0, 0), pipeline_mode=pl.Buffered(1))
    return pl.pallas_call(
        functools.partial(_ffn_body, bb=bb, tt=tt, tf=tf),
        grid=(batch // bb, t_len // tt),
        in_specs=[pl.BlockSpec((bb, tt, d), lambda b, t: (b, t, 0)),
                  const(gain), const(w_in), const(conv_w), const(conv_b), const(w_down),
                  pl.BlockSpec((None, bb, CONV_W - 1, D_FF), lambda b, t: (layer, b, 0, 0))],
        out_specs=[pl.BlockSpec((bb, tt, d), lambda b, t: (b, t, 0)),
                   pl.BlockSpec((bb, CONV_W - 1, D_FF), lambda b, t: (b, 0, 0))],
        out_shape=[jax.ShapeDtypeStruct((batch, t_len, d), F32),
                   jax.ShapeDtypeStruct((batch, CONV_W - 1, D_FF), F32)],
        scratch_shapes=[pltpu.VMEM((bb * tt, d), F32), pltpu.VMEM((bb, CONV_W - 1, D_FF), F32)],
        compiler_params=_params(2),
        name="conv_ffn",
    )(x3, gain, w_in, conv_w, conv_b, w_down, state_all)


def _trunk(x3, mem_kv, a_bufs, hg_s, rw_s, rw_shift, ffn_buf, w, prompt):
    batch, t_len, d = x3.shape
    m = batch * t_len
    n_a = w['attn_w_qkv'].shape[0]
    new_a = None
    new_hg, new_rw, new_sh, new_ffn = [], [], [], []
    if prompt:
        xa_bb, xa_tt = 1, 512
        ff_bb, ff_tt = 1, 512
        rw_bb, rw_tt = 1, 256
    else:
        xa_bb, xa_tt = 4, t_len
        ff_bb, ff_tt = batch, t_len
        rw_bb, rw_tt = batch, t_len
    for i in range(DEPTH):
        kind, j = i % N_MIXERS, i // N_MIXERS
        x2 = x3.reshape(m, d)
        g_mix = w['norm_mix'][i][None]
        if kind == 0:
            qkv = _norm_matmul(x2, g_mix, w['attn_w_qkv'][j], w['attn_head_gain'][j], hn_width=A_HEAD_DIM,
                               hn_tiles=(True,) * N_GROUPS + (True, False) * N_GROUPS, tn=A_WIDTH)
            if prompt:
                o = _attn_prompt(qkv, batch, t_len)
                new_a = tuple(_kv_cache_out(qkv, None if new_a is None else new_a[gi], gi, j, n_a, batch, t_len)
                              for gi in range(N_GROUPS))
            else:
                o, c0, c1, c2 = _attn_sample(qkv, a_bufs, new_a, j, batch, t_len)
                new_a = (c0, c1, c2)
            w_mix = w['attn_w_o'][j]
        elif kind == 1:
            proj = _norm_matmul(x2, g_mix, w['hg_w_in'][j])
            o, st = _hgrn(proj, w['hg_lb_logits'], w['hg_out_gain'][j][None], hg_s[j], i, batch, t_len)
            new_hg.append(st)
            w_mix = w['hg_w_o'][j]
        else:
            proj = _rwkv_proj(x3, w['norm_mix'][i][None], rw_shift[j][:, None, :], w['rw_mu'][j], w['rw_w_rkv'][j],
                              w['rw_w0'][j][None], w['rw_w1'][j], w['rw_w2'][j], w['rw_a0'][j][None],
                              w['rw_a1'][j], w['rw_a2'][j], w['rw_g1'][j], w['rw_g2'][j],
                              w['rw_k_k'][j][None], w['rw_k_a'][j][None], rw_bb, rw_tt)
            o, st = _rwkv_rec(proj[:7], w['rw_r_k'][j].reshape(1, d), w['rw_ln_g'][j][None], w['rw_ln_b'][j][None],
                              rw_s[j], batch, t_len)
            new_rw.append(st)
            new_sh.append(proj[7][:, 0, :])
            w_mix = w['rw_w_o'][j]
        x3 = _xattn(x3, o, w_mix, w['norm_mem'][i][None], w['xa_w_q'][i], w['xa_q_gain'][i][None], mem_kv, i,
                    w['xa_w_o'][i], xa_bb, xa_tt)
        x3, fb = _ffn(x3, w['norm_ffn'][i][None], w['ffn_w_in'][i], w['ffn_conv_w'][i], w['ffn_conv_b'][i][None],
                      w['ffn_w_down'][i], ffn_buf, i, ff_bb, ff_tt)
        new_ffn.append(fb)
    return (x3, new_a, jnp.stack(new_hg), jnp.stack(new_rw), jnp.stack(new_sh), jnp.stack(new_ffn))


def kernel(x_prompt, x_sample, mem_prompt, cache_attn_kv_w128, cache_attn_kv_w512, cache_attn_kv_w2048, state_hgrn, state_rwkv, state_rwkv_shift, state_ffn_conv, cache_mem_kv, norm_mix, norm_mem, norm_ffn, mem_norm, attn_w_qkv, attn_q_gain, attn_k_gain, attn_w_o, hg_w_in, hg_lb_logits, hg_out_gain, hg_w_o, rw_mu, rw_w_rkv, rw_w0, rw_w1, rw_w2, rw_a0, rw_a1, rw_a2, rw_g1, rw_g2, rw_k_k, rw_k_a, rw_r_k, rw_ln_g, rw_ln_b, rw_w_o, xa_w_q, xa_w_kv, xa_q_gain, xa_k_gain, xa_w_o, ffn_w_in, ffn_conv_w, ffn_conv_b, ffn_w_down):
    d = D_MODEL
    bp = x_prompt.shape[0]
    bs = x_sample.shape[0]
    n_a, n_b, n_c = attn_w_qkv.shape[0], hg_w_in.shape[0], rw_w_rkv.shape[0]
    bf = lambda a: a.astype(BF16)

    wq4 = attn_w_qkv.reshape(n_a, d, 3, N_GROUPS, A_WIDTH)
    w_qkv = jnp.concatenate([wq4[:, :, 0].reshape(n_a, d, N_GROUPS * A_WIDTH),
                             jnp.swapaxes(wq4[:, :, 1:3], 2, 3).reshape(n_a, d, N_GROUPS * 2 * A_WIDTH)], axis=-1)
    tile_h = lambda gn: jnp.broadcast_to(gn[:, :, None, :], (n_a, N_GROUPS, A_HEADS, A_HEAD_DIM))
    kgain = jnp.stack([tile_h(attn_k_gain), jnp.ones((n_a, N_GROUPS, A_HEADS, A_HEAD_DIM), F32)], axis=2)
    head_gain = jnp.concatenate([tile_h(attn_q_gain).reshape(n_a, 1, -1), kgain.reshape(n_a, 1, -1)], axis=-1)

    w = {
        'norm_mix': norm_mix, 'norm_mem': norm_mem, 'norm_ffn': norm_ffn,
        'attn_w_qkv': bf(w_qkv), 'attn_head_gain': head_gain, 'attn_w_o': bf(attn_w_o),
        'hg_w_in': bf(hg_w_in), 'hg_lb_logits': hg_lb_logits, 'hg_out_gain': hg_out_gain, 'hg_w_o': bf(hg_w_o),
        'rw_mu': rw_mu, 'rw_w_rkv': bf(rw_w_rkv), 'rw_w0': rw_w0, 'rw_w1': bf(rw_w1), 'rw_w2': bf(rw_w2),
        'rw_a0': rw_a0, 'rw_a1': bf(rw_a1), 'rw_a2': bf(rw_a2), 'rw_g1': bf(rw_g1), 'rw_g2': bf(rw_g2),
        'rw_k_k': rw_k_k, 'rw_k_a': rw_k_a, 'rw_r_k': rw_r_k, 'rw_ln_g': rw_ln_g, 'rw_ln_b': rw_ln_b,
        'rw_w_o': bf(rw_w_o),
        'xa_w_q': bf(xa_w_q), 'xa_q_gain': xa_q_gain, 'xa_w_o': bf(xa_w_o),
        'ffn_w_in': bf(ffn_w_in), 'ffn_conv_w': ffn_conv_w, 'ffn_conv_b': ffn_conv_b, 'ffn_w_down': bf(ffn_w_down),
    }

    mem_kv_prompt, mem_kv_flat = _memory_kv(mem_prompt.reshape(bp * N_MEM, d), mem_norm[:, None, :], bf(xa_w_kv),
                                            xa_k_gain[:, None, :], bp)

    as_rows = lambda cc: cc.reshape(cc.shape[0], cc.shape[1], cc.shape[2] * cc.shape[3] * cc.shape[4], cc.shape[5])
    as_cache = lambda rr: rr.reshape(rr.shape[0], rr.shape[1], rr.shape[2] // KV_ROWS, 2, A_HEADS, A_HEAD_DIM)
    y_prompt, a_p, hg_p, rw_p, sh_p, ffn_p = _trunk(
        x_prompt, mem_kv_flat, None,
        jnp.zeros((n_b, bp, HG_HEADS, HG_DK, HG_DK), F32),
        jnp.zeros((n_c, bp, RW_HEADS, RW_HEAD_DIM, RW_HEAD_DIM), F32),
        jnp.zeros((n_c, bp, d), F32),
        jnp.zeros((DEPTH, bp, CONV_W - 1, D_FF), F32), w, True)
    y_sample, a_s, hg_s, rw_s, sh_s, ffn_s = _trunk(
        x_sample, cache_mem_kv,
        tuple(as_rows(cc) for cc in (cache_attn_kv_w128, cache_attn_kv_w512, cache_attn_kv_w2048)),
        state_hgrn, state_rwkv, state_rwkv_shift, state_ffn_conv, w, False)
    a_p = tuple(as_cache(rr) for rr in a_p)
    a_s = tuple(as_cache(rr) for rr in a_s)
    return (y_prompt, y_sample, a_p[0], a_p[1], a_p[2], hg_p, rw_p, sh_p, ffn_p, mem_kv_prompt,
            a_s[0], a_s[1], a_s[2], hg_s, rw_s, sh_s, ffn_s)
```

```python
import functools
import math

import numpy as np
import jax
import jax.numpy as jnp
from jax import lax
from jax.experimental import pallas as pl
from jax.experimental.pallas import tpu as pltpu

F32 = jnp.float32
BF16 = jnp.bfloat16

D_MODEL = 1024
DEPTH = 4
N_MIXERS = 3
DIL_WINDOWS = (128, 512, 2048)
DIL_RATES = (1, 4, 16)
N_GROUPS = 3
A_SPAN = 128
A_HEADS = 4
A_HEAD_DIM = 128
A_BLOCK = 128
A_UNROLL = 4
A_WIDTH = A_HEADS * A_HEAD_DIM
KV_ROWS = 2 * A_HEADS
HG_HEADS = 8
HG_DK = 128
HG_CHUNK = 128
HG_SHORT_CHUNK = 16
RW_HEADS = 16
RW_HEAD_DIM = 64
RW_CHUNK = 64
RW_GN_EPS = 64e-5
N_MEM = 256
XA_HEADS = 4
XA_HEAD_DIM = D_MODEL // XA_HEADS
D_FF = 2816
CONV_W = 3
FF_GROUP = 4
RMS_EPS = 1e-6
LANES = 128
VMEM_LIMIT = 52 * 1024 * 1024

NT_DIMS = (((1,), (1,)), ((), ()))
TN_DIMS = (((0,), (0,)), ((), ()))


def _params(n_axes):
    return pltpu.CompilerParams(dimension_semantics=("arbitrary",) * n_axes, vmem_limit_bytes=VMEM_LIMIT)


def _rms(x):
    return x * lax.rsqrt(jnp.mean(x * x, axis=-1, keepdims=True) + RMS_EPS)


def _dot(a, b):
    return jnp.dot(a.astype(BF16), b.astype(BF16), preferred_element_type=F32)


def _dot_nt(a, b):
    return lax.dot_general(a.astype(BF16), b.astype(BF16), NT_DIMS, preferred_element_type=F32)


def _dot_tn(a, b):
    return lax.dot_general(a.astype(BF16), b.astype(BF16), TN_DIMS, preferred_element_type=F32)


def _split3(x):
    p1 = x.astype(BF16)
    r1 = x - p1.astype(F32)
    p2 = r1.astype(BF16)
    p3 = (r1 - p2.astype(F32)).astype(BF16)
    return jnp.concatenate([p1, p2, p3], axis=1)


def _exact_rowmix(mat_bf16, x):
    w = x.shape[1]
    y = jnp.dot(mat_bf16, _split3(x), preferred_element_type=F32)
    return y[:, :w] + y[:, w:2 * w] + y[:, 2 * w:]


def _softplus(x):
    return jnp.maximum(x, 0.0) + jnp.log1p(jnp.exp(-jnp.abs(x)))


def _sigmoid(x):
    return 1.0 / (1.0 + jnp.exp(-x))


def _silu(x):
    return x * _sigmoid(x)


def _nmm_body(x_ref, g_ref, w_ref, hg_ref, o_ref, *, hn_width, hn_tiles, tn):
    h = (_rms(x_ref[...]) * g_ref[...]).astype(BF16)
    for c in range(w_ref.shape[1] // tn):
        y = jnp.dot(h, w_ref[:, c * tn:(c + 1) * tn], preferred_element_type=F32)
        if hn_width is not None and hn_tiles[c]:
            for cc in range(tn // hn_width):
                sl = slice(c * tn + cc * hn_width, c * tn + (cc + 1) * hn_width)
                o_ref[:, sl] = _rms(y[:, cc * hn_width:(cc + 1) * hn_width]) * hg_ref[:, sl]
        else:
            o_ref[:, c * tn:(c + 1) * tn] = y


def _norm_matmul(x, gain, w, head_gain=None, hn_width=None, hn_tiles=None, tm=256, tn=512):
    m, d = x.shape
    n = w.shape[1]
    tm = min(tm, m)
    if head_gain is None:
        head_gain = jnp.ones((1, n), F32)
    return pl.pallas_call(
        functools.partial(_nmm_body, hn_width=hn_width, hn_tiles=hn_tiles, tn=tn),
        grid=(m // tm,),
        in_specs=[
            pl.BlockSpec((tm, d), lambda i: (i, 0)),
            pl.BlockSpec((1, d), lambda i: (0, 0)),
            pl.BlockSpec((d, n), lambda i: (0, 0)),
            pl.BlockSpec((1, n), lambda i: (0, 0)),
        ],
        out_specs=pl.BlockSpec((tm, n), lambda i: (i, 0)),
        out_shape=jax.ShapeDtypeStruct((m, n), F32),
        compiler_params=_params(1),
        name="norm_matmul",
    )(x, gain, w, head_gain)


def _ds(start, size, stride):
    return pl.ds(start, size) if stride == 1 else pl.ds(start, size, stride=stride)


def _attn_prompt_body(q0, q1, q2, k0, v0, k1, v1, k2, v2, o_ref, acc, m_s, l_s, *, seq):
    scale = A_HEAD_DIM ** -0.5
    blk = A_BLOCK
    for g, (q_ref, k_ref, v_ref) in enumerate(((q0, k0, v0), (q1, k1, v1), (q2, k2, v2))):
        dil = DIL_RATES[g]
        n = seq // dil
        nb = n // blk
        width = 2 * blk if nb > 1 else blk
        ii = lax.broadcasted_iota(jnp.int32, (blk, width), 0)
        jj = lax.broadcasted_iota(jnp.int32, (blk, width), 1)
        if nb > 1:
            in_cur = (jj >= blk) & (jj - blk <= ii)
            in_prev = (jj < blk) & (jj >= ii)
        else:
            in_cur, in_prev = jj <= ii, None

        def body(it, carry, g=g, dil=dil, nb=nb, q_ref=q_ref, k_ref=k_ref, v_ref=v_ref, in_cur=in_cur,
                 in_prev=in_prev):
            rows, ss, vws = [], [], []
            for uu in range(A_UNROLL):
                idx = it * A_UNROLL + uu
                r = idx // nb
                mb = idx % nb
                rw = _ds(r + dil * blk * mb, blk, dil)
                kc = k_ref[rw, :].astype(BF16)
                vc = v_ref[rw, :].astype(BF16)
                if nb > 1:
                    prev = _ds(r + dil * blk * jnp.maximum(mb - 1, 0), blk, dil)
                    kw = jnp.concatenate([k_ref[prev, :].astype(BF16), kc], axis=0)
                    vw = jnp.concatenate([v_ref[prev, :].astype(BF16), vc], axis=0)
                    prev_bias = jnp.where(mb > 0, 0.0, -jnp.inf)
                    bias = jnp.where(in_cur, 0.0, jnp.where(in_prev, prev_bias, -jnp.inf))
                else:
                    kw, vw = kc, vc
                    bias = jnp.where(in_cur, 0.0, -jnp.inf)
                rows.append(rw)
                vws.append(vw)
                ss.append(_dot_nt(q_ref[rw, :], kw) * scale + bias)
            ms = [jnp.max(s, axis=-1, keepdims=True) for s in ss]
            ps = [jnp.exp(s - m) for s, m in zip(ss, ms)]
            ls = [jnp.sum(p, axis=-1, keepdims=True) for p in ps]
            os_ = [_dot(p, vw) for p, vw in zip(ps, vws)]
            m_bs = [jnp.broadcast_to(m, (blk, LANES)) for m in ms]
            l_bs = [jnp.broadcast_to(l, (blk, LANES)) for l in ls]
            if g == 0:
                for rw, o, m_b, l_b in zip(rows, os_, m_bs, l_bs):
                    acc[rw, :] = o
                    m_s[rw, :] = m_b
                    l_s[rw, :] = l_b
            else:
                olds = [(m_s[rw, :], l_s[rw, :], acc[rw, :]) for rw in rows]
                for rw, o, m_b, l_b, (m_old, l_old, acc_old) in zip(rows, os_, m_bs, l_bs, olds):
                    m_new = jnp.maximum(m_old, m_b)
                    a_old = jnp.exp(m_old - m_new)
                    a_cur = jnp.exp(m_b - m_new)
                    acc[rw, :] = acc_old * a_old + o * a_cur
                    l_s[rw, :] = l_old * a_old + l_b * a_cur
                    m_s[rw, :] = m_new
            return carry

        assert (dil * nb) % A_UNROLL == 0
        lax.fori_loop(0, dil * nb // A_UNROLL, body, 0)
    o_ref[...] = acc[...] / l_s[...]


def _attn_prompt(qkv, batch, seq):
    m = batch * seq
    assert seq % (A_BLOCK * max(DIL_RATES)) == 0 and A_BLOCK == A_SPAN
    nq = N_GROUPS * A_HEADS

    def spec(col_fn):
        return pl.BlockSpec((seq, A_HEAD_DIM), lambda b, h: (b, col_fn(h)))

    in_specs = [spec(lambda h, g=g: g * A_HEADS + h) for g in range(N_GROUPS)]
    for g in range(N_GROUPS):
        in_specs.append(spec(lambda h, g=g: nq + g * A_HEADS + h))
        in_specs.append(spec(lambda h, g=g: 2 * nq + g * A_HEADS + h))
    return pl.pallas_call(
        functools.partial(_attn_prompt_body, seq=seq),
        grid=(batch, A_HEADS),
        in_specs=in_specs,
        out_specs=pl.BlockSpec((seq, A_HEAD_DIM), lambda b, h: (b, h)),
        out_shape=jax.ShapeDtypeStruct((m, A_WIDTH), F32),
        scratch_shapes=[pltpu.VMEM((seq, A_HEAD_DIM), F32), pltpu.VMEM((seq, LANES), F32),
                        pltpu.VMEM((seq, LANES), F32)],
        compiler_params=_params(2),
        name="dilated_attn_prompt",
    )(*([qkv] * 9))


def _attn_sample_body(*refs, t_new, n_alias):
    qkv_ref, bufs = refs[0], refs[1:4]
    o_ref, outs = refs[4 + n_alias], refs[5 + n_alias:8 + n_alias]
    scale = A_HEAD_DIM ** -0.5
    pad = jnp.zeros((LANES - t_new, A_HEAD_DIM), F32)
    nq = N_GROUPS * A_WIDTH

    def new_rows(g, kv, h):
        base = (1 + kv) * nq + g * A_WIDTH + h * A_HEAD_DIM
        return qkv_ref[:, base:base + A_HEAD_DIM]

    for g in range(N_GROUPS):
        length = bufs[g].shape[0] // KV_ROWS
        keep = (length - t_new) * KV_ROWS
        outs[g][0:keep, :] = bufs[g][t_new * KV_ROWS:length * KV_ROWS, :]
        for kv in range(2):
            for h in range(A_HEADS):
                outs[g][pl.ds(keep + kv * A_HEADS + h, t_new, stride=KV_ROWS), :] = new_rows(g, kv, h)

    for h in range(A_HEADS):
        parts, vals = [], []
        for g in range(N_GROUPS):
            dil = DIL_RATES[g]
            length = bufs[g].shape[0] // KV_ROWS
            q = qkv_ref[:, g * A_WIDTH + h * A_HEAD_DIM: g * A_WIDTH + (h + 1) * A_HEAD_DIM]
            s_buf = _dot_nt(q, bufs[g][pl.ds(h, length, stride=KV_ROWS), :]) * scale
            s_new = _dot_nt(q, jnp.concatenate([new_rows(g, 0, h), pad], axis=0)) * scale
            for s, base in ((s_buf, 0), (s_new, length)):
                ii = lax.broadcasted_iota(jnp.int32, s.shape, 0)
                jj = lax.broadcasted_iota(jnp.int32, s.shape, 1) + base
                dd = jj - ii
                valid = (dd >= 0) & (dd <= A_SPAN * dil) & ((dd & (dil - 1)) == 0) & (jj < length + t_new)
                parts.append(jnp.where(valid, s, -jnp.inf))
            vals.append(bufs[g][pl.ds(A_HEADS + h, length, stride=KV_ROWS), :])
            vals.append(jnp.concatenate([new_rows(g, 1, h), pad], axis=0))
        mx = functools.reduce(jnp.maximum, [jnp.max(s, axis=-1, keepdims=True) for s in parts])
        exps = [jnp.exp(s - mx) for s in parts]
        den = functools.reduce(lambda a, b: a + b, [jnp.sum(e, axis=-1, keepdims=True) for e in exps])
        o = functools.reduce(lambda a, b: a + b, [_dot(e / den, vv) for e, vv in zip(exps, vals)])
        o_ref[:, h * A_HEAD_DIM:(h + 1) * A_HEAD_DIM] = o


def _attn_sample(qkv, caches, prev_out, j, batch, t_new):
    assert t_new % 8 == 0 and t_new <= LANES
    lens = [cch.shape[2] // KV_ROWS for cch in caches]
    for g in range(N_GROUPS):
        assert lens[g] == A_SPAN * DIL_RATES[g] and (DIL_RATES[g] & (DIL_RATES[g] - 1)) == 0

    def cache_spec(g):
        return pl.BlockSpec((None, None, lens[g] * KV_ROWS, A_HEAD_DIM), lambda b: (j, b, 0, 0))

    in_specs = [pl.BlockSpec((t_new, qkv.shape[1]), lambda b: (b, 0))] + [cache_spec(g) for g in range(N_GROUPS)]
    n_alias = 0 if prev_out is None else N_GROUPS
    aliases = {}
    if prev_out is not None:
        in_specs += [pl.BlockSpec(memory_space=pl.ANY)] * N_GROUPS
        aliases = {4 + g: 1 + g for g in range(N_GROUPS)}
    out_specs = [pl.BlockSpec((t_new, A_WIDTH), lambda b: (b, 0))] + [cache_spec(g) for g in range(N_GROUPS)]
    out_shape = [jax.ShapeDtypeStruct((batch * t_new, A_WIDTH), F32)]
    out_shape += [jax.ShapeDtypeStruct(cch.shape, F32) for cch in caches]
    return pl.pallas_call(
        functools.partial(_attn_sample_body, t_new=t_new, n_alias=n_alias),
        grid=(batch,),
        in_specs=in_specs,
        out_specs=out_specs,
        out_shape=out_shape,
        input_output_aliases=aliases,
        compiler_params=_params(1),
        name="dilated_attn_sample",
    )(qkv, *caches, *(prev_out or ()))


def _kv_cache_body(*refs):
    k_ref, v_ref, o_ref = refs[0], refs[1], refs[-1]
    tr = k_ref.shape[0]
    for kv, x_ref in enumerate((k_ref, v_ref)):
        for h in range(A_HEADS):
            o_ref[pl.ds(kv * A_HEADS + h, tr, stride=KV_ROWS), :] = x_ref[:, h * A_HEAD_DIM:(h + 1) * A_HEAD_DIM]


def _kv_cache_out(qkv, prev_out, g, j, n_layers, batch, seq):
    win = min(DIL_WINDOWS[g], seq)
    tr = min(win, 512)
    first = (seq - win) // tr
    row = lambda b, t: b * (seq // tr) + first + t
    in_specs = [pl.BlockSpec((tr, A_WIDTH), lambda b, t: (row(b, t), N_GROUPS + g)),
                pl.BlockSpec((tr, A_WIDTH), lambda b, t: (row(b, t), 2 * N_GROUPS + g))]
    aliases = {}
    if prev_out is not None:
        in_specs.append(pl.BlockSpec(memory_space=pl.ANY))
        aliases = {2: 0}
    return pl.pallas_call(
        _kv_cache_body,
        grid=(batch, win // tr),
        in_specs=in_specs,
        out_specs=pl.BlockSpec((None, None, tr * KV_ROWS, A_HEAD_DIM), lambda b, t: (j, b, t, 0)),
        out_shape=jax.ShapeDtypeStruct((n_layers, batch, win * KV_ROWS, A_HEAD_DIM), F32),
        input_output_aliases=aliases,
        compiler_params=_params(2),
        name="kv_cache_out",
    )(qkv, qkv, *(() if prev_out is None else (prev_out,)))


def _hgrn_consts(c):
    t = np.arange(c)[:, None]
    u = np.arange(c)[None, :]
    masks = [(u == t)]
    s = 1
    while s < c:
        upper = (t // s) % 2 == 1
        lower_u = (u // s) % 2 == 0
        masks.append(upper & lower_u & (t // (2 * s) == u // (2 * s)))
        s *= 2
    masks = np.stack([m.astype(np.float32) for m in masks], axis=0)
    return jnp.asarray((u <= t).astype(np.float32), BF16), jnp.asarray(masks, F32)


def _hgrn_level_exponent(b, s, row):
    c, width = b.shape
    if s >= 8:
        mid = jnp.concatenate([jnp.broadcast_to(b[j * 2 * s + s - 1:j * 2 * s + s, :], (2 * s, width))
                               for j in range(c // (2 * s))], axis=0)
    else:
        b3 = b.reshape(c // 8, 8, width)
        sub = lax.broadcasted_iota(jnp.int32, b3.shape, 1)
        pick = lambda k: jnp.broadcast_to(b3[:, k:k + 1, :], b3.shape)
        mid3 = pick(s - 1)
        for k in range(2 * s + s - 1, 8, 2 * s):
            mid3 = jnp.where(sub >= k - (s - 1), pick(k), mid3)
        mid = mid3.reshape(c, width)
    return jnp.where((row & s) != 0, b - mid, mid - b)


def _hgrn_body(q_ref, z_ref, i_ref, g_ref, lbl_ref, og_ref, s0_ref, mat_ref, msk_ref, o_ref, st_ref, st_scr,
               *, layer, t_len, c, hp):
    n_lvl = msk_ref.shape[0] - 1
    width = hp * HG_DK
    hr = range(hp)
    lane = lambda x, hd: x[:, hd * HG_DK:(hd + 1) * HG_DK]
    lg = lbl_ref[...]
    e = jnp.exp(lg - jnp.max(lg, axis=0, keepdims=True))
    prob = e / jnp.sum(e, axis=0, keepdims=True)
    lb = jnp.sum(prob[0:layer + 1], axis=0, keepdims=True) - prob[0:1]
    log_lb = jnp.log(lb)
    log_1m = jnp.log1p(-lb)
    @pl.when(pl.program_id(2) == 0)
    def _():
        for hd in hr:
            st_scr[hd] = s0_ref[hd].T

    row = lax.broadcasted_iota(jnp.int32, (c, width), 0)
    out_gain = jnp.concatenate([og_ref[...]] * hp, axis=1)

    def chunk(ci, carry):
        if t_len >= c:
            rows = pl.ds(pl.multiple_of(ci * c, c), c)
            ld = lambda ref: ref[rows, :]
            live = None
        else:
            padz = jnp.zeros((c - t_len, width), F32)
            ld = lambda ref: jnp.concatenate([ref[...], padz], axis=0)
            live = row < t_len
        z = ld(z_ref)
        lsig = jnp.minimum(z, 0.0) - jnp.log1p(jnp.exp(-jnp.abs(z)))
        a1 = jnp.broadcast_to(log_lb, z.shape)
        a2 = log_1m + lsig
        hi = jnp.maximum(a1, a2)
        lo = jnp.minimum(a1, a2)
        log_f = hi + jnp.log1p(jnp.exp(lo - hi))
        kg = (1.0 - lb) * _sigmoid(-z)
        if live is not None:
            log_f = jnp.where(live, log_f, 0.0)
            kg = jnp.where(live, kg, 0.0)
        qa = _silu(ld(q_ref))
        vv = ld(i_ref)
        b = _exact_rowmix(mat_ref[...], log_f)
        sts = [st_scr[hd] for hd in hr]
        a = [_dot_nt(lane(qa, hd), lane(kg, hd)) * msk_ref[0] for hd in hr]
        for lv in range(n_lvl):
            wv = jnp.exp(_hgrn_level_exponent(b, 1 << lv, row))
            qw = qa * wv
            kw = kg * wv
            a = [a[hd] + _dot_nt(lane(qw, hd), lane(kw, hd)) * msk_ref[1 + lv] for hd in hr]
        qb = qa * jnp.exp(b)
        kd = kg * jnp.exp(b[c - 1:c] - b)
        dec = jnp.exp(b[c - 1:c])
        o = [_dot_nt(lane(qb, hd), sts[hd]) + _dot(a[hd], lane(vv, hd)) for hd in hr]
        st_scr[...] = jnp.stack([sts[hd] * lane(dec, hd) + _dot_tn(lane(vv, hd), lane(kd, hd)) for hd in hr], axis=0)
        y = jnp.concatenate([_rms(x) for x in o], axis=1) * out_gain * _silu(ld(g_ref))
        if t_len >= c:
            o_ref[rows, :] = y
        else:
            o_ref[...] = y[0:t_len]
        return carry

    lax.fori_loop(0, max(t_len // c, 1), chunk, 0)
    for hd in hr:
        st_ref[hd] = st_scr[hd].T


def _hgrn(proj, lb_logits, out_gain, s0, layer, batch, t_len, tt=512):
    tt = min(tt, t_len)
    if tt >= HG_CHUNK:
        c, hp = HG_CHUNK, 4
        assert tt % c == 0 and t_len % tt == 0
    else:
        c, hp = HG_SHORT_CHUNK, HG_HEADS
        assert tt % 8 == 0 and tt <= c
    mats, masks = _hgrn_consts(c)
    width = hp * HG_DK
    ng = HG_HEADS // hp
    nt = t_len // tt

    def col(part):
        return pl.BlockSpec((tt, width), lambda b, h, t: (b * nt + t, part * ng + h))

    st_spec = pl.BlockSpec((None, hp, HG_DK, HG_DK), lambda b, h, t: (b, h, 0, 0))
    return pl.pallas_call(
        functools.partial(_hgrn_body, layer=layer, t_len=tt, c=c, hp=hp),
        grid=(batch, ng, nt),
        in_specs=[col(0), col(1), col(2), col(3),
                  pl.BlockSpec((DEPTH, width), lambda b, h, t: (0, h)),
                  pl.BlockSpec((1, HG_DK), lambda b, h, t: (0, 0)),
                  st_spec,
                  pl.BlockSpec(mats.shape, lambda b, h, t: (0, 0)),
                  pl.BlockSpec(masks.shape, lambda b, h, t: (0, 0, 0))],
        out_specs=[pl.BlockSpec((tt, width), lambda b, h, t: (b * nt + t, h)), st_spec],
        out_shape=[jax.ShapeDtypeStruct((batch * t_len, HG_HEADS * HG_DK), F32),
                   jax.ShapeDtypeStruct((batch, HG_HEADS, HG_DK, HG_DK), F32)],
        scratch_shapes=[pltpu.VMEM((hp, HG_DK, HG_DK), F32)],
        compiler_params=_params(3),
        name="hgrn2_chunked",
    )(proj, proj, proj, proj, lb_logits, out_gain, s0, mats, masks)


def _rwkv_proj_body(x_ref, g_ref, sh_ref, mu_ref, wrkv_ref, w0_ref, w1_ref, w2_ref, a0_ref, a1_ref, a2_ref,
                    g1_ref, g2_ref, kk_ref, ka_ref,
                    r_o, k_o, v_o, ld_o, kk_o, a_o, g_o, sh_o, carry, *, bb, tt):
    t = pl.program_id(1)
    d = D_MODEL

    @pl.when(t == 0)
    def _():
        carry[...] = sh_ref[...]

    h3 = _rms(x_ref[...]) * g_ref[...]
    h = h3.reshape(bb * tt, d)
    rolled = pltpu.roll(h, 1, 0).reshape(bb, tt, d)
    tpos = lax.broadcasted_iota(jnp.int32, (bb, tt, d), 1)
    prev = jnp.where(tpos == 0, carry[...], rolled).reshape(bb * tt, d)
    last = h3[:, tt - 1:tt, :]
    carry[...] = last
    sh_o[...] = last
    dx = prev - h
    mix = lambda jm: h + dx * mu_ref[jm:jm + 1, :]
    r = _dot(mix(0), wrkv_ref[0])
    k = _dot(mix(1), wrkv_ref[1])
    v = _dot(mix(2), wrkv_ref[2])
    wl = w0_ref[...] + _dot(jnp.tanh(_dot(mix(3), w1_ref[...])), w2_ref[...])
    wlog = -_softplus(-wl) - 0.5
    a = _sigmoid(a0_ref[...] + _dot(_dot(mix(4), a1_ref[...]), a2_ref[...]))
    gate = _dot(_sigmoid(_dot(mix(5), g1_ref[...])), g2_ref[...])
    r_o[...] = r
    k_o[...] = k * (1.0 + (a - 1.0) * ka_ref[...])
    v_o[...] = v
    ld_o[...] = -jnp.exp(wlog)
    kk_o[...] = k * kk_ref[...]
    a_o[...] = a
    g_o[...] = gate


def _rwkv_proj(x3, gain, shift, mu, wrkv, w0, w1, w2, a0, a1, a2, g1, g2, k_k, k_a, bb, tt):
    batch, t_len, d = x3.shape
    m = batch * t_len
    rows = bb * tt
    full = lambda arr: pl.BlockSpec(arr.shape, lambda b, t: (0,) * arr.ndim)
    row_spec = pl.BlockSpec((rows, d), lambda b, t: (b * (t_len // tt) + t, 0))
    outs = pl.pallas_call(
        functools.partial(_rwkv_proj_body, bb=bb, tt=tt),
        grid=(batch // bb, t_len // tt),
        in_specs=[pl.BlockSpec((bb, tt, d), lambda b, t: (b, t, 0)), full(gain),
                  pl.BlockSpec((bb, 1, d), lambda b, t: (b, 0, 0)), full(mu), full(wrkv),
                  full(w0), full(w1), full(w2), full(a0), full(a1), full(a2), full(g1), full(g2),
                  full(k_k), full(k_a)],
        out_specs=[row_spec] * 7 + [pl.BlockSpec((bb, 1, d), lambda b, t: (b, 0, 0))],
        out_shape=[jax.ShapeDtypeStruct((m, d), F32)] * 7 + [jax.ShapeDtypeStruct((batch, 1, d), F32)],
        scratch_shapes=[pltpu.VMEM((bb, 1, d), F32)],
        compiler_params=_params(2),
        name="rwkv_proj",
    )(x3, gain, shift, mu, wrkv, w0, w1, w2, a0, a1, a2, g1, g2, k_k, k_a)
    return outs


def _rwkv_consts(c):
    n = RW_HEAD_DIM
    t = np.arange(c)[:, None]
    u = np.arange(c)[None, :]
    f = lambda mat, dt: jnp.asarray(np.asarray(mat).astype(np.float32), dt)
    tri = f(u <= t, BF16)
    half = np.concatenate([u < t, u <= t], axis=0)
    lo = f(np.concatenate([half, half], axis=1), F32)
    eye = f(u == t, F32)
    lane = np.arange(LANES)
    bdiag = f((lane[:, None] // n) == (lane[None, :] // n), F32)
    return tri, lo, eye, bdiag


def _rwkv_rec_body(r_ref, k_ref, v_ref, ld_ref, kk_ref, a_ref, g_ref, rk_ref, lng_ref, lnb_ref, s0_ref,
                   tri_ref, lo_ref, eye_ref, bd_ref, o_ref, st_ref, st_scr, *, tt):
    c = RW_CHUNK
    n = RW_HEAD_DIM
    heads = RW_HEADS
    pairs = heads // 2
    w = D_MODEL
    t = pl.program_id(1)

    @pl.when(t == 0)
    def _():
        zero = jnp.zeros((n, n), F32)
        for p in range(pairs):
            top = jnp.concatenate([s0_ref[2 * p], zero], axis=1)
            bot = jnp.concatenate([zero, s0_ref[2 * p + 1]], axis=1)
            st_scr[p] = jnp.concatenate([top, bot], axis=0)

    low_2c = lax.broadcasted_iota(jnp.int32, (2 * c, LANES), 1) < n
    low_c = lax.broadcasted_iota(jnp.int32, (c, LANES), 1) < n
    tile = lambda x, p: x[:, p * LANES:(p + 1) * LANES]
    hr = range(heads)
    pr = range(pairs)
    zeros_c = jnp.zeros((c, LANES), F32)

    def head_sums(x):
        out = []
        for p in pr:
            xt = tile(x, p)
            lo_s = jnp.sum(jnp.where(low_c, xt, 0.0), axis=-1, keepdims=True)
            hi_s = jnp.sum(jnp.where(low_c, 0.0, xt), axis=-1, keepdims=True)
            out.append(jnp.where(low_c, lo_s, hi_s))
        return jnp.concatenate(out, axis=1)

    def chunk(ci, carry):
        if tt >= c:
            rows = pl.ds(pl.multiple_of(ci * c, c), c)
            ld = lambda ref: ref[rows, :]
        else:
            padz = jnp.zeros((c - tt, w), F32)
            ld = lambda ref: jnp.concatenate([ref[...], padz], axis=0)
        r, kf, v, lc, kk, a, gate = (ld(ref) for ref in (r_ref, k_ref, v_ref, ld_ref, kk_ref, a_ref, g_ref))
        kk = kk / jnp.maximum(jnp.sqrt(head_sums(kk * kk)), 1e-12)
        bvec = kk * a
        cum = _exact_rowmix(tri_ref[...], lc)
        c_end = cum[c - 1:c]
        e_neg = jnp.exp(-cum)
        e_end = jnp.exp(c_end - cum)
        d_end = jnp.exp(c_end)
        kr = jnp.concatenate([kk * jnp.exp(cum - lc), r * jnp.exp(cum)], axis=0)
        bx = jnp.concatenate([bvec * e_neg, kf * e_neg], axis=0)
        bk = jnp.concatenate([bvec * e_end, kf * e_end], axis=0)
        lo = lo_ref[...]
        eye = eye_ref[...]
        krm = [jnp.where(low_2c if h % 2 == 0 else jnp.logical_not(low_2c), tile(kr, h // 2), 0.0) for h in hr]
        gm = [_dot_nt(krm[h], tile(bx, h // 2)) * lo for h in hr]
        nn = [x[0:c, 0:c] for x in gm]
        tm = [eye - x for x in nn]
        npow = [_dot(x, x) for x in nn]
        p2 = 2
        while p2 < c:
            tm = [x + _dot(x, y) for x, y in zip(tm, npow)]
            p2 *= 2
            if p2 < c:
                npow = [_dot(x, x) for x in npow]
        sts = [st_scr[p] for p in pr]
        ks = [_dot_nt(tile(kr, p), sts[p]) for p in pr]
        vz = [jnp.concatenate([zeros_c, tile(v, p)], axis=0) for p in pr]
        av = [jnp.where(low_2c, _dot(gm[2 * p], vz[p]), _dot(gm[2 * p + 1], vz[p])) for p in pr]
        rhs = [ks[p][0:c] + av[p][0:c] for p in pr]
        u = [jnp.where(low_c, -_dot(tm[2 * p], rhs[p]), -_dot(tm[2 * p + 1], rhs[p])) for p in pr]
        uz = [jnp.concatenate([u[p], zeros_c], axis=0) for p in pr]
        y = [ks[p][c:2 * c] + av[p][c:2 * c]
             + jnp.where(low_c, _dot(gm[2 * p][c:2 * c], uz[p]), _dot(gm[2 * p + 1][c:2 * c], uz[p])) for p in pr]
        new_st = [sts[p] * tile(d_end, p)
                  + _dot_tn(jnp.concatenate([u[p], tile(v, p)], axis=0), tile(bk, p)) * bd_ref[...] for p in pr]
        st_scr[...] = jnp.stack(new_st, axis=0)
        yy = jnp.concatenate(y, axis=1)
        yc = yy - head_sums(yy) * (1.0 / n)
        var = head_sums(yc * yc) * (1.0 / n)
        yn = yc * lax.rsqrt(var + RW_GN_EPS) * lng_ref[...] + lnb_ref[...]
        z = (yn + head_sums(r * kf * rk_ref[...]) * v) * gate
        if tt >= c:
            o_ref[rows, :] = z
        else:
            o_ref[...] = z[0:tt]
        return carry

    lax.fori_loop(0, max(tt // c, 1), chunk, 0)
    for p in range(pairs):
        blk = st_scr[p]
        st_ref[2 * p] = blk[0:n, 0:n]
        st_ref[2 * p + 1] = blk[n:2 * n, n:2 * n]


def _rwkv_rec(proj, r_k, ln_g, ln_b, s0, batch, t_len, tt=512):
    r, kf, v, ld, kk, a, gate = proj
    tt = min(tt, t_len)
    assert tt % RW_CHUNK == 0 or (tt < RW_CHUNK and tt % 8 == 0)
    assert RW_HEADS * RW_HEAD_DIM == D_MODEL and 2 * RW_HEAD_DIM == LANES and 2 * RW_CHUNK == LANES
    consts = _rwkv_consts(RW_CHUNK)
    nt = t_len // tt
    row_spec = pl.BlockSpec((tt, D_MODEL), lambda b, t: (b * nt + t, 0))
    par_spec = pl.BlockSpec((1, D_MODEL), lambda b, t: (0, 0))
    st_spec = pl.BlockSpec((None, RW_HEADS, RW_HEAD_DIM, RW_HEAD_DIM), lambda b, t: (b, 0, 0, 0))
    const = lambda arr: pl.BlockSpec(arr.shape, lambda b, t: (0, 0))
    return pl.pallas_call(
        functools.partial(_rwkv_rec_body, tt=tt),
        grid=(batch, nt),
        in_specs=[row_spec] * 7 + [par_spec] * 3 + [st_spec] + [const(x) for x in consts],
        out_specs=[row_spec, st_spec],
        out_shape=[jax.ShapeDtypeStruct((batch * t_len, D_MODEL), F32),
                   jax.ShapeDtypeStruct((batch, RW_HEADS, RW_HEAD_DIM, RW_HEAD_DIM), F32)],
        scratch_shapes=[pltpu.VMEM((RW_HEADS // 2, LANES, LANES), F32)],
        compiler_params=_params(2),
        name="rwkv7_chunked",
    )(r, kf, v, ld, kk, a, gate, r_k, ln_g, ln_b, s0, *consts)


def _xattn_body(x_ref, a_ref, wm_ref, g_ref, wq_ref, qg_ref, kv_ref, wo_ref, o_ref, *, bb, tt, native):
    d = D_MODEL
    e = XA_HEAD_DIM
    scale = XA_HEAD_DIM ** -0.5
    x = x_ref[...].reshape(bb * tt, d) + jnp.dot(a_ref[...].astype(BF16), wm_ref[...], preferred_element_type=F32)
    q = jnp.dot((_rms(x) * g_ref[...]).astype(BF16), wq_ref[...], preferred_element_type=F32)
    qn = lambda b, h: _rms(q[b * tt:(b + 1) * tt, h * e:(h + 1) * e]) * qg_ref[...]
    per_batch = []
    for b in range(bb):
        if native:
            rows = N_MEM * XA_HEADS
            s = _dot_nt(jnp.concatenate([qn(b, h) for h in range(XA_HEADS)], axis=0),
                        kv_ref[b, :, 0].reshape(rows, e)) * scale
            own = (lax.broadcasted_iota(jnp.int32, s.shape, 0) // tt
                   == lax.broadcasted_iota(jnp.int32, s.shape, 1) % XA_HEADS)
            s = jnp.where(own, s, -jnp.inf)
            p = jnp.exp(s - jnp.max(s, axis=-1, keepdims=True))
            p = p / jnp.sum(p, axis=-1, keepdims=True)
            oh = _dot(p, kv_ref[b, :, 1].reshape(rows, e))
            per_batch.append(jnp.concatenate([oh[h * tt:(h + 1) * tt] for h in range(XA_HEADS)], axis=-1))
            continue
        heads = []
        for h in range(XA_HEADS):
            s = _dot_nt(qn(b, h), kv_ref[b, :, h * e:(h + 1) * e]) * scale
            p = jnp.exp(s - jnp.max(s, axis=-1, keepdims=True))
            p = p / jnp.sum(p, axis=-1, keepdims=True)
            heads.append(_dot(p, kv_ref[b, :, d + h * e:d + (h + 1) * e]))
        per_batch.append(jnp.concatenate(heads, axis=-1))
    o = jnp.concatenate(per_batch, axis=0) if bb > 1 else per_batch[0]
    y = x + jnp.dot(o.astype(BF16), wo_ref[...], preferred_element_type=F32)
    o_ref[...] = y.reshape(bb, tt, d)


def _xattn(x3, mix_out, w_mix, gain, wq, q_gain, kv_all, layer, wo, bb, tt):
    batch, t_len, d = x3.shape
    km = mix_out.shape[1]
    nt = t_len // tt
    assert bb == 1 or nt == 1
    native = kv_all.ndim == 6
    if native:
        kv_spec = pl.BlockSpec((None, bb, N_MEM, 2, XA_HEADS, XA_HEAD_DIM), lambda b, t: (layer, b, 0, 0, 0, 0))
    else:
        kv_spec = pl.BlockSpec((None, bb, N_MEM, 2 * d), lambda b, t: (layer, b, 0, 0))
    return pl.pallas_call(
        functools.partial(_xattn_body, bb=bb, tt=tt, native=native),
        grid=(batch // bb, t_len // tt),
        in_specs=[pl.BlockSpec((bb, tt, d), lambda b, t: (b, t, 0)),
                  pl.BlockSpec((bb * tt, km), lambda b, t: (b * nt + t, 0)),
                  pl.BlockSpec((km, d), lambda b, t: (0, 0)),
                  pl.BlockSpec((1, d), lambda b, t: (0, 0)),
                  pl.BlockSpec((d, d), lambda b, t: (0, 0)),
                  pl.BlockSpec((1, XA_HEAD_DIM), lambda b, t: (0, 0)),
                  kv_spec,
                  pl.BlockSpec((d, d), lambda b, t: (0, 0))],
        out_specs=pl.BlockSpec((bb, tt, d), lambda b, t: (b, t, 0)),
        out_shape=jax.ShapeDtypeStruct((batch, t_len, d), F32),
        compiler_params=_params(2),
        name="memory_xattn",
    )(x3, mix_out, w_mix, gain, wq, q_gain, kv_all, wo)


def _memkv_body(x_ref, g_ref, w_ref, kg_ref, o_ref, flat_ref):
    c = pl.program_id(1)
    e = XA_HEAD_DIM
    h = (_rms(x_ref[...]) * g_ref[...]).astype(BF16)
    y = jnp.dot(h, w_ref[...], preferred_element_type=F32)

    @pl.when(c == 0)
    def _():
        for h in range(XA_HEADS):
            kn = _rms(y[:, h * e:(h + 1) * e]) * kg_ref[...]
            o_ref[:, h, :] = kn
            flat_ref[:, h * e:(h + 1) * e] = kn

    @pl.when(c == 1)
    def _():
        flat_ref[...] = y
        for h in range(XA_HEADS):
            o_ref[:, h, :] = y[:, h * e:(h + 1) * e]


def _memory_kv(mem2, mem_norm, w_kv, k_gain, batch):
    d = D_MODEL
    return pl.pallas_call(
        _memkv_body,
        grid=(DEPTH, 2, batch),
        in_specs=[pl.BlockSpec((N_MEM, d), lambda l, c, b: (b, 0)),
                  pl.BlockSpec((None, 1, d), lambda l, c, b: (l, 0, 0)),
                  pl.BlockSpec((None, d, d), lambda l, c, b: (l, 0, c)),
                  pl.BlockSpec((None, 1, XA_HEAD_DIM), lambda l, c, b: (l, 0, 0))],
        out_specs=[pl.BlockSpec((None, None, N_MEM, None, XA_HEADS, XA_HEAD_DIM), lambda l, c, b: (l, b, 0, c, 0, 0)),
                   pl.BlockSpec((None, None, N_MEM, d), lambda l, c, b: (l, b, 0, c))],
        out_shape=[jax.ShapeDtypeStruct((DEPTH, batch, N_MEM, 2, XA_HEADS, XA_HEAD_DIM), F32),
                   jax.ShapeDtypeStruct((DEPTH, batch, N_MEM, 2 * d), F32)],
        compiler_params=_params(3),
        name="memory_kv",
    )(mem2, mem_norm, w_kv, k_gain)


def _ffn_body(x_ref, g_ref, win_ref, cw_ref, cb_ref, wd_ref, pin_ref, o_ref, st_ref, acc, carry, *, bb, tt, tf):
    t = pl.program_id(1)
    d = D_MODEL

    @pl.when(t == 0)
    def _():
        carry[...] = pin_ref[...]

    x = x_ref[...].reshape(bb * tt, d)
    h = (_rms(x) * g_ref[...]).astype(BF16)
    acc[...] = x
    tpos = lax.broadcasted_iota(jnp.int32, (bb, tt, tf), 1)
    tails, acts = [], []
    nf = D_FF // tf
    for f in range(nf):
        fs = slice(f * tf, (f + 1) * tf)
        u = jnp.dot(h, win_ref[:, fs], preferred_element_type=F32)
        gate = jnp.dot(h, win_ref[:, D_FF + f * tf:D_FF + (f + 1) * tf], preferred_element_type=F32)
        prev = carry[:, :, fs]
        u3 = u.reshape(bb, tt, tf)
        u1 = jnp.where(tpos == 0, prev[:, 1:2, :], pltpu.roll(u, 1, 0).reshape(bb, tt, tf))
        u2 = pltpu.roll(u, 2, 0).reshape(bb, tt, tf)
        u2 = jnp.where(tpos == 0, prev[:, 0:1, :], jnp.where(tpos == 1, prev[:, 1:2, :], u2))
        conv = cb_ref[:, fs] + cw_ref[0:1, fs] * u2 + cw_ref[1:2, fs] * u1 + cw_ref[2:3, fs] * u3
        acts.append((_silu(conv) * gate.reshape(bb, tt, tf)).reshape(bb * tt, tf).astype(BF16))
        tails.append(u3[:, tt - 2:tt, :])
        if len(acts) == FF_GROUP or f == nf - 1:
            lo_f = f + 1 - len(acts)
            acc[...] += jnp.dot(jnp.concatenate(acts, axis=1) if len(acts) > 1 else acts[0],
                                wd_ref[lo_f * tf:(f + 1) * tf, :], preferred_element_type=F32)
            acts = []
    tail = jnp.concatenate(tails, axis=-1)
    carry[...] = tail
    st_ref[...] = tail
    o_ref[...] = acc[...].reshape(bb, tt, d)


def _ffn(x3, gain, w_in, conv_w, conv_b, w_down, state_all, layer, bb, tt, tf=256):
    batch, t_len, d = x3.shape
    assert t_len >= CONV_W - 1 and tt >= CONV_W - 1 and D_FF % tf == 0
    const = lambda arr: pl.BlockSpec(arr.shape, lambda b, t: (0, 0), pipeline_mode=pl.Buffered(1))
    return pl.pallas_call(
        functools.partial(_ffn_body, bb=bb, tt=tt, tf=tf),
        grid=(batch // bb, t_len // tt),
        in_specs=[pl.BlockSpec((bb, tt, d), lambda b, t: (b, t, 0)),
                  const(gain), const(w_in), const(conv_w), const(conv_b), const(w_down),
                  pl.BlockSpec((None, bb, CONV_W - 1, D_FF), lambda b, t: (layer, b, 0, 0))],
        out_specs=[pl.BlockSpec((bb, tt, d), lambda b, t: (b, t, 0)),
                   pl.BlockSpec((bb, CONV_W - 1, D_FF), lambda b, t: (b, 0, 0))],
        out_shape=[jax.ShapeDtypeStruct((batch, t_len, d), F32),
                   jax.ShapeDtypeStruct((batch, CONV_W - 1, D_FF), F32)],
        scratch_shapes=[pltpu.VMEM((bb * tt, d), F32), pltpu.VMEM((bb, CONV_W - 1, D_FF), F32)],
        compiler_params=_params(2),
        name="conv_ffn",
    )(x3, gain, w_in, conv_w, conv_b, w_down, state_all)


def _trunk(x3, mem_kv, a_bufs, hg_s, rw_s, rw_shift, ffn_buf, w, prompt):
    batch, t_len, d = x3.shape
    m = batch * t_len
    n_a = w['attn_w_qkv'].shape[0]
    new_a = None
    new_hg, new_rw, new_sh, new_ffn = [], [], [], []
    if prompt:
        xa_bb, xa_tt = 1, 512
        ff_bb, ff_tt = 1, 512
        rw_bb, rw_tt = 1, 256
    else:
        xa_bb, xa_tt = 4, t_len
        ff_bb, ff_tt = batch, t_len
        rw_bb, rw_tt = batch, t_len
    for i in range(DEPTH):
        kind, j = i % N_MIXERS, i // N_MIXERS
        x2 = x3.reshape(m, d)
        g_mix = w['norm_mix'][i][None]
        if kind == 0:
            qkv = _norm_matmul(x2, g_mix, w['attn_w_qkv'][j], w['attn_head_gain'][j], hn_width=A_HEAD_DIM,
                               hn_tiles=(True,) * (2 * N_GROUPS) + (False,) * N_GROUPS, tn=A_WIDTH)
            if prompt:
                o = _attn_prompt(qkv, batch, t_len)
                new_a = tuple(_kv_cache_out(qkv, None if new_a is None else new_a[gi], gi, j, n_a, batch, t_len)
                              for gi in range(N_GROUPS))
            else:
                o, c0, c1, c2 = _attn_sample(qkv, a_bufs, new_a, j, batch, t_len)
                new_a = (c0, c1, c2)
            w_mix = w['attn_w_o'][j]
        elif kind == 1:
            proj = _norm_matmul(x2, g_mix, w['hg_w_in'][j])
            o, st = _hgrn(proj, w['hg_lb_logits'], w['hg_out_gain'][j][None], hg_s[j], i, batch, t_len)
            new_hg.append(st)
            w_mix = w['hg_w_o'][j]
        else:
            proj = _rwkv_proj(x3, w['norm_mix'][i][None], rw_shift[j][:, None, :], w['rw_mu'][j], w['rw_w_rkv'][j],
                              w['rw_w0'][j][None], w['rw_w1'][j], w['rw_w2'][j], w['rw_a0'][j][None],
                              w['rw_a1'][j], w['rw_a2'][j], w['rw_g1'][j], w['rw_g2'][j],
                              w['rw_k_k'][j][None], w['rw_k_a'][j][None], rw_bb, rw_tt)
            o, st = _rwkv_rec(proj[:7], w['rw_r_k'][j].reshape(1, d), w['rw_ln_g'][j][None], w['rw_ln_b'][j][None],
                              rw_s[j], batch, t_len)
            new_rw.append(st)
            new_sh.append(proj[7][:, 0, :])
            w_mix = w['rw_w_o'][j]
        x3 = _xattn(x3, o, w_mix, w['norm_mem'][i][None], w['xa_w_q'][i], w['xa_q_gain'][i][None], mem_kv, i,
                    w['xa_w_o'][i], xa_bb, xa_tt)
        x3, fb = _ffn(x3, w['norm_ffn'][i][None], w['ffn_w_in'][i], w['ffn_conv_w'][i], w['ffn_conv_b'][i][None],
                      w['ffn_w_down'][i], ffn_buf, i, ff_bb, ff_tt)
        new_ffn.append(fb)
    return (x3, new_a, jnp.stack(new_hg), jnp.stack(new_rw), jnp.stack(new_sh), jnp.stack(new_ffn))


def kernel(x_prompt, x_sample, mem_prompt, cache_attn_kv_w128, cache_attn_kv_w512, cache_attn_kv_w2048, state_hgrn, state_rwkv, state_rwkv_shift, state_ffn_conv, cache_mem_kv, norm_mix, norm_mem, norm_ffn, mem_norm, attn_w_qkv, attn_q_gain, attn_k_gain, attn_w_o, hg_w_in, hg_lb_logits, hg_out_gain, hg_w_o, rw_mu, rw_w_rkv, rw_w0, rw_w1, rw_w2, rw_a0, rw_a1, rw_a2, rw_g1, rw_g2, rw_k_k, rw_k_a, rw_r_k, rw_ln_g, rw_ln_b, rw_w_o, xa_w_q, xa_w_kv, xa_q_gain, xa_k_gain, xa_w_o, ffn_w_in, ffn_conv_w, ffn_conv_b, ffn_w_down):
    d = D_MODEL
    bp = x_prompt.shape[0]
    bs = x_sample.shape[0]
    n_a, n_b, n_c = attn_w_qkv.shape[0], hg_w_in.shape[0], rw_w_rkv.shape[0]
    bf = lambda a: a.astype(BF16)

    tile_h = lambda gn: jnp.broadcast_to(gn[:, :, None, :], (n_a, N_GROUPS, A_HEADS, A_HEAD_DIM)).reshape(n_a, 1, -1)
    head_gain = jnp.concatenate([tile_h(attn_q_gain), tile_h(attn_k_gain),
                                 jnp.ones((n_a, 1, N_GROUPS * A_WIDTH), F32)], axis=-1)

    w = {
        'norm_mix': norm_mix, 'norm_mem': norm_mem, 'norm_ffn': norm_ffn,
        'attn_w_qkv': bf(attn_w_qkv), 'attn_head_gain': head_gain, 'attn_w_o': bf(attn_w_o),
        'hg_w_in': bf(hg_w_in), 'hg_lb_logits': hg_lb_logits, 'hg_out_gain': hg_out_gain, 'hg_w_o': bf(hg_w_o),
        'rw_mu': rw_mu, 'rw_w_rkv': bf(rw_w_rkv), 'rw_w0': rw_w0, 'rw_w1': bf(rw_w1), 'rw_w2': bf(rw_w2),
        'rw_a0': rw_a0, 'rw_a1': bf(rw_a1), 'rw_a2': bf(rw_a2), 'rw_g1': bf(rw_g1), 'rw_g2': bf(rw_g2),
        'rw_k_k': rw_k_k, 'rw_k_a': rw_k_a, 'rw_r_k': rw_r_k, 'rw_ln_g': rw_ln_g, 'rw_ln_b': rw_ln_b,
        'rw_w_o': bf(rw_w_o),
        'xa_w_q': bf(xa_w_q), 'xa_q_gain': xa_q_gain, 'xa_w_o': bf(xa_w_o),
        'ffn_w_in': bf(ffn_w_in), 'ffn_conv_w': ffn_conv_w, 'ffn_conv_b': ffn_conv_b, 'ffn_w_down': bf(ffn_w_down),
    }

    mem_kv_prompt, mem_kv_flat = _memory_kv(mem_prompt.reshape(bp * N_MEM, d), mem_norm[:, None, :], bf(xa_w_kv),
                                            xa_k_gain[:, None, :], bp)

    as_rows = lambda cc: cc.reshape(cc.shape[0], cc.shape[1], cc.shape[2] * cc.shape[3] * cc.shape[4], cc.shape[5])
    as_cache = lambda rr: rr.reshape(rr.shape[0], rr.shape[1], rr.shape[2] // KV_ROWS, 2, A_HEADS, A_HEAD_DIM)
    y_prompt, a_p, hg_p, rw_p, sh_p, ffn_p = _trunk(
        x_prompt, mem_kv_flat, None,
        jnp.zeros((n_b, bp, HG_HEADS, HG_DK, HG_DK), F32),
        jnp.zeros((n_c, bp, RW_HEADS, RW_HEAD_DIM, RW_HEAD_DIM), F32),
        jnp.zeros((n_c, bp, d), F32),
        jnp.zeros((DEPTH, bp, CONV_W - 1, D_FF), F32), w, True)
    y_sample, a_s, hg_s, rw_s, sh_s, ffn_s = _trunk(
        x_sample, cache_mem_kv,
        tuple(as_rows(cc) for cc in (cache_attn_kv_w128, cache_attn_kv_w512, cache_attn_kv_w2048)),
        state_hgrn, state_rwkv, state_rwkv_shift, state_ffn_conv, w, False)
    a_p = tuple(as_cache(rr) for rr in a_p)
    a_s = tuple(as_cache(rr) for rr in a_s)
    return (y_prompt, y_sample, a_p[0], a_p[1], a_p[2], hg_p, rw_p, sh_p, ffn_p, mem_kv_prompt,
            a_s[0], a_s[1], a_s[2], hg_s, rw_s, sh_s, ffn_s)
```

```python
import functools
import math

import numpy as np
import jax
import jax.numpy as jnp
from jax import lax
from jax.experimental import pallas as pl
from jax.experimental.pallas import tpu as pltpu

F32 = jnp.float32
BF16 = jnp.bfloat16

D_MODEL = 1024
DEPTH = 4
N_MIXERS = 3
DIL_WINDOWS = (128, 512, 2048)
DIL_RATES = (1, 4, 16)
N_GROUPS = 3
A_SPAN = 128
A_HEADS = 4
A_HEAD_DIM = 128
A_BLOCK = 128
A_UNROLL = 8
A_WIDTH = A_HEADS * A_HEAD_DIM
KV_ROWS = 2 * A_HEADS
HG_HEADS = 8
HG_DK = 128
HG_CHUNK = 128
HG_SHORT_CHUNK = 16
RW_HEADS = 16
RW_HEAD_DIM = 64
RW_CHUNK = 64
RW_GN_EPS = 64e-5
N_MEM = 256
XA_HEADS = 4
XA_HEAD_DIM = D_MODEL // XA_HEADS
D_FF = 2816
CONV_W = 3
FF_GROUP = 6
RMS_EPS = 1e-6
LANES = 128
VMEM_LIMIT = 52 * 1024 * 1024

NT_DIMS = (((1,), (1,)), ((), ()))
TN_DIMS = (((0,), (0,)), ((), ()))


def _params(n_axes):
    return pltpu.CompilerParams(dimension_semantics=("arbitrary",) * n_axes, vmem_limit_bytes=VMEM_LIMIT)


def _rms(x):
    return x * lax.rsqrt(jnp.mean(x * x, axis=-1, keepdims=True) + RMS_EPS)


def _dot(a, b):
    return jnp.dot(a.astype(BF16), b.astype(BF16), preferred_element_type=F32)


def _dot_nt(a, b):
    return lax.dot_general(a.astype(BF16), b.astype(BF16), NT_DIMS, preferred_element_type=F32)


def _dot_tn(a, b):
    return lax.dot_general(a.astype(BF16), b.astype(BF16), TN_DIMS, preferred_element_type=F32)


def _split3(x):
    p1 = x.astype(BF16)
    r1 = x - p1.astype(F32)
    p2 = r1.astype(BF16)
    p3 = (r1 - p2.astype(F32)).astype(BF16)
    return jnp.concatenate([p1, p2, p3], axis=1)


def _exact_rowmix(mat_bf16, x):
    w = x.shape[1]
    y = jnp.dot(mat_bf16, _split3(x), preferred_element_type=F32)
    return y[:, :w] + y[:, w:2 * w] + y[:, 2 * w:]


def _softplus(x):
    return jnp.maximum(x, 0.0) + jnp.log1p(jnp.exp(-jnp.abs(x)))


def _sigmoid(x):
    return 1.0 / (1.0 + jnp.exp(-x))


def _silu(x):
    return x * _sigmoid(x)


def _nmm_body(x_ref, g_ref, w_ref, hg_ref, o_ref, *, hn_width, hn_tiles, tn):
    h = (_rms(x_ref[...]) * g_ref[...]).astype(BF16)
    for c in range(w_ref.shape[1] // tn):
        y = jnp.dot(h, w_ref[:, c * tn:(c + 1) * tn], preferred_element_type=F32)
        if hn_width is not None and hn_tiles[c]:
            for cc in range(tn // hn_width):
                sl = slice(c * tn + cc * hn_width, c * tn + (cc + 1) * hn_width)
                o_ref[:, sl] = _rms(y[:, cc * hn_width:(cc + 1) * hn_width]) * hg_ref[:, sl]
        else:
            o_ref[:, c * tn:(c + 1) * tn] = y


def _norm_matmul(x, gain, w, head_gain=None, hn_width=None, hn_tiles=None, tm=256, tn=512):
    m, d = x.shape
    n = w.shape[1]
    tm = min(tm, m)
    if head_gain is None:
        head_gain = jnp.ones((1, n), F32)
    return pl.pallas_call(
        functools.partial(_nmm_body, hn_width=hn_width, hn_tiles=hn_tiles, tn=tn),
        grid=(m // tm,),
        in_specs=[
            pl.BlockSpec((tm, d), lambda i: (i, 0)),
            pl.BlockSpec((1, d), lambda i: (0, 0)),
            pl.BlockSpec((d, n), lambda i: (0, 0)),
            pl.BlockSpec((1, n), lambda i: (0, 0)),
        ],
        out_specs=pl.BlockSpec((tm, n), lambda i: (i, 0)),
        out_shape=jax.ShapeDtypeStruct((m, n), F32),
        compiler_params=_params(1),
        name="norm_matmul",
    )(x, gain, w, head_gain)


def _ds(start, size, stride):
    return pl.ds(start, size) if stride == 1 else pl.ds(start, size, stride=stride)


def _attn_prompt_body(q0, q1, q2, k0, v0, k1, v1, k2, v2, o_ref, acc, m_s, l_s, *, seq):
    scale = A_HEAD_DIM ** -0.5
    blk = A_BLOCK
    for g, (q_ref, k_ref, v_ref) in enumerate(((q0, k0, v0), (q1, k1, v1), (q2, k2, v2))):
        dil = DIL_RATES[g]
        n = seq // dil
        nb = n // blk
        width = 2 * blk if nb > 1 else blk
        ii = lax.broadcasted_iota(jnp.int32, (blk, width), 0)
        jj = lax.broadcasted_iota(jnp.int32, (blk, width), 1)
        if nb > 1:
            in_cur = (jj >= blk) & (jj - blk <= ii)
            in_prev = (jj < blk) & (jj >= ii)
        else:
            in_cur, in_prev = jj <= ii, None

        def body(it, carry, g=g, dil=dil, nb=nb, q_ref=q_ref, k_ref=k_ref, v_ref=v_ref, in_cur=in_cur,
                 in_prev=in_prev):
            rows, ss, vws = [], [], []
            for uu in range(A_UNROLL):
                idx = it * A_UNROLL + uu
                r = idx // nb
                mb = idx % nb
                rw = _ds(r + dil * blk * mb, blk, dil)
                kc = k_ref[rw, :].astype(BF16)
                vc = v_ref[rw, :].astype(BF16)
                if nb > 1:
                    prev = _ds(r + dil * blk * jnp.maximum(mb - 1, 0), blk, dil)
                    kw = jnp.concatenate([k_ref[prev, :].astype(BF16), kc], axis=0)
                    vw = jnp.concatenate([v_ref[prev, :].astype(BF16), vc], axis=0)
                    prev_bias = jnp.where(mb > 0, 0.0, -jnp.inf)
                    bias = jnp.where(in_cur, 0.0, jnp.where(in_prev, prev_bias, -jnp.inf))
                else:
                    kw, vw = kc, vc
                    bias = jnp.where(in_cur, 0.0, -jnp.inf)
                rows.append(rw)
                vws.append(jnp.concatenate([vw, jnp.ones((vw.shape[0], LANES), BF16)], axis=1))
                ss.append(_dot_nt(q_ref[rw, :], kw) * scale + bias)
            ms = [jnp.max(s, axis=-1, keepdims=True) for s in ss]
            ps = [jnp.exp(s - m) for s, m in zip(ss, ms)]
            ols = [_dot(p, vw) for p, vw in zip(ps, vws)]
            os_ = [x[:, 0:A_HEAD_DIM] for x in ols]
            l_bs = [x[:, A_HEAD_DIM:A_HEAD_DIM + LANES] for x in ols]
            m_bs = [jnp.broadcast_to(m, (blk, LANES)) for m in ms]
            if g == 0:
                for rw, o, m_b, l_b in zip(rows, os_, m_bs, l_bs):
                    acc[rw, :] = o
                    m_s[rw, :] = m_b
                    l_s[rw, :] = l_b
            else:
                olds = [(m_s[rw, :], l_s[rw, :], acc[rw, :]) for rw in rows]
                for rw, o, m_b, l_b, (m_old, l_old, acc_old) in zip(rows, os_, m_bs, l_bs, olds):
                    m_new = jnp.maximum(m_old, m_b)
                    a_old = jnp.exp(m_old - m_new)
                    a_cur = jnp.exp(m_b - m_new)
                    acc[rw, :] = acc_old * a_old + o * a_cur
                    l_s[rw, :] = l_old * a_old + l_b * a_cur
                    m_s[rw, :] = m_new
            return carry

        assert (dil * nb) % A_UNROLL == 0
        lax.fori_loop(0, dil * nb // A_UNROLL, body, 0)
    o_ref[...] = acc[...] / l_s[...]


def _attn_prompt(qkv, batch, seq):
    m = batch * seq
    assert seq % (A_BLOCK * max(DIL_RATES)) == 0 and A_BLOCK == A_SPAN
    nq = N_GROUPS * A_HEADS

    def spec(col_fn):
        return pl.BlockSpec((seq, A_HEAD_DIM), lambda b, h: (b, col_fn(h)))

    in_specs = [spec(lambda h, g=g: g * A_HEADS + h) for g in range(N_GROUPS)]
    for g in range(N_GROUPS):
        in_specs.append(spec(lambda h, g=g: nq + g * A_HEADS + h))
        in_specs.append(spec(lambda h, g=g: 2 * nq + g * A_HEADS + h))
    return pl.pallas_call(
        functools.partial(_attn_prompt_body, seq=seq),
        grid=(batch, A_HEADS),
        in_specs=in_specs,
        out_specs=pl.BlockSpec((seq, A_HEAD_DIM), lambda b, h: (b, h)),
        out_shape=jax.ShapeDtypeStruct((m, A_WIDTH), F32),
        scratch_shapes=[pltpu.VMEM((seq, A_HEAD_DIM), F32), pltpu.VMEM((seq, LANES), F32),
                        pltpu.VMEM((seq, LANES), F32)],
        compiler_params=_params(2),
        name="dilated_attn_prompt",
    )(*([qkv] * 9))


def _attn_sample_body(*refs, t_new, n_alias):
    qkv_ref, bufs = refs[0], refs[1:4]
    o_ref, outs = refs[4 + n_alias], refs[5 + n_alias:8 + n_alias]
    scale = A_HEAD_DIM ** -0.5
    pad = jnp.zeros((LANES - t_new, A_HEAD_DIM), F32)
    nq = N_GROUPS * A_WIDTH

    def new_rows(g, kv, h):
        base = (1 + kv) * nq + g * A_WIDTH + h * A_HEAD_DIM
        return qkv_ref[:, base:base + A_HEAD_DIM]

    for g in range(N_GROUPS):
        length = bufs[g].shape[0] // KV_ROWS
        keep = (length - t_new) * KV_ROWS
        outs[g][0:keep, :] = bufs[g][t_new * KV_ROWS:length * KV_ROWS, :]
        for kv in range(2):
            for h in range(A_HEADS):
                outs[g][pl.ds(keep + kv * A_HEADS + h, t_new, stride=KV_ROWS), :] = new_rows(g, kv, h)

    for h in range(A_HEADS):
        parts, vals = [], []
        for g in range(N_GROUPS):
            dil = DIL_RATES[g]
            length = bufs[g].shape[0] // KV_ROWS
            q = qkv_ref[:, g * A_WIDTH + h * A_HEAD_DIM: g * A_WIDTH + (h + 1) * A_HEAD_DIM]
            s_buf = _dot_nt(q, bufs[g][pl.ds(h, length, stride=KV_ROWS), :]) * scale
            s_new = _dot_nt(q, jnp.concatenate([new_rows(g, 0, h), pad], axis=0)) * scale
            for s, base in ((s_buf, 0), (s_new, length)):
                ii = lax.broadcasted_iota(jnp.int32, s.shape, 0)
                jj = lax.broadcasted_iota(jnp.int32, s.shape, 1) + base
                dd = jj - ii
                valid = (dd >= 0) & (dd <= A_SPAN * dil) & ((dd & (dil - 1)) == 0) & (jj < length + t_new)
                parts.append(jnp.where(valid, s, -jnp.inf))
            vals.append(bufs[g][pl.ds(A_HEADS + h, length, stride=KV_ROWS), :])
            vals.append(jnp.concatenate([new_rows(g, 1, h), pad], axis=0))
        mx = functools.reduce(jnp.maximum, [jnp.max(s, axis=-1, keepdims=True) for s in parts])
        exps = [jnp.exp(s - mx) for s in parts]
        den = functools.reduce(lambda a, b: a + b, [jnp.sum(e, axis=-1, keepdims=True) for e in exps])
        o = functools.reduce(lambda a, b: a + b, [_dot(e / den, vv) for e, vv in zip(exps, vals)])
        o_ref[:, h * A_HEAD_DIM:(h + 1) * A_HEAD_DIM] = o


def _attn_sample(qkv, caches, prev_out, j, batch, t_new):
    assert t_new % 8 == 0 and t_new <= LANES
    lens = [cch.shape[2] // KV_ROWS for cch in caches]
    for g in range(N_GROUPS):
        assert lens[g] == A_SPAN * DIL_RATES[g] and (DIL_RATES[g] & (DIL_RATES[g] - 1)) == 0

    def cache_spec(g):
        return pl.BlockSpec((None, None, lens[g] * KV_ROWS, A_HEAD_DIM), lambda b: (j, b, 0, 0))

    in_specs = [pl.BlockSpec((t_new, qkv.shape[1]), lambda b: (b, 0))] + [cache_spec(g) for g in range(N_GROUPS)]
    n_alias = 0 if prev_out is None else N_GROUPS
    aliases = {}
    if prev_out is not None:
        in_specs += [pl.BlockSpec(memory_space=pl.ANY)] * N_GROUPS
        aliases = {4 + g: 1 + g for g in range(N_GROUPS)}
    out_specs = [pl.BlockSpec((t_new, A_WIDTH), lambda b: (b, 0))] + [cache_spec(g) for g in range(N_GROUPS)]
    out_shape = [jax.ShapeDtypeStruct((batch * t_new, A_WIDTH), F32)]
    out_shape += [jax.ShapeDtypeStruct(cch.shape, F32) for cch in caches]
    return pl.pallas_call(
        functools.partial(_attn_sample_body, t_new=t_new, n_alias=n_alias),
        grid=(batch,),
        in_specs=in_specs,
        out_specs=out_specs,
        out_shape=out_shape,
        input_output_aliases=aliases,
        compiler_params=_params(1),
        name="dilated_attn_sample",
    )(qkv, *caches, *(prev_out or ()))


def _kv_cache_body(*refs):
    k_ref, v_ref, o_ref = refs[0], refs[1], refs[-1]
    tr = k_ref.shape[0]
    for kv, x_ref in enumerate((k_ref, v_ref)):
        for h in range(A_HEADS):
            o_ref[pl.ds(kv * A_HEADS + h, tr, stride=KV_ROWS), :] = x_ref[:, h * A_HEAD_DIM:(h + 1) * A_HEAD_DIM]


def _kv_cache_out(qkv, prev_out, g, j, n_layers, batch, seq):
    win = min(DIL_WINDOWS[g], seq)
    tr = min(win, 512)
    first = (seq - win) // tr
    row = lambda b, t: b * (seq // tr) + first + t
    in_specs = [pl.BlockSpec((tr, A_WIDTH), lambda b, t: (row(b, t), N_GROUPS + g)),
                pl.BlockSpec((tr, A_WIDTH), lambda b, t: (row(b, t), 2 * N_GROUPS + g))]
    aliases = {}
    if prev_out is not None:
        in_specs.append(pl.BlockSpec(memory_space=pl.ANY))
        aliases = {2: 0}
    return pl.pallas_call(
        _kv_cache_body,
        grid=(batch, win // tr),
        in_specs=in_specs,
        out_specs=pl.BlockSpec((None, None, tr * KV_ROWS, A_HEAD_DIM), lambda b, t: (j, b, t, 0)),
        out_shape=jax.ShapeDtypeStruct((n_layers, batch, win * KV_ROWS, A_HEAD_DIM), F32),
        input_output_aliases=aliases,
        compiler_params=_params(2),
        name="kv_cache_out",
    )(qkv, qkv, *(() if prev_out is None else (prev_out,)))


def _hgrn_consts(c):
    t = np.arange(c)[:, None]
    u = np.arange(c)[None, :]
    masks = [(u == t)]
    s = 1
    while s < c:
        upper = (t // s) % 2 == 1
        lower_u = (u // s) % 2 == 0
        masks.append(upper & lower_u & (t // (2 * s) == u // (2 * s)))
        s *= 2
    masks = np.stack([m.astype(np.float32) for m in masks], axis=0)
    return jnp.asarray((u <= t).astype(np.float32), BF16), jnp.asarray(masks, F32)


def _hgrn_level_exponent(b, s, row):
    c, width = b.shape
    if s >= 8:
        mid = jnp.concatenate([jnp.broadcast_to(b[j * 2 * s + s - 1:j * 2 * s + s, :], (2 * s, width))
                               for j in range(c // (2 * s))], axis=0)
    else:
        b3 = b.reshape(c // 8, 8, width)
        sub = lax.broadcasted_iota(jnp.int32, b3.shape, 1)
        pick = lambda k: jnp.broadcast_to(b3[:, k:k + 1, :], b3.shape)
        mid3 = pick(s - 1)
        for k in range(2 * s + s - 1, 8, 2 * s):
            mid3 = jnp.where(sub >= k - (s - 1), pick(k), mid3)
        mid = mid3.reshape(c, width)
    return jnp.where((row & s) != 0, b - mid, mid - b)


def _hgrn_body(q_ref, z_ref, i_ref, g_ref, lbl_ref, og_ref, s0_ref, mat_ref, msk_ref, o_ref, st_ref, st_scr,
               *, layer, t_len, c, hp):
    n_lvl = msk_ref.shape[0] - 1
    width = hp * HG_DK
    hr = range(hp)
    lane = lambda x, hd: x[:, hd * HG_DK:(hd + 1) * HG_DK]
    lg = lbl_ref[...]
    e = jnp.exp(lg - jnp.max(lg, axis=0, keepdims=True))
    prob = e / jnp.sum(e, axis=0, keepdims=True)
    lb = jnp.sum(prob[0:layer + 1], axis=0, keepdims=True) - prob[0:1]
    log_lb = jnp.log(lb)
    log_1m = jnp.log1p(-lb)
    @pl.when(pl.program_id(2) == 0)
    def _():
        for hd in hr:
            st_scr[hd] = s0_ref[hd].T

    row = lax.broadcasted_iota(jnp.int32, (c, width), 0)
    out_gain = jnp.concatenate([og_ref[...]] * hp, axis=1)

    def chunk(ci, carry):
        if t_len >= c:
            rows = pl.ds(pl.multiple_of(ci * c, c), c)
            ld = lambda ref: ref[rows, :]
            live = None
        else:
            padz = jnp.zeros((c - t_len, width), F32)
            ld = lambda ref: jnp.concatenate([ref[...], padz], axis=0)
            live = row < t_len
        z = ld(z_ref)
        lsig = jnp.minimum(z, 0.0) - jnp.log1p(jnp.exp(-jnp.abs(z)))
        a1 = jnp.broadcast_to(log_lb, z.shape)
        a2 = log_1m + lsig
        hi = jnp.maximum(a1, a2)
        lo = jnp.minimum(a1, a2)
        log_f = hi + jnp.log1p(jnp.exp(lo - hi))
        kg = (1.0 - lb) * _sigmoid(-z)
        if live is not None:
            log_f = jnp.where(live, log_f, 0.0)
            kg = jnp.where(live, kg, 0.0)
        qa = _silu(ld(q_ref))
        vv = ld(i_ref)
        b = _exact_rowmix(mat_ref[...], log_f)
        sts = [st_scr[hd] for hd in hr]
        a = [_dot_nt(lane(qa, hd), lane(kg, hd)) * msk_ref[0] for hd in hr]
        for lv in range(n_lvl):
            wv = jnp.exp(_hgrn_level_exponent(b, 1 << lv, row))
            qw = qa * wv
            kw = kg * wv
            a = [a[hd] + _dot_nt(lane(qw, hd), lane(kw, hd)) * msk_ref[1 + lv] for hd in hr]
        qb = qa * jnp.exp(b)
        kd = kg * jnp.exp(b[c - 1:c] - b)
        dec = jnp.exp(b[c - 1:c])
        o = [_dot_nt(lane(qb, hd), sts[hd]) + _dot(a[hd], lane(vv, hd)) for hd in hr]
        st_scr[...] = jnp.stack([sts[hd] * lane(dec, hd) + _dot_tn(lane(vv, hd), lane(kd, hd)) for hd in hr], axis=0)
        y = jnp.concatenate([_rms(x) for x in o], axis=1) * out_gain * _silu(ld(g_ref))
        if t_len >= c:
            o_ref[rows, :] = y
        else:
            o_ref[...] = y[0:t_len]
        return carry

    lax.fori_loop(0, max(t_len // c, 1), chunk, 0)
    for hd in hr:
        st_ref[hd] = st_scr[hd].T


def _hgrn(proj, lb_logits, out_gain, s0, layer, batch, t_len, tt=512):
    tt = min(tt, t_len)
    if tt >= HG_CHUNK:
        c, hp = HG_CHUNK, 8
        assert tt % c == 0 and t_len % tt == 0
    else:
        c, hp = HG_SHORT_CHUNK, HG_HEADS
        assert tt % 8 == 0 and tt <= c
    mats, masks = _hgrn_consts(c)
    width = hp * HG_DK
    ng = HG_HEADS // hp
    nt = t_len // tt

    def col(part):
        return pl.BlockSpec((tt, width), lambda b, h, t: (b * nt + t, part * ng + h))

    st_spec = pl.BlockSpec((None, hp, HG_DK, HG_DK), lambda b, h, t: (b, h, 0, 0))
    return pl.pallas_call(
        functools.partial(_hgrn_body, layer=layer, t_len=tt, c=c, hp=hp),
        grid=(batch, ng, nt),
        in_specs=[col(0), col(1), col(2), col(3),
                  pl.BlockSpec((DEPTH, width), lambda b, h, t: (0, h)),
                  pl.BlockSpec((1, HG_DK), lambda b, h, t: (0, 0)),
                  st_spec,
                  pl.BlockSpec(mats.shape, lambda b, h, t: (0, 0)),
                  pl.BlockSpec(masks.shape, lambda b, h, t: (0, 0, 0))],
        out_specs=[pl.BlockSpec((tt, width), lambda b, h, t: (b * nt + t, h)), st_spec],
        out_shape=[jax.ShapeDtypeStruct((batch * t_len, HG_HEADS * HG_DK), F32),
                   jax.ShapeDtypeStruct((batch, HG_HEADS, HG_DK, HG_DK), F32)],
        scratch_shapes=[pltpu.VMEM((hp, HG_DK, HG_DK), F32)],
        compiler_params=_params(3),
        name="hgrn2_chunked",
    )(proj, proj, proj, proj, lb_logits, out_gain, s0, mats, masks)


def _rwkv_proj_body(x_ref, g_ref, sh_ref, mu_ref, wrkv_ref, w0_ref, w1_ref, w2_ref, a0_ref, a1_ref, a2_ref,
                    g1_ref, g2_ref, kk_ref, ka_ref,
                    r_o, k_o, v_o, ld_o, kk_o, a_o, g_o, sh_o, carry, *, bb, tt):
    t = pl.program_id(1)
    d = D_MODEL

    @pl.when(t == 0)
    def _():
        carry[...] = sh_ref[...]

    h3 = _rms(x_ref[...]) * g_ref[...]
    h = h3.reshape(bb * tt, d)
    rolled = pltpu.roll(h, 1, 0).reshape(bb, tt, d)
    tpos = lax.broadcasted_iota(jnp.int32, (bb, tt, d), 1)
    prev = jnp.where(tpos == 0, carry[...], rolled).reshape(bb * tt, d)
    last = h3[:, tt - 1:tt, :]
    carry[...] = last
    sh_o[...] = last
    dx = prev - h
    mix = lambda jm: h + dx * mu_ref[jm:jm + 1, :]
    r = _dot(mix(0), wrkv_ref[0])
    k = _dot(mix(1), wrkv_ref[1])
    v = _dot(mix(2), wrkv_ref[2])
    wl = w0_ref[...] + _dot(jnp.tanh(_dot(mix(3), w1_ref[...])), w2_ref[...])
    wlog = -_softplus(-wl) - 0.5
    a = _sigmoid(a0_ref[...] + _dot(_dot(mix(4), a1_ref[...]), a2_ref[...]))
    gate = _dot(_sigmoid(_dot(mix(5), g1_ref[...])), g2_ref[...])
    r_o[...] = r
    k_o[...] = k * (1.0 + (a - 1.0) * ka_ref[...])
    v_o[...] = v
    ld_o[...] = -jnp.exp(wlog)
    kk_o[...] = k * kk_ref[...]
    a_o[...] = a
    g_o[...] = gate


def _rwkv_proj(x3, gain, shift, mu, wrkv, w0, w1, w2, a0, a1, a2, g1, g2, k_k, k_a, bb, tt):
    batch, t_len, d = x3.shape
    m = batch * t_len
    rows = bb * tt
    full = lambda arr: pl.BlockSpec(arr.shape, lambda b, t: (0,) * arr.ndim)
    row_spec = pl.BlockSpec((rows, d), lambda b, t: (b * (t_len // tt) + t, 0))
    outs = pl.pallas_call(
        functools.partial(_rwkv_proj_body, bb=bb, tt=tt),
        grid=(batch // bb, t_len // tt),
        in_specs=[pl.BlockSpec((bb, tt, d), lambda b, t: (b, t, 0)), full(gain),
                  pl.BlockSpec((bb, 1, d), lambda b, t: (b, 0, 0)), full(mu), full(wrkv),
                  full(w0), full(w1), full(w2), full(a0), full(a1), full(a2), full(g1), full(g2),
                  full(k_k), full(k_a)],
        out_specs=[row_spec] * 7 + [pl.BlockSpec((bb, 1, d), lambda b, t: (b, 0, 0))],
        out_shape=[jax.ShapeDtypeStruct((m, d), F32)] * 7 + [jax.ShapeDtypeStruct((batch, 1, d), F32)],
        scratch_shapes=[pltpu.VMEM((bb, 1, d), F32)],
        compiler_params=_params(2),
        name="rwkv_proj",
    )(x3, gain, shift, mu, wrkv, w0, w1, w2, a0, a1, a2, g1, g2, k_k, k_a)
    return outs


def _rwkv_consts(c):
    n = RW_HEAD_DIM
    t = np.arange(c)[:, None]
    u = np.arange(c)[None, :]
    f = lambda mat, dt: jnp.asarray(np.asarray(mat).astype(np.float32), dt)
    tri = f(u <= t, BF16)
    half = np.concatenate([u < t, u <= t], axis=0)
    lo = f(np.concatenate([half, half], axis=1), F32)
    eye = f(u == t, F32)
    lane = np.arange(LANES)
    bdiag = f((lane[:, None] // n) == (lane[None, :] // n), F32)
    return tri, lo, eye, bdiag


def _rwkv_rec_body(r_ref, k_ref, v_ref, ld_ref, kk_ref, a_ref, g_ref, rk_ref, lng_ref, lnb_ref, s0_ref,
                   tri_ref, lo_ref, eye_ref, bd_ref, o_ref, st_ref, st_scr, *, tt):
    c = RW_CHUNK
    n = RW_HEAD_DIM
    heads = RW_HEADS
    pairs = heads // 2
    w = D_MODEL
    t = pl.program_id(1)

    @pl.when(t == 0)
    def _():
        zero = jnp.zeros((n, n), F32)
        for p in range(pairs):
            top = jnp.concatenate([s0_ref[2 * p], zero], axis=1)
            bot = jnp.concatenate([zero, s0_ref[2 * p + 1]], axis=1)
            st_scr[p] = jnp.concatenate([top, bot], axis=0)

    low_2c = lax.broadcasted_iota(jnp.int32, (2 * c, LANES), 1) < n
    low_c = lax.broadcasted_iota(jnp.int32, (c, LANES), 1) < n
    tile = lambda x, p: x[:, p * LANES:(p + 1) * LANES]
    hr = range(heads)
    pr = range(pairs)
    zeros_c = jnp.zeros((c, LANES), F32)

    def head_sums(x):
        out = []
        for p in pr:
            xt = tile(x, p)
            lo_s = jnp.sum(jnp.where(low_c, xt, 0.0), axis=-1, keepdims=True)
            hi_s = jnp.sum(jnp.where(low_c, 0.0, xt), axis=-1, keepdims=True)
            out.append(jnp.where(low_c, lo_s, hi_s))
        return jnp.concatenate(out, axis=1)

    def chunk(ci, carry):
        if tt >= c:
            rows = pl.ds(pl.multiple_of(ci * c, c), c)
            ld = lambda ref: ref[rows, :]
        else:
            padz = jnp.zeros((c - tt, w), F32)
            ld = lambda ref: jnp.concatenate([ref[...], padz], axis=0)
        r, kf, v, lc, kk, a, gate = (ld(ref) for ref in (r_ref, k_ref, v_ref, ld_ref, kk_ref, a_ref, g_ref))
        kk = kk / jnp.maximum(jnp.sqrt(head_sums(kk * kk)), 1e-12)
        bvec = kk * a
        cum = _exact_rowmix(tri_ref[...], lc)
        c_end = cum[c - 1:c]
        e_neg = jnp.exp(-cum)
        e_end = jnp.exp(c_end - cum)
        d_end = jnp.exp(c_end)
        kr = jnp.concatenate([kk * jnp.exp(cum - lc), r * jnp.exp(cum)], axis=0)
        bx = jnp.concatenate([bvec * e_neg, kf * e_neg], axis=0)
        bk = jnp.concatenate([bvec * e_end, kf * e_end], axis=0)
        lo = lo_ref[...]
        eye = eye_ref[...]
        krm = [jnp.where(low_2c if h % 2 == 0 else jnp.logical_not(low_2c), tile(kr, h // 2), 0.0) for h in hr]
        gm = [_dot_nt(krm[h], tile(bx, h // 2)) * lo for h in hr]
        nn = [x[0:c, 0:c] for x in gm]
        tm = [eye - x for x in nn]
        npow = [_dot(x, x) for x in nn]
        p2 = 2
        while p2 < c:
            tm = [x + _dot(x, y) for x, y in zip(tm, npow)]
            p2 *= 2
            if p2 < c:
                npow = [_dot(x, x) for x in npow]
        sts = [st_scr[p] for p in pr]
        ks = [_dot_nt(tile(kr, p), sts[p]) for p in pr]
        vz = [jnp.concatenate([zeros_c, tile(v, p)], axis=0) for p in pr]
        av = [jnp.where(low_2c, _dot(gm[2 * p], vz[p]), _dot(gm[2 * p + 1], vz[p])) for p in pr]
        rhs = [ks[p][0:c] + av[p][0:c] for p in pr]
        u = [jnp.where(low_c, -_dot(tm[2 * p], rhs[p]), -_dot(tm[2 * p + 1], rhs[p])) for p in pr]
        uz = [jnp.concatenate([u[p], zeros_c], axis=0) for p in pr]
        y = [ks[p][c:2 * c] + av[p][c:2 * c]
             + jnp.where(low_c, _dot(gm[2 * p][c:2 * c], uz[p]), _dot(gm[2 * p + 1][c:2 * c], uz[p])) for p in pr]
        new_st = [sts[p] * tile(d_end, p)
                  + _dot_tn(jnp.concatenate([u[p], tile(v, p)], axis=0), tile(bk, p)) * bd_ref[...] for p in pr]
        st_scr[...] = jnp.stack(new_st, axis=0)
        yy = jnp.concatenate(y, axis=1)
        yc = yy - head_sums(yy) * (1.0 / n)
        var = head_sums(yc * yc) * (1.0 / n)
        yn = yc * lax.rsqrt(var + RW_GN_EPS) * lng_ref[...] + lnb_ref[...]
        z = (yn + head_sums(r * kf * rk_ref[...]) * v) * gate
        if tt >= c:
            o_ref[rows, :] = z
        else:
            o_ref[...] = z[0:tt]
        return carry

    lax.fori_loop(0, max(tt // c, 1), chunk, 0)
    for p in range(pairs):
        blk = st_scr[p]
        st_ref[2 * p] = blk[0:n, 0:n]
        st_ref[2 * p + 1] = blk[n:2 * n, n:2 * n]


def _rwkv_rec(proj, r_k, ln_g, ln_b, s0, batch, t_len, tt=512):
    r, kf, v, ld, kk, a, gate = proj
    tt = min(tt, t_len)
    assert tt % RW_CHUNK == 0 or (tt < RW_CHUNK and tt % 8 == 0)
    assert RW_HEADS * RW_HEAD_DIM == D_MODEL and 2 * RW_HEAD_DIM == LANES and 2 * RW_CHUNK == LANES
    consts = _rwkv_consts(RW_CHUNK)
    nt = t_len // tt
    row_spec = pl.BlockSpec((tt, D_MODEL), lambda b, t: (b * nt + t, 0))
    par_spec = pl.BlockSpec((1, D_MODEL), lambda b, t: (0, 0))
    st_spec = pl.BlockSpec((None, RW_HEADS, RW_HEAD_DIM, RW_HEAD_DIM), lambda b, t: (b, 0, 0, 0))
    const = lambda arr: pl.BlockSpec(arr.shape, lambda b, t: (0, 0))
    return pl.pallas_call(
        functools.partial(_rwkv_rec_body, tt=tt),
        grid=(batch, nt),
        in_specs=[row_spec] * 7 + [par_spec] * 3 + [st_spec] + [const(x) for x in consts],
        out_specs=[row_spec, st_spec],
        out_shape=[jax.ShapeDtypeStruct((batch * t_len, D_MODEL), F32),
                   jax.ShapeDtypeStruct((batch, RW_HEADS, RW_HEAD_DIM, RW_HEAD_DIM), F32)],
        scratch_shapes=[pltpu.VMEM((RW_HEADS // 2, LANES, LANES), F32)],
        compiler_params=_params(2),
        name="rwkv7_chunked",
    )(r, kf, v, ld, kk, a, gate, r_k, ln_g, ln_b, s0, *consts)


def _xattn_body(x_ref, a_ref, wm_ref, g_ref, wq_ref, qg_ref, kv_ref, wo_ref, o_ref, *, bb, tt, native):
    d = D_MODEL
    e = XA_HEAD_DIM
    scale = XA_HEAD_DIM ** -0.5
    x = x_ref[...].reshape(bb * tt, d) + jnp.dot(a_ref[...].astype(BF16), wm_ref[...], preferred_element_type=F32)
    q = jnp.dot((_rms(x) * g_ref[...]).astype(BF16), wq_ref[...], preferred_element_type=F32)
    qn = lambda b, h: _rms(q[b * tt:(b + 1) * tt, h * e:(h + 1) * e]) * qg_ref[...]
    per_batch = []
    for b in range(bb):
        if native:
            rows = N_MEM * XA_HEADS
            s = _dot_nt(jnp.concatenate([qn(b, h) for h in range(XA_HEADS)], axis=0),
                        kv_ref[b, :, 0].reshape(rows, e)) * scale
            own = (lax.broadcasted_iota(jnp.int32, s.shape, 0) // tt
                   == lax.broadcasted_iota(jnp.int32, s.shape, 1) % XA_HEADS)
            s = jnp.where(own, s, -jnp.inf)
            p = jnp.exp(s - jnp.max(s, axis=-1, keepdims=True))
            p = p / jnp.sum(p, axis=-1, keepdims=True)
            oh = _dot(p, kv_ref[b, :, 1].reshape(rows, e))
            per_batch.append(jnp.concatenate([oh[h * tt:(h + 1) * tt] for h in range(XA_HEADS)], axis=-1))
            continue
        heads = []
        for h in range(XA_HEADS):
            s = _dot_nt(qn(b, h), kv_ref[b, :, h * e:(h + 1) * e]) * scale
            p = jnp.exp(s - jnp.max(s, axis=-1, keepdims=True))
            p = p / jnp.sum(p, axis=-1, keepdims=True)
            heads.append(_dot(p, kv_ref[b, :, d + h * e:d + (h + 1) * e]))
        per_batch.append(jnp.concatenate(heads, axis=-1))
    o = jnp.concatenate(per_batch, axis=0) if bb > 1 else per_batch[0]
    y = x + jnp.dot(o.astype(BF16), wo_ref[...], preferred_element_type=F32)
    o_ref[...] = y.reshape(bb, tt, d)


def _xattn(x3, mix_out, w_mix, gain, wq, q_gain, kv_all, layer, wo, bb, tt):
    batch, t_len, d = x3.shape
    km = mix_out.shape[1]
    nt = t_len // tt
    assert bb == 1 or nt == 1
    native = kv_all.ndim == 6
    if native:
        kv_spec = pl.BlockSpec((None, bb, N_MEM, 2, XA_HEADS, XA_HEAD_DIM), lambda b, t: (layer, b, 0, 0, 0, 0))
    else:
        kv_spec = pl.BlockSpec((None, bb, N_MEM, 2 * d), lambda b, t: (layer, b, 0, 0))
    return pl.pallas_call(
        functools.partial(_xattn_body, bb=bb, tt=tt, native=native),
        grid=(batch // bb, t_len // tt),
        in_specs=[pl.BlockSpec((bb, tt, d), lambda b, t: (b, t, 0)),
                  pl.BlockSpec((bb * tt, km), lambda b, t: (b * nt + t, 0)),
                  pl.BlockSpec((km, d), lambda b, t: (0, 0)),
                  pl.BlockSpec((1, d), lambda b, t: (0, 0)),
                  pl.BlockSpec((d, d), lambda b, t: (0, 0)),
                  pl.BlockSpec((1, XA_HEAD_DIM), lambda b, t: (0, 0)),
                  kv_spec,
                  pl.BlockSpec((d, d), lambda b, t: (0, 0))],
        out_specs=pl.BlockSpec((bb, tt, d), lambda b, t: (b, t, 0)),
        out_shape=jax.ShapeDtypeStruct((batch, t_len, d), F32),
        compiler_params=_params(2),
        name="memory_xattn",
    )(x3, mix_out, w_mix, gain, wq, q_gain, kv_all, wo)


def _memkv_body(x_ref, g_ref, w_ref, kg_ref, o_ref, flat_ref):
    c = pl.program_id(1)
    e = XA_HEAD_DIM
    h = (_rms(x_ref[...]) * g_ref[...]).astype(BF16)
    y = jnp.dot(h, w_ref[...], preferred_element_type=F32)

    @pl.when(c == 0)
    def _():
        for h in range(XA_HEADS):
            kn = _rms(y[:, h * e:(h + 1) * e]) * kg_ref[...]
            o_ref[:, h, :] = kn
            flat_ref[:, h * e:(h + 1) * e] = kn

    @pl.when(c == 1)
    def _():
        flat_ref[...] = y
        for h in range(XA_HEADS):
            o_ref[:, h, :] = y[:, h * e:(h + 1) * e]


def _memory_kv(mem2, mem_norm, w_kv, k_gain, batch):
    d = D_MODEL
    return pl.pallas_call(
        _memkv_body,
        grid=(DEPTH, 2, batch),
        in_specs=[pl.BlockSpec((N_MEM, d), lambda l, c, b: (b, 0)),
                  pl.BlockSpec((None, 1, d), lambda l, c, b: (l, 0, 0)),
                  pl.BlockSpec((None, d, d), lambda l, c, b: (l, 0, c)),
                  pl.BlockSpec((None, 1, XA_HEAD_DIM), lambda l, c, b: (l, 0, 0))],
        out_specs=[pl.BlockSpec((None, None, N_MEM, None, XA_HEADS, XA_HEAD_DIM), lambda l, c, b: (l, b, 0, c, 0, 0)),
                   pl.BlockSpec((None, None, N_MEM, d), lambda l, c, b: (l, b, 0, c))],
        out_shape=[jax.ShapeDtypeStruct((DEPTH, batch, N_MEM, 2, XA_HEADS, XA_HEAD_DIM), F32),
                   jax.ShapeDtypeStruct((DEPTH, batch, N_MEM, 2 * d), F32)],
        compiler_params=_params(3),
        name="memory_kv",
    )(mem2, mem_norm, w_kv, k_gain)


def _ffn_body(x_ref, g_ref, win_ref, cw_ref, cb_ref, wd_ref, pin_ref, o_ref, st_ref, acc, carry, *, bb, tt, tf):
    t = pl.program_id(1)
    d = D_MODEL

    @pl.when(t == 0)
    def _():
        carry[...] = pin_ref[...]

    x = x_ref[...].reshape(bb * tt, d)
    h = (_rms(x) * g_ref[...]).astype(BF16)
    acc[...] = x
    tpos = lax.broadcasted_iota(jnp.int32, (bb, tt, tf), 1)
    tails, acts = [], []
    nf = D_FF // tf
    for f in range(nf):
        fs = slice(f * tf, (f + 1) * tf)
        u = jnp.dot(h, win_ref[:, fs], preferred_element_type=F32)
        gate = jnp.dot(h, win_ref[:, D_FF + f * tf:D_FF + (f + 1) * tf], preferred_element_type=F32)
        prev = carry[:, :, fs]
        u3 = u.reshape(bb, tt, tf)
        u1 = jnp.where(tpos == 0, prev[:, 1:2, :], pltpu.roll(u, 1, 0).reshape(bb, tt, tf))
        u2 = pltpu.roll(u, 2, 0).reshape(bb, tt, tf)
        u2 = jnp.where(tpos == 0, prev[:, 0:1, :], jnp.where(tpos == 1, prev[:, 1:2, :], u2))
        conv = cb_ref[:, fs] + cw_ref[0:1, fs] * u2 + cw_ref[1:2, fs] * u1 + cw_ref[2:3, fs] * u3
        acts.append((_silu(conv) * gate.reshape(bb, tt, tf)).reshape(bb * tt, tf).astype(BF16))
        tails.append(u3[:, tt - 2:tt, :])
        if len(acts) == FF_GROUP or f == nf - 1:
            lo_f = f + 1 - len(acts)
            acc[...] += jnp.dot(jnp.concatenate(acts, axis=1) if len(acts) > 1 else acts[0],
                                wd_ref[lo_f * tf:(f + 1) * tf, :], preferred_element_type=F32)
            acts = []
    tail = jnp.concatenate(tails, axis=-1)
    carry[...] = tail
    st_ref[...] = tail
    o_ref[...] = acc[...].reshape(bb, tt, d)


def _ffn(x3, gain, w_in, conv_w, conv_b, w_down, state_all, layer, bb, tt, tf=256):
    batch, t_len, d = x3.shape
    assert t_len >= CONV_W - 1 and tt >= CONV_W - 1 and D_FF % tf == 0
    const = lambda arr: pl.BlockSpec(arr.shape, lambda b, t: (0, 0), pipeline_mode=pl.Buffered(1))
    return pl.pallas_call(
        functools.partial(_ffn_body, bb=bb, tt=tt, tf=tf),
        grid=(batch // bb, t_len // tt),
        in_specs=[pl.BlockSpec((bb, tt, d), lambda b, t: (b, t, 0)),
                  const(gain), const(w_in), const(conv_w), const(conv_b), const(w_down),
                  pl.BlockSpec((None, bb, CONV_W - 1, D_FF), lambda b, t: (layer, b, 0, 0))],
        out_specs=[pl.BlockSpec((bb, tt, d), lambda b, t: (b, t, 0)),
                   pl.BlockSpec((bb, CONV_W - 1, D_FF), lambda b, t: (b, 0, 0))],
        out_shape=[jax.ShapeDtypeStruct((batch, t_len, d), F32),
                   jax.ShapeDtypeStruct((batch, CONV_W - 1, D_FF), F32)],
        scratch_shapes=[pltpu.VMEM((bb * tt, d), F32), pltpu.VMEM((bb, CONV_W - 1, D_FF), F32)],
        compiler_params=_params(2),
        name="conv_ffn",
    )(x3, gain, w_in, conv_w, conv_b, w_down, state_all)


def _trunk(x3, mem_kv, a_bufs, hg_s, rw_s, rw_shift, ffn_buf, w, prompt):
    batch, t_len, d = x3.shape
    m = batch * t_len
    n_a = w['attn_w_qkv'].shape[0]
    new_a = None
    new_hg, new_rw, new_sh, new_ffn = [], [], [], []
    if prompt:
        xa_bb, xa_tt = 1, 512
        ff_bb, ff_tt = 1, 512
        rw_bb, rw_tt = 1, 512
    else:
        xa_bb, xa_tt = 4, t_len
        ff_bb, ff_tt = batch, t_len
        rw_bb, rw_tt = batch, t_len
    for i in range(DEPTH):
        kind, j = i % N_MIXERS, i // N_MIXERS
        x2 = x3.reshape(m, d)
        g_mix = w['norm_mix'][i][None]
        if kind == 0:
            qkv = _norm_matmul(x2, g_mix, w['attn_w_qkv'][j], w['attn_head_gain'][j], hn_width=A_HEAD_DIM,
                               hn_tiles=(True,) * (2 * N_GROUPS) + (False,) * N_GROUPS, tn=A_WIDTH)
            if prompt:
                o = _attn_prompt(qkv, batch, t_len)
                new_a = tuple(_kv_cache_out(qkv, None if new_a is None else new_a[gi], gi, j, n_a, batch, t_len)
                              for gi in range(N_GROUPS))
            else:
                o, c0, c1, c2 = _attn_sample(qkv, a_bufs, new_a, j, batch, t_len)
                new_a = (c0, c1, c2)
            w_mix = w['attn_w_o'][j]
        elif kind == 1:
            proj = _norm_matmul(x2, g_mix, w['hg_w_in'][j])
            o, st = _hgrn(proj, w['hg_lb_logits'], w['hg_out_gain'][j][None], hg_s[j], i, batch, t_len)
            new_hg.append(st)
            w_mix = w['hg_w_o'][j]
        else:
            proj = _rwkv_proj(x3, w['norm_mix'][i][None], rw_shift[j][:, None, :], w['rw_mu'][j], w['rw_w_rkv'][j],
                              w['rw_w0'][j][None], w['rw_w1'][j], w['rw_w2'][j], w['rw_a0'][j][None],
                              w['rw_a1'][j], w['rw_a2'][j], w['rw_g1'][j], w['rw_g2'][j],
                              w['rw_k_k'][j][None], w['rw_k_a'][j][None], rw_bb, rw_tt)
            o, st = _rwkv_rec(proj[:7], w['rw_r_k'][j].reshape(1, d), w['rw_ln_g'][j][None], w['rw_ln_b'][j][None],
                              rw_s[j], batch, t_len)
            new_rw.append(st)
            new_sh.append(proj[7][:, 0, :])
            w_mix = w['rw_w_o'][j]
        x3 = _xattn(x3, o, w_mix, w['norm_mem'][i][None], w['xa_w_q'][i], w['xa_q_gain'][i][None], mem_kv, i,
                    w['xa_w_o'][i], xa_bb, xa_tt)
        x3, fb = _ffn(x3, w['norm_ffn'][i][None], w['ffn_w_in'][i], w['ffn_conv_w'][i], w['ffn_conv_b'][i][None],
                      w['ffn_w_down'][i], ffn_buf, i, ff_bb, ff_tt)
        new_ffn.append(fb)
    return (x3, new_a, jnp.stack(new_hg), jnp.stack(new_rw), jnp.stack(new_sh), jnp.stack(new_ffn))


def kernel(x_prompt, x_sample, mem_prompt, cache_attn_kv_w128, cache_attn_kv_w512, cache_attn_kv_w2048, state_hgrn, state_rwkv, state_rwkv_shift, state_ffn_conv, cache_mem_kv, norm_mix, norm_mem, norm_ffn, mem_norm, attn_w_qkv, attn_q_gain, attn_k_gain, attn_w_o, hg_w_in, hg_lb_logits, hg_out_gain, hg_w_o, rw_mu, rw_w_rkv, rw_w0, rw_w1, rw_w2, rw_a0, rw_a1, rw_a2, rw_g1, rw_g2, rw_k_k, rw_k_a, rw_r_k, rw_ln_g, rw_ln_b, rw_w_o, xa_w_q, xa_w_kv, xa_q_gain, xa_k_gain, xa_w_o, ffn_w_in, ffn_conv_w, ffn_conv_b, ffn_w_down):
    d = D_MODEL
    bp = x_prompt.shape[0]
    bs = x_sample.shape[0]
    n_a, n_b, n_c = attn_w_qkv.shape[0], hg_w_in.shape[0], rw_w_rkv.shape[0]
    bf = lambda a: a.astype(BF16)

    tile_h = lambda gn: jnp.broadcast_to(gn[:, :, None, :], (n_a, N_GROUPS, A_HEADS, A_HEAD_DIM)).reshape(n_a, 1, -1)
    head_gain = jnp.concatenate([tile_h(attn_q_gain), tile_h(attn_k_gain),
                                 jnp.ones((n_a, 1, N_GROUPS * A_WIDTH), F32)], axis=-1)

    w = {
        'norm_mix': norm_mix, 'norm_mem': norm_mem, 'norm_ffn': norm_ffn,
        'attn_w_qkv': bf(attn_w_qkv), 'attn_head_gain': head_gain, 'attn_w_o': bf(attn_w_o),
        'hg_w_in': bf(hg_w_in), 'hg_lb_logits': hg_lb_logits, 'hg_out_gain': hg_out_gain, 'hg_w_o': bf(hg_w_o),
        'rw_mu': rw_mu, 'rw_w_rkv': bf(rw_w_rkv), 'rw_w0': rw_w0, 'rw_w1': bf(rw_w1), 'rw_w2': bf(rw_w2),
        'rw_a0': rw_a0, 'rw_a1': bf(rw_a1), 'rw_a2': bf(rw_a2), 'rw_g1': bf(rw_g1), 'rw_g2': bf(rw_g2),
        'rw_k_k': rw_k_k, 'rw_k_a': rw_k_a, 'rw_r_k': rw_r_k, 'rw_ln_g': rw_ln_g, 'rw_ln_b': rw_ln_b,
        'rw_w_o': bf(rw_w_o),
        'xa_w_q': bf(xa_w_q), 'xa_q_gain': xa_q_gain, 'xa_w_o': bf(xa_w_o),
        'ffn_w_in': bf(ffn_w_in), 'ffn_conv_w': ffn_conv_w, 'ffn_conv_b': ffn_conv_b, 'ffn_w_down': bf(ffn_w_down),
    }

    mem_kv_prompt, mem_kv_flat = _memory_kv(mem_prompt.reshape(bp * N_MEM, d), mem_norm[:, None, :], bf(xa_w_kv),
                                            xa_k_gain[:, None, :], bp)

    as_rows = lambda cc: cc.reshape(cc.shape[0], cc.shape[1], cc.shape[2] * cc.shape[3] * cc.shape[4], cc.shape[5])
    as_cache = lambda rr: rr.reshape(rr.shape[0], rr.shape[1], rr.shape[2] // KV_ROWS, 2, A_HEADS, A_HEAD_DIM)
    y_prompt, a_p, hg_p, rw_p, sh_p, ffn_p = _trunk(
        x_prompt, mem_kv_flat, None,
        jnp.zeros((n_b, bp, HG_HEADS, HG_DK, HG_DK), F32),
        jnp.zeros((n_c, bp, RW_HEADS, RW_HEAD_DIM, RW_HEAD_DIM), F32),
        jnp.zeros((n_c, bp, d), F32),
        jnp.zeros((DEPTH, bp, CONV_W - 1, D_FF), F32), w, True)
    y_sample, a_s, hg_s, rw_s, sh_s, ffn_s = _trunk(
        x_sample, cache_mem_kv,
        tuple(as_rows(cc) for cc in (cache_attn_kv_w128, cache_attn_kv_w512, cache_attn_kv_w2048)),
        state_hgrn, state_rwkv, state_rwkv_shift, state_ffn_conv, w, False)
    a_p = tuple(as_cache(rr) for rr in a_p)
    a_s = tuple(as_cache(rr) for rr in a_s)
    return (y_prompt, y_sample, a_p[0], a_p[1], a_p[2], hg_p, rw_p, sh_p, ffn_p, mem_kv_prompt,
            a_s[0], a_s[1], a_s[2], hg_s, rw_s, sh_s, ffn_s)
```

```python
import functools
import math

import numpy as np
import jax
import jax.numpy as jnp
from jax import lax
from jax.experimental import pallas as pl
from jax.experimental.pallas import tpu as pltpu

F32 = jnp.float32
BF16 = jnp.bfloat16

D_MODEL = 1024
DEPTH = 4
N_MIXERS = 3
DIL_WINDOWS = (128, 512, 2048)
DIL_RATES = (1, 4, 16)
N_GROUPS = 3
A_SPAN = 128
A_HEADS = 4
A_HEAD_DIM = 128
A_BLOCK = 128
A_UNROLL = 8
A_WIDTH = A_HEADS * A_HEAD_DIM
KV_ROWS = 2 * A_HEADS
HG_HEADS = 8
HG_DK = 128
HG_CHUNK = 128
HG_SHORT_CHUNK = 16
RW_HEADS = 16
RW_HEAD_DIM = 64
RW_CHUNK = 64
RW_GN_EPS = 64e-5
N_MEM = 256
XA_HEADS = 4
XA_HEAD_DIM = D_MODEL // XA_HEADS
D_FF = 2816
CONV_W = 3
FF_GROUP = 6
RMS_EPS = 1e-6
LANES = 128
VMEM_LIMIT = 52 * 1024 * 1024

NT_DIMS = (((1,), (1,)), ((), ()))
TN_DIMS = (((0,), (0,)), ((), ()))


def _params(n_axes):
    return pltpu.CompilerParams(dimension_semantics=("arbitrary",) * n_axes, vmem_limit_bytes=VMEM_LIMIT)


def _rms(x):
    return x * lax.rsqrt(jnp.mean(x * x, axis=-1, keepdims=True) + RMS_EPS)


def _dot(a, b):
    return jnp.dot(a.astype(BF16), b.astype(BF16), preferred_element_type=F32)


def _dot_nt(a, b):
    return lax.dot_general(a.astype(BF16), b.astype(BF16), NT_DIMS, preferred_element_type=F32)


def _dot_tn(a, b):
    return lax.dot_general(a.astype(BF16), b.astype(BF16), TN_DIMS, preferred_element_type=F32)


def _split3(x):
    p1 = x.astype(BF16)
    r1 = x - p1.astype(F32)
    p2 = r1.astype(BF16)
    p3 = (r1 - p2.astype(F32)).astype(BF16)
    return jnp.concatenate([p1, p2, p3], axis=1)


def _exact_rowmix(mat_bf16, x):
    w = x.shape[1]
    y = jnp.dot(mat_bf16, _split3(x), preferred_element_type=F32)
    return y[:, :w] + y[:, w:2 * w] + y[:, 2 * w:]


def _softplus(x):
    return jnp.maximum(x, 0.0) + jnp.log1p(jnp.exp(-jnp.abs(x)))


def _sigmoid(x):
    return 1.0 / (1.0 + jnp.exp(-x))


def _silu(x):
    return x * _sigmoid(x)


def _nmm_body(x_ref, g_ref, w_ref, hg_ref, o_ref, *, hn_width, hn_tiles, tn):
    h = (_rms(x_ref[...]) * g_ref[...]).astype(BF16)
    for c in range(w_ref.shape[1] // tn):
        y = jnp.dot(h, w_ref[:, c * tn:(c + 1) * tn], preferred_element_type=F32)
        if hn_width is not None and hn_tiles[c]:
            for cc in range(tn // hn_width):
                sl = slice(c * tn + cc * hn_width, c * tn + (cc + 1) * hn_width)
                o_ref[:, sl] = _rms(y[:, cc * hn_width:(cc + 1) * hn_width]) * hg_ref[:, sl]
        else:
            o_ref[:, c * tn:(c + 1) * tn] = y


def _norm_matmul(x, gain, w, head_gain=None, hn_width=None, hn_tiles=None, tm=256, tn=512):
    m, d = x.shape
    n = w.shape[1]
    tm = min(tm, m)
    if head_gain is None:
        head_gain = jnp.ones((1, n), F32)
    return pl.pallas_call(
        functools.partial(_nmm_body, hn_width=hn_width, hn_tiles=hn_tiles, tn=tn),
        grid=(m // tm,),
        in_specs=[
            pl.BlockSpec((tm, d), lambda i: (i, 0)),
            pl.BlockSpec((1, d), lambda i: (0, 0)),
            pl.BlockSpec((d, n), lambda i: (0, 0)),
            pl.BlockSpec((1, n), lambda i: (0, 0)),
        ],
        out_specs=pl.BlockSpec((tm, n), lambda i: (i, 0)),
        out_shape=jax.ShapeDtypeStruct((m, n), F32),
        compiler_params=_params(1),
        name="norm_matmul",
    )(x, gain, w, head_gain)


def _ds(start, size, stride):
    return pl.ds(start, size) if stride == 1 else pl.ds(start, size, stride=stride)


def _attn_prompt_body(q0, q1, q2, k0, v0, k1, v1, k2, v2, o_ref, acc, m_s, l_s, *, seq):
    scale = A_HEAD_DIM ** -0.5
    blk = A_BLOCK
    for g, (q_ref, k_ref, v_ref) in enumerate(((q0, k0, v0), (q1, k1, v1), (q2, k2, v2))):
        dil = DIL_RATES[g]
        n = seq // dil
        nb = n // blk
        width = 2 * blk if nb > 1 else blk
        ii = lax.broadcasted_iota(jnp.int32, (blk, width), 0)
        jj = lax.broadcasted_iota(jnp.int32, (blk, width), 1)
        if nb > 1:
            in_cur = (jj >= blk) & (jj - blk <= ii)
            in_prev = (jj < blk) & (jj >= ii)
        else:
            in_cur, in_prev = jj <= ii, None

        def body(it, carry, g=g, dil=dil, nb=nb, q_ref=q_ref, k_ref=k_ref, v_ref=v_ref, in_cur=in_cur,
                 in_prev=in_prev):
            rows, ss, vws = [], [], []
            for uu in range(A_UNROLL):
                idx = it * A_UNROLL + uu
                r = idx // nb
                mb = idx % nb
                rw = _ds(r + dil * blk * mb, blk, dil)
                kc = k_ref[rw, :].astype(BF16)
                vc = v_ref[rw, :].astype(BF16)
                if nb > 1:
                    prev = _ds(r + dil * blk * jnp.maximum(mb - 1, 0), blk, dil)
                    kw = jnp.concatenate([k_ref[prev, :].astype(BF16), kc], axis=0)
                    vw = jnp.concatenate([v_ref[prev, :].astype(BF16), vc], axis=0)
                    prev_bias = jnp.where(mb > 0, 0.0, -jnp.inf)
                    bias = jnp.where(in_cur, 0.0, jnp.where(in_prev, prev_bias, -jnp.inf))
                else:
                    kw, vw = kc, vc
                    bias = jnp.where(in_cur, 0.0, -jnp.inf)
                rows.append(rw)
                vws.append(jnp.concatenate([vw, jnp.ones((vw.shape[0], LANES), BF16)], axis=1))
                ss.append(_dot_nt(q_ref[rw, :], kw) * scale + bias)
            ms = [jnp.max(s, axis=-1, keepdims=True) for s in ss]
            ps = [jnp.exp(s - m) for s, m in zip(ss, ms)]
            ols = [_dot(p, vw) for p, vw in zip(ps, vws)]
            os_ = [x[:, 0:A_HEAD_DIM] for x in ols]
            l_bs = [x[:, A_HEAD_DIM:A_HEAD_DIM + LANES] for x in ols]
            m_bs = [jnp.broadcast_to(m, (blk, LANES)) for m in ms]
            if g == 0:
                for rw, o, m_b, l_b in zip(rows, os_, m_bs, l_bs):
                    acc[rw, :] = o
                    m_s[rw, :] = m_b
                    l_s[rw, :] = l_b
            else:
                olds = [(m_s[rw, :], l_s[rw, :], acc[rw, :]) for rw in rows]
                for rw, o, m_b, l_b, (m_old, l_old, acc_old) in zip(rows, os_, m_bs, l_bs, olds):
                    m_new = jnp.maximum(m_old, m_b)
                    a_old = jnp.exp(m_old - m_new)
                    a_cur = jnp.exp(m_b - m_new)
                    acc[rw, :] = acc_old * a_old + o * a_cur
                    l_s[rw, :] = l_old * a_old + l_b * a_cur
                    m_s[rw, :] = m_new
            return carry

        assert (dil * nb) % A_UNROLL == 0
        lax.fori_loop(0, dil * nb // A_UNROLL, body, 0)
    o_ref[...] = acc[...] / l_s[...]


def _attn_prompt(qkv, batch, seq):
    m = batch * seq
    assert seq % (A_BLOCK * max(DIL_RATES)) == 0 and A_BLOCK == A_SPAN
    nq = N_GROUPS * A_HEADS

    def spec(col_fn):
        return pl.BlockSpec((seq, A_HEAD_DIM), lambda b, h: (b, col_fn(h)))

    in_specs = [spec(lambda h, g=g: g * A_HEADS + h) for g in range(N_GROUPS)]
    for g in range(N_GROUPS):
        in_specs.append(spec(lambda h, g=g: nq + g * A_HEADS + h))
        in_specs.append(spec(lambda h, g=g: 2 * nq + g * A_HEADS + h))
    return pl.pallas_call(
        functools.partial(_attn_prompt_body, seq=seq),
        grid=(batch, A_HEADS),
        in_specs=in_specs,
        out_specs=pl.BlockSpec((seq, A_HEAD_DIM), lambda b, h: (b, h)),
        out_shape=jax.ShapeDtypeStruct((m, A_WIDTH), F32),
        scratch_shapes=[pltpu.VMEM((seq, A_HEAD_DIM), F32), pltpu.VMEM((seq, LANES), F32),
                        pltpu.VMEM((seq, LANES), F32)],
        compiler_params=_params(2),
        name="dilated_attn_prompt",
    )(*([qkv] * 9))


def _attn_sample_body(*refs, t_new, n_alias):
    qkv_ref, bufs = refs[0], refs[1:4]
    o_ref, outs = refs[4 + n_alias], refs[5 + n_alias:8 + n_alias]
    scale = A_HEAD_DIM ** -0.5
    pad = jnp.zeros((LANES - t_new, A_HEAD_DIM), F32)
    nq = N_GROUPS * A_WIDTH

    def new_rows(g, kv, h):
        base = (1 + kv) * nq + g * A_WIDTH + h * A_HEAD_DIM
        return qkv_ref[:, base:base + A_HEAD_DIM]

    for g in range(N_GROUPS):
        length = bufs[g].shape[0] // KV_ROWS
        keep = (length - t_new) * KV_ROWS
        outs[g][0:keep, :] = bufs[g][t_new * KV_ROWS:length * KV_ROWS, :]
        for kv in range(2):
            for h in range(A_HEADS):
                outs[g][pl.ds(keep + kv * A_HEADS + h, t_new, stride=KV_ROWS), :] = new_rows(g, kv, h)

    for h in range(A_HEADS):
        parts, vals = [], []
        for g in range(N_GROUPS):
            dil = DIL_RATES[g]
            length = bufs[g].shape[0] // KV_ROWS
            q = qkv_ref[:, g * A_WIDTH + h * A_HEAD_DIM: g * A_WIDTH + (h + 1) * A_HEAD_DIM]
            s_buf = _dot_nt(q, bufs[g][pl.ds(h, length, stride=KV_ROWS), :]) * scale
            s_new = _dot_nt(q, jnp.concatenate([new_rows(g, 0, h), pad], axis=0)) * scale
            for s, base in ((s_buf, 0), (s_new, length)):
                ii = lax.broadcasted_iota(jnp.int32, s.shape, 0)
                jj = lax.broadcasted_iota(jnp.int32, s.shape, 1) + base
                dd = jj - ii
                valid = (dd >= 0) & (dd <= A_SPAN * dil) & ((dd & (dil - 1)) == 0) & (jj < length + t_new)
                parts.append(jnp.where(valid, s, -jnp.inf))
            vals.append(bufs[g][pl.ds(A_HEADS + h, length, stride=KV_ROWS), :])
            vals.append(jnp.concatenate([new_rows(g, 1, h), pad], axis=0))
        mx = functools.reduce(jnp.maximum, [jnp.max(s, axis=-1, keepdims=True) for s in parts])
        exps = [jnp.exp(s - mx) for s in parts]
        den = functools.reduce(lambda a, b: a + b, [jnp.sum(e, axis=-1, keepdims=True) for e in exps])
        o = functools.reduce(lambda a, b: a + b, [_dot(e / den, vv) for e, vv in zip(exps, vals)])
        o_ref[:, h * A_HEAD_DIM:(h + 1) * A_HEAD_DIM] = o


def _attn_sample(qkv, caches, prev_out, j, batch, t_new):
    assert t_new % 8 == 0 and t_new <= LANES
    lens = [cch.shape[2] // KV_ROWS for cch in caches]
    for g in range(N_GROUPS):
        assert lens[g] == A_SPAN * DIL_RATES[g] and (DIL_RATES[g] & (DIL_RATES[g] - 1)) == 0

    def cache_spec(g):
        return pl.BlockSpec((None, None, lens[g] * KV_ROWS, A_HEAD_DIM), lambda b: (j, b, 0, 0))

    in_specs = [pl.BlockSpec((t_new, qkv.shape[1]), lambda b: (b, 0))] + [cache_spec(g) for g in range(N_GROUPS)]
    n_alias = 0 if prev_out is None else N_GROUPS
    aliases = {}
    if prev_out is not None:
        in_specs += [pl.BlockSpec(memory_space=pl.ANY)] * N_GROUPS
        aliases = {4 + g: 1 + g for g in range(N_GROUPS)}
    out_specs = [pl.BlockSpec((t_new, A_WIDTH), lambda b: (b, 0))] + [cache_spec(g) for g in range(N_GROUPS)]
    out_shape = [jax.ShapeDtypeStruct((batch * t_new, A_WIDTH), F32)]
    out_shape += [jax.ShapeDtypeStruct(cch.shape, F32) for cch in caches]
    return pl.pallas_call(
        functools.partial(_attn_sample_body, t_new=t_new, n_alias=n_alias),
        grid=(batch,),
        in_specs=in_specs,
        out_specs=out_specs,
        out_shape=out_shape,
        input_output_aliases=aliases,
        compiler_params=_params(1),
        name="dilated_attn_sample",
    )(qkv, *caches, *(prev_out or ()))


def _kv_cache_body(*refs):
    k_ref, v_ref, o_ref = refs[0], refs[1], refs[-1]
    tr = k_ref.shape[0]
    for kv, x_ref in enumerate((k_ref, v_ref)):
        for h in range(A_HEADS):
            o_ref[pl.ds(kv * A_HEADS + h, tr, stride=KV_ROWS), :] = x_ref[:, h * A_HEAD_DIM:(h + 1) * A_HEAD_DIM]


def _kv_cache_out(qkv, prev_out, g, j, n_layers, batch, seq):
    win = min(DIL_WINDOWS[g], seq)
    tr = min(win, 512)
    first = (seq - win) // tr
    row = lambda b, t: b * (seq // tr) + first + t
    in_specs = [pl.BlockSpec((tr, A_WIDTH), lambda b, t: (row(b, t), N_GROUPS + g)),
                pl.BlockSpec((tr, A_WIDTH), lambda b, t: (row(b, t), 2 * N_GROUPS + g))]
    aliases = {}
    if prev_out is not None:
        in_specs.append(pl.BlockSpec(memory_space=pl.ANY))
        aliases = {2: 0}
    return pl.pallas_call(
        _kv_cache_body,
        grid=(batch, win // tr),
        in_specs=in_specs,
        out_specs=pl.BlockSpec((None, None, tr * KV_ROWS, A_HEAD_DIM), lambda b, t: (j, b, t, 0)),
        out_shape=jax.ShapeDtypeStruct((n_layers, batch, win * KV_ROWS, A_HEAD_DIM), F32),
        input_output_aliases=aliases,
        compiler_params=_params(2),
        name="kv_cache_out",
    )(qkv, qkv, *(() if prev_out is None else (prev_out,)))


def _hgrn_consts(c):
    t = np.arange(c)[:, None]
    u = np.arange(c)[None, :]
    masks = [(u == t)]
    s = 1
    while s < c:
        upper = (t // s) % 2 == 1
        lower_u = (u // s) % 2 == 0
        masks.append(upper & lower_u & (t // (2 * s) == u // (2 * s)))
        s *= 2
    masks = np.stack([m.astype(np.float32) for m in masks], axis=0)
    return jnp.asarray((u <= t).astype(np.float32), BF16), jnp.asarray(masks, F32)


def _hgrn_level_exponent(b, s, row):
    c, width = b.shape
    if s >= 8:
        mid = jnp.concatenate([jnp.broadcast_to(b[j * 2 * s + s - 1:j * 2 * s + s, :], (2 * s, width))
                               for j in range(c // (2 * s))], axis=0)
    else:
        b3 = b.reshape(c // 8, 8, width)
        sub = lax.broadcasted_iota(jnp.int32, b3.shape, 1)
        pick = lambda k: jnp.broadcast_to(b3[:, k:k + 1, :], b3.shape)
        mid3 = pick(s - 1)
        for k in range(2 * s + s - 1, 8, 2 * s):
            mid3 = jnp.where(sub >= k - (s - 1), pick(k), mid3)
        mid = mid3.reshape(c, width)
    return jnp.where((row & s) != 0, b - mid, mid - b)


def _hgrn_body(q_ref, z_ref, i_ref, g_ref, lbl_ref, og_ref, s0_ref, mat_ref, msk_ref, o_ref, st_ref, st_scr,
               *, layer, t_len, c, hp):
    n_lvl = msk_ref.shape[0] - 1
    width = hp * HG_DK
    hr = range(hp)
    lane = lambda x, hd: x[:, hd * HG_DK:(hd + 1) * HG_DK]
    lg = lbl_ref[...]
    e = jnp.exp(lg - jnp.max(lg, axis=0, keepdims=True))
    prob = e / jnp.sum(e, axis=0, keepdims=True)
    lb = jnp.sum(prob[0:layer + 1], axis=0, keepdims=True) - prob[0:1]
    log_lb = jnp.log(lb)
    log_1m = jnp.log1p(-lb)
    @pl.when(pl.program_id(2) == 0)
    def _():
        for hd in hr:
            st_scr[hd] = s0_ref[hd].T

    row = lax.broadcasted_iota(jnp.int32, (c, width), 0)
    out_gain = jnp.concatenate([og_ref[...]] * hp, axis=1)

    def chunk(ci, carry):
        if t_len >= c:
            rows = pl.ds(pl.multiple_of(ci * c, c), c)
            ld = lambda ref: ref[rows, :]
            live = None
        else:
            padz = jnp.zeros((c - t_len, width), F32)
            ld = lambda ref: jnp.concatenate([ref[...], padz], axis=0)
            live = row < t_len
        z = ld(z_ref)
        lsig = jnp.minimum(z, 0.0) - jnp.log1p(jnp.exp(-jnp.abs(z)))
        a1 = jnp.broadcast_to(log_lb, z.shape)
        a2 = log_1m + lsig
        hi = jnp.maximum(a1, a2)
        lo = jnp.minimum(a1, a2)
        log_f = hi + jnp.log1p(jnp.exp(lo - hi))
        kg = (1.0 - lb) * _sigmoid(-z)
        if live is not None:
            log_f = jnp.where(live, log_f, 0.0)
            kg = jnp.where(live, kg, 0.0)
        qa = _silu(ld(q_ref))
        vv = ld(i_ref)
        b = _exact_rowmix(mat_ref[...], log_f)
        sts = [st_scr[hd] for hd in hr]
        a = [_dot_nt(lane(qa, hd), lane(kg, hd)) * msk_ref[0] for hd in hr]
        for lv in range(n_lvl):
            wv = jnp.exp(_hgrn_level_exponent(b, 1 << lv, row))
            qw = qa * wv
            kw = kg * wv
            a = [a[hd] + _dot_nt(lane(qw, hd), lane(kw, hd)) * msk_ref[1 + lv] for hd in hr]
        qb = qa * jnp.exp(b)
        kd = kg * jnp.exp(b[c - 1:c] - b)
        dec = jnp.exp(b[c - 1:c])
        o = [_dot_nt(lane(qb, hd), sts[hd]) + _dot(a[hd], lane(vv, hd)) for hd in hr]
        st_scr[...] = jnp.stack([sts[hd] * lane(dec, hd) + _dot_tn(lane(vv, hd), lane(kd, hd)) for hd in hr], axis=0)
        y = jnp.concatenate([_rms(x) for x in o], axis=1) * out_gain * _silu(ld(g_ref))
        if t_len >= c:
            o_ref[rows, :] = y
        else:
            o_ref[...] = y[0:t_len]
        return carry

    lax.fori_loop(0, max(t_len // c, 1), chunk, 0)
    for hd in hr:
        st_ref[hd] = st_scr[hd].T


def _hgrn(proj, lb_logits, out_gain, s0, layer, batch, t_len, tt=512):
    tt = min(tt, t_len)
    if tt >= HG_CHUNK:
        c, hp = HG_CHUNK, 8
        assert tt % c == 0 and t_len % tt == 0
    else:
        c, hp = HG_SHORT_CHUNK, HG_HEADS
        assert tt % 8 == 0 and tt <= c
    mats, masks = _hgrn_consts(c)
    width = hp * HG_DK
    ng = HG_HEADS // hp
    nt = t_len // tt

    def col(part):
        return pl.BlockSpec((tt, width), lambda b, h, t: (b * nt + t, part * ng + h))

    st_spec = pl.BlockSpec((None, hp, HG_DK, HG_DK), lambda b, h, t: (b, h, 0, 0))
    return pl.pallas_call(
        functools.partial(_hgrn_body, layer=layer, t_len=tt, c=c, hp=hp),
        grid=(batch, ng, nt),
        in_specs=[col(0), col(1), col(2), col(3),
                  pl.BlockSpec((DEPTH, width), lambda b, h, t: (0, h)),
                  pl.BlockSpec((1, HG_DK), lambda b, h, t: (0, 0)),
                  st_spec,
                  pl.BlockSpec(mats.shape, lambda b, h, t: (0, 0)),
                  pl.BlockSpec(masks.shape, lambda b, h, t: (0, 0, 0))],
        out_specs=[pl.BlockSpec((tt, width), lambda b, h, t: (b * nt + t, h)), st_spec],
        out_shape=[jax.ShapeDtypeStruct((batch * t_len, HG_HEADS * HG_DK), F32),
                   jax.ShapeDtypeStruct((batch, HG_HEADS, HG_DK, HG_DK), F32)],
        scratch_shapes=[pltpu.VMEM((hp, HG_DK, HG_DK), F32)],
        compiler_params=_params(3),
        name="hgrn2_chunked",
    )(proj, proj, proj, proj, lb_logits, out_gain, s0, mats, masks)


def _rwkv_proj_body(x_ref, g_ref, sh_ref, mu_ref, wrkv_ref, w0_ref, w1_ref, w2_ref, a0_ref, a1_ref, a2_ref,
                    g1_ref, g2_ref, kk_ref, ka_ref,
                    r_o, k_o, v_o, ld_o, kk_o, a_o, g_o, sh_o, carry, *, bb, tt):
    t = pl.program_id(1)
    d = D_MODEL

    @pl.when(t == 0)
    def _():
        carry[...] = sh_ref[...]

    h3 = _rms(x_ref[...]) * g_ref[...]
    h = h3.reshape(bb * tt, d)
    rolled = pltpu.roll(h, 1, 0).reshape(bb, tt, d)
    tpos = lax.broadcasted_iota(jnp.int32, (bb, tt, d), 1)
    prev = jnp.where(tpos == 0, carry[...], rolled).reshape(bb * tt, d)
    last = h3[:, tt - 1:tt, :]
    carry[...] = last
    sh_o[...] = last
    dx = prev - h
    mix = lambda jm: h + dx * mu_ref[jm:jm + 1, :]
    r = _dot(mix(0), wrkv_ref[0])
    k = _dot(mix(1), wrkv_ref[1])
    v = _dot(mix(2), wrkv_ref[2])
    wl = w0_ref[...] + _dot(jnp.tanh(_dot(mix(3), w1_ref[...])), w2_ref[...])
    wlog = -_softplus(-wl) - 0.5
    a = _sigmoid(a0_ref[...] + _dot(_dot(mix(4), a1_ref[...]), a2_ref[...]))
    gate = _dot(_sigmoid(_dot(mix(5), g1_ref[...])), g2_ref[...])
    r_o[...] = r
    k_o[...] = k * (1.0 + (a - 1.0) * ka_ref[...])
    v_o[...] = v
    ld_o[...] = -jnp.exp(wlog)
    kk_o[...] = k * kk_ref[...]
    a_o[...] = a
    g_o[...] = gate


def _rwkv_proj(x3, gain, shift, mu, wrkv, w0, w1, w2, a0, a1, a2, g1, g2, k_k, k_a, bb, tt):
    batch, t_len, d = x3.shape
    m = batch * t_len
    rows = bb * tt
    full = lambda arr: pl.BlockSpec(arr.shape, lambda b, t: (0,) * arr.ndim)
    row_spec = pl.BlockSpec((rows, d), lambda b, t: (b * (t_len // tt) + t, 0))
    outs = pl.pallas_call(
        functools.partial(_rwkv_proj_body, bb=bb, tt=tt),
        grid=(batch // bb, t_len // tt),
        in_specs=[pl.BlockSpec((bb, tt, d), lambda b, t: (b, t, 0)), full(gain),
                  pl.BlockSpec((bb, 1, d), lambda b, t: (b, 0, 0)), full(mu), full(wrkv),
                  full(w0), full(w1), full(w2), full(a0), full(a1), full(a2), full(g1), full(g2),
                  full(k_k), full(k_a)],
        out_specs=[row_spec] * 7 + [pl.BlockSpec((bb, 1, d), lambda b, t: (b, 0, 0))],
        out_shape=[jax.ShapeDtypeStruct((m, d), F32)] * 7 + [jax.ShapeDtypeStruct((batch, 1, d), F32)],
        scratch_shapes=[pltpu.VMEM((bb, 1, d), F32)],
        compiler_params=_params(2),
        name="rwkv_proj",
    )(x3, gain, shift, mu, wrkv, w0, w1, w2, a0, a1, a2, g1, g2, k_k, k_a)
    return outs


def _rwkv_consts(c):
    n = RW_HEAD_DIM
    t = np.arange(c)[:, None]
    u = np.arange(c)[None, :]
    f = lambda mat, dt: jnp.asarray(np.asarray(mat).astype(np.float32), dt)
    tri = f(u <= t, BF16)
    half = np.concatenate([u < t, u <= t], axis=0)
    lo = f(np.concatenate([half, half], axis=1), F32)
    eye = f(u == t, F32)
    lane = np.arange(LANES)
    bdiag = f((lane[:, None] // n) == (lane[None, :] // n), F32)
    return tri, lo, eye, bdiag


def _rwkv_rec_body(r_ref, k_ref, v_ref, ld_ref, kk_ref, a_ref, g_ref, rk_ref, lng_ref, lnb_ref, s0_ref,
                   tri_ref, lo_ref, eye_ref, bd_ref, o_ref, st_ref, st_scr, *, tt):
    c = RW_CHUNK
    n = RW_HEAD_DIM
    heads = RW_HEADS
    pairs = heads // 2
    w = D_MODEL
    t = pl.program_id(1)

    @pl.when(t == 0)
    def _():
        zero = jnp.zeros((n, n), F32)
        for p in range(pairs):
            top = jnp.concatenate([s0_ref[2 * p], zero], axis=1)
            bot = jnp.concatenate([zero, s0_ref[2 * p + 1]], axis=1)
            st_scr[p] = jnp.concatenate([top, bot], axis=0)

    low_2c = lax.broadcasted_iota(jnp.int32, (2 * c, LANES), 1) < n
    low_c = lax.broadcasted_iota(jnp.int32, (c, LANES), 1) < n
    tile = lambda x, p: x[:, p * LANES:(p + 1) * LANES]
    hr = range(heads)
    pr = range(pairs)
    zeros_c = jnp.zeros((c, LANES), F32)

    def head_sums(x):
        out = []
        for p in pr:
            xt = tile(x, p)
            lo_s = jnp.sum(jnp.where(low_c, xt, 0.0), axis=-1, keepdims=True)
            hi_s = jnp.sum(jnp.where(low_c, 0.0, xt), axis=-1, keepdims=True)
            out.append(jnp.where(low_c, lo_s, hi_s))
        return jnp.concatenate(out, axis=1)

    def chunk(ci, carry):
        if tt >= c:
            rows = pl.ds(pl.multiple_of(ci * c, c), c)
            ld = lambda ref: ref[rows, :]
        else:
            padz = jnp.zeros((c - tt, w), F32)
            ld = lambda ref: jnp.concatenate([ref[...], padz], axis=0)
        r, kf, v, lc, kk, a, gate = (ld(ref) for ref in (r_ref, k_ref, v_ref, ld_ref, kk_ref, a_ref, g_ref))
        kk = kk / jnp.maximum(jnp.sqrt(head_sums(kk * kk)), 1e-12)
        bvec = kk * a
        cum = _exact_rowmix(tri_ref[...], lc)
        c_end = cum[c - 1:c]
        e_neg = jnp.exp(-cum)
        e_end = jnp.exp(c_end - cum)
        d_end = jnp.exp(c_end)
        kr = jnp.concatenate([kk * jnp.exp(cum - lc), r * jnp.exp(cum)], axis=0)
        bx = jnp.concatenate([bvec * e_neg, kf * e_neg], axis=0)
        bk = jnp.concatenate([bvec * e_end, kf * e_end], axis=0)
        lo = lo_ref[...]
        eye = eye_ref[...]
        krm = [jnp.where(low_2c if h % 2 == 0 else jnp.logical_not(low_2c), tile(kr, h // 2), 0.0) for h in hr]
        gm = [_dot_nt(krm[h], tile(bx, h // 2)) * lo for h in hr]
        nn = [x[0:c, 0:c] for x in gm]
        tm = [eye - x for x in nn]
        npow = [_dot(x, x) for x in nn]
        p2 = 2
        while p2 < c:
            tm = [x + _dot(x, y) for x, y in zip(tm, npow)]
            p2 *= 2
            if p2 < c:
                npow = [_dot(x, x) for x in npow]
        sts = [st_scr[p] for p in pr]
        ks = [_dot_nt(tile(kr, p), sts[p]) for p in pr]
        vz = [jnp.concatenate([zeros_c, tile(v, p)], axis=0) for p in pr]
        av = [jnp.where(low_2c, _dot(gm[2 * p], vz[p]), _dot(gm[2 * p + 1], vz[p])) for p in pr]
        rhs = [ks[p][0:c] + av[p][0:c] for p in pr]
        u = [jnp.where(low_c, -_dot(tm[2 * p], rhs[p]), -_dot(tm[2 * p + 1], rhs[p])) for p in pr]
        uz = [jnp.concatenate([u[p], zeros_c], axis=0) for p in pr]
        y = [ks[p][c:2 * c] + av[p][c:2 * c]
             + jnp.where(low_c, _dot(gm[2 * p][c:2 * c], uz[p]), _dot(gm[2 * p + 1][c:2 * c], uz[p])) for p in pr]
        new_st = [sts[p] * tile(d_end, p)
                  + _dot_tn(jnp.concatenate([u[p], tile(v, p)], axis=0), tile(bk, p)) * bd_ref[...] for p in pr]
        st_scr[...] = jnp.stack(new_st, axis=0)
        yy = jnp.concatenate(y, axis=1)
        yc = yy - head_sums(yy) * (1.0 / n)
        var = head_sums(yc * yc) * (1.0 / n)
        yn = yc * lax.rsqrt(var + RW_GN_EPS) * lng_ref[...] + lnb_ref[...]
        z = (yn + head_sums(r * kf * rk_ref[...]) * v) * gate
        if tt >= c:
            o_ref[rows, :] = z
        else:
            o_ref[...] = z[0:tt]
        return carry

    lax.fori_loop(0, max(tt // c, 1), chunk, 0)
    for p in range(pairs):
        blk = st_scr[p]
        st_ref[2 * p] = blk[0:n, 0:n]
        st_ref[2 * p + 1] = blk[n:2 * n, n:2 * n]


def _rwkv_rec(proj, r_k, ln_g, ln_b, s0, batch, t_len, tt=512):
    r, kf, v, ld, kk, a, gate = proj
    tt = min(tt, t_len)
    assert tt % RW_CHUNK == 0 or (tt < RW_CHUNK and tt % 8 == 0)
    assert RW_HEADS * RW_HEAD_DIM == D_MODEL and 2 * RW_HEAD_DIM == LANES and 2 * RW_CHUNK == LANES
    consts = _rwkv_consts(RW_CHUNK)
    nt = t_len // tt
    row_spec = pl.BlockSpec((tt, D_MODEL), lambda b, t: (b * nt + t, 0))
    par_spec = pl.BlockSpec((1, D_MODEL), lambda b, t: (0, 0))
    st_spec = pl.BlockSpec((None, RW_HEADS, RW_HEAD_DIM, RW_HEAD_DIM), lambda b, t: (b, 0, 0, 0))
    const = lambda arr: pl.BlockSpec(arr.shape, lambda b, t: (0, 0))
    return pl.pallas_call(
        functools.partial(_rwkv_rec_body, tt=tt),
        grid=(batch, nt),
        in_specs=[row_spec] * 7 + [par_spec] * 3 + [st_spec] + [const(x) for x in consts],
        out_specs=[row_spec, st_spec],
        out_shape=[jax.ShapeDtypeStruct((batch * t_len, D_MODEL), F32),
                   jax.ShapeDtypeStruct((batch, RW_HEADS, RW_HEAD_DIM, RW_HEAD_DIM), F32)],
        scratch_shapes=[pltpu.VMEM((RW_HEADS // 2, LANES, LANES), F32)],
        compiler_params=_params(2),
        name="rwkv7_chunked",
    )(r, kf, v, ld, kk, a, gate, r_k, ln_g, ln_b, s0, *consts)


def _xattn_body(x_ref, a_ref, wm_ref, g_ref, wq_ref, qg_ref, kv_ref, wo_ref, o_ref, *, bb, tt, native):
    d = D_MODEL
    e = XA_HEAD_DIM
    scale = XA_HEAD_DIM ** -0.5
    x = x_ref[...].reshape(bb * tt, d) + jnp.dot(a_ref[...].astype(BF16), wm_ref[...], preferred_element_type=F32)
    q = jnp.dot((_rms(x) * g_ref[...]).astype(BF16), wq_ref[...], preferred_element_type=F32)
    qn = lambda b, h: _rms(q[b * tt:(b + 1) * tt, h * e:(h + 1) * e]) * qg_ref[...]
    per_batch = []
    for b in range(bb):
        if native:
            rows = N_MEM * XA_HEADS
            s = _dot_nt(jnp.concatenate([qn(b, h) for h in range(XA_HEADS)], axis=0),
                        kv_ref[b, :, 0].reshape(rows, e)) * scale
            own = (lax.broadcasted_iota(jnp.int32, s.shape, 0) // tt
                   == lax.broadcasted_iota(jnp.int32, s.shape, 1) % XA_HEADS)
            s = jnp.where(own, s, -jnp.inf)
            p = jnp.exp(s - jnp.max(s, axis=-1, keepdims=True))
            p = p / jnp.sum(p, axis=-1, keepdims=True)
            oh = _dot(p, kv_ref[b, :, 1].reshape(rows, e))
            per_batch.append(jnp.concatenate([oh[h * tt:(h + 1) * tt] for h in range(XA_HEADS)], axis=-1))
            continue
        heads = []
        for h in range(XA_HEADS):
            s = _dot_nt(qn(b, h), kv_ref[b, :, h * e:(h + 1) * e]) * scale
            p = jnp.exp(s - jnp.max(s, axis=-1, keepdims=True))
            p = p / jnp.sum(p, axis=-1, keepdims=True)
            heads.append(_dot(p, kv_ref[b, :, d + h * e:d + (h + 1) * e]))
        per_batch.append(jnp.concatenate(heads, axis=-1))
    o = jnp.concatenate(per_batch, axis=0) if bb > 1 else per_batch[0]
    y = x + jnp.dot(o.astype(BF16), wo_ref[...], preferred_element_type=F32)
    o_ref[...] = y.reshape(bb, tt, d)


def _xattn(x3, mix_out, w_mix, gain, wq, q_gain, kv_all, layer, wo, bb, tt):
    batch, t_len, d = x3.shape
    km = mix_out.shape[1]
    nt = t_len // tt
    assert bb == 1 or nt == 1
    native = kv_all.ndim == 6
    if native:
        kv_spec = pl.BlockSpec((None, bb, N_MEM, 2, XA_HEADS, XA_HEAD_DIM), lambda b, t: (layer, b, 0, 0, 0, 0))
    else:
        kv_spec = pl.BlockSpec((None, bb, N_MEM, 2 * d), lambda b, t: (layer, b, 0, 0))
    return pl.pallas_call(
        functools.partial(_xattn_body, bb=bb, tt=tt, native=native),
        grid=(batch // bb, t_len // tt),
        in_specs=[pl.BlockSpec((bb, tt, d), lambda b, t: (b, t, 0)),
                  pl.BlockSpec((bb * tt, km), lambda b, t: (b * nt + t, 0)),
                  pl.BlockSpec((km, d), lambda b, t: (0, 0)),
                  pl.BlockSpec((1, d), lambda b, t: (0, 0)),
                  pl.BlockSpec((None, d, d), lambda b, t: (layer, 0, 0)),
                  pl.BlockSpec((1, XA_HEAD_DIM), lambda b, t: (0, 0)),
                  kv_spec,
                  pl.BlockSpec((None, d, d), lambda b, t: (layer, 0, 0))],
        out_specs=pl.BlockSpec((bb, tt, d), lambda b, t: (b, t, 0)),
        out_shape=jax.ShapeDtypeStruct((batch, t_len, d), F32),
        compiler_params=_params(2),
        name="memory_xattn",
    )(x3, mix_out, w_mix, gain, wq, q_gain, kv_all, wo)


def _memkv_body(x_ref, g_ref, w_ref, kg_ref, o_ref, flat_ref):
    c = pl.program_id(1)
    e = XA_HEAD_DIM
    h = (_rms(x_ref[...]) * g_ref[...]).astype(BF16)
    y = jnp.dot(h, w_ref[...], preferred_element_type=F32)

    @pl.when(c == 0)
    def _():
        for h in range(XA_HEADS):
            kn = _rms(y[:, h * e:(h + 1) * e]) * kg_ref[...]
            o_ref[:, h, :] = kn
            flat_ref[:, h * e:(h + 1) * e] = kn

    @pl.when(c == 1)
    def _():
        flat_ref[...] = y
        for h in range(XA_HEADS):
            o_ref[:, h, :] = y[:, h * e:(h + 1) * e]


def _memory_kv(mem2, mem_norm, w_kv, k_gain, batch):
    d = D_MODEL
    return pl.pallas_call(
        _memkv_body,
        grid=(DEPTH, 2, batch),
        in_specs=[pl.BlockSpec((N_MEM, d), lambda l, c, b: (b, 0)),
                  pl.BlockSpec((None, 1, d), lambda l, c, b: (l, 0, 0)),
                  pl.BlockSpec((None, d, d), lambda l, c, b: (l, 0, c)),
                  pl.BlockSpec((None, 1, XA_HEAD_DIM), lambda l, c, b: (l, 0, 0))],
        out_specs=[pl.BlockSpec((None, None, N_MEM, None, XA_HEADS, XA_HEAD_DIM), lambda l, c, b: (l, b, 0, c, 0, 0)),
                   pl.BlockSpec((None, None, N_MEM, d), lambda l, c, b: (l, b, 0, c))],
        out_shape=[jax.ShapeDtypeStruct((DEPTH, batch, N_MEM, 2, XA_HEADS, XA_HEAD_DIM), F32),
                   jax.ShapeDtypeStruct((DEPTH, batch, N_MEM, 2 * d), F32)],
        compiler_params=_params(3),
        name="memory_kv",
    )(mem2, mem_norm, w_kv, k_gain)


def _ffn_body(x_ref, g_ref, win_ref, cw_ref, cb_ref, wd_ref, pin_ref, o_ref, st_ref, acc, carry, *, bb, tt, tf):
    t = pl.program_id(1)
    d = D_MODEL

    @pl.when(t == 0)
    def _():
        carry[...] = pin_ref[...]

    x = x_ref[...].reshape(bb * tt, d)
    h = (_rms(x) * g_ref[...]).astype(BF16)
    acc[...] = x
    tpos = lax.broadcasted_iota(jnp.int32, (bb, tt, tf), 1)
    tails, acts = [], []
    nf = D_FF // tf
    for f in range(nf):
        fs = slice(f * tf, (f + 1) * tf)
        u = jnp.dot(h, win_ref[:, fs], preferred_element_type=F32)
        gate = jnp.dot(h, win_ref[:, D_FF + f * tf:D_FF + (f + 1) * tf], preferred_element_type=F32)
        prev = carry[:, :, fs]
        u3 = u.reshape(bb, tt, tf)
        u1 = jnp.where(tpos == 0, prev[:, 1:2, :], pltpu.roll(u, 1, 0).reshape(bb, tt, tf))
        u2 = pltpu.roll(u, 2, 0).reshape(bb, tt, tf)
        u2 = jnp.where(tpos == 0, prev[:, 0:1, :], jnp.where(tpos == 1, prev[:, 1:2, :], u2))
        conv = cb_ref[:, fs] + cw_ref[0:1, fs] * u2 + cw_ref[1:2, fs] * u1 + cw_ref[2:3, fs] * u3
        acts.append((_silu(conv) * gate.reshape(bb, tt, tf)).reshape(bb * tt, tf).astype(BF16))
        tails.append(u3[:, tt - 2:tt, :])
        if len(acts) == FF_GROUP or f == nf - 1:
            lo_f = f + 1 - len(acts)
            acc[...] += jnp.dot(jnp.concatenate(acts, axis=1) if len(acts) > 1 else acts[0],
                                wd_ref[lo_f * tf:(f + 1) * tf, :], preferred_element_type=F32)
            acts = []
    tail = jnp.concatenate(tails, axis=-1)
    carry[...] = tail
    st_ref[...] = tail
    o_ref[...] = acc[...].reshape(bb, tt, d)


def _ffn(x3, gain, w_in, conv_w, conv_b, w_down, state_all, layer, bb, tt, tf=256):
    batch, t_len, d = x3.shape
    assert t_len >= CONV_W - 1 and tt >= CONV_W - 1 and D_FF % tf == 0
    const = lambda arr: pl.BlockSpec(arr.shape, lambda b, t: (0, 0), pipeline_mode=pl.Buffered(1))
    layer_w = lambda arr: pl.BlockSpec((None,) + arr.shape[1:], lambda b, t: (layer, 0, 0),
                                       pipeline_mode=pl.Buffered(1))
    return pl.pallas_call(
        functools.partial(_ffn_body, bb=bb, tt=tt, tf=tf),
        grid=(batch // bb, t_len // tt),
        in_specs=[pl.BlockSpec((bb, tt, d), lambda b, t: (b, t, 0)),
                  const(gain), layer_w(w_in), const(conv_w), const(conv_b), layer_w(w_down),
                  pl.BlockSpec((None, bb, CONV_W - 1, D_FF), lambda b, t: (layer, b, 0, 0))],
        out_specs=[pl.BlockSpec((bb, tt, d), lambda b, t: (b, t, 0)),
                   pl.BlockSpec((bb, CONV_W - 1, D_FF), lambda b, t: (b, 0, 0))],
        out_shape=[jax.ShapeDtypeStruct((batch, t_len, d), F32),
                   jax.ShapeDtypeStruct((batch, CONV_W - 1, D_FF), F32)],
        scratch_shapes=[pltpu.VMEM((bb * tt, d), F32), pltpu.VMEM((bb, CONV_W - 1, D_FF), F32)],
        compiler_params=_params(2),
        name="conv_ffn",
    )(x3, gain, w_in, conv_w, conv_b, w_down, state_all)


def _trunk(x3, mem_kv, a_bufs, hg_s, rw_s, rw_shift, ffn_buf, w, prompt):
    batch, t_len, d = x3.shape
    m = batch * t_len
    n_a = w['attn_w_qkv'].shape[0]
    new_a = None
    new_hg, new_rw, new_sh, new_ffn = [], [], [], []
    if prompt:
        xa_bb, xa_tt = 1, 512
        ff_bb, ff_tt = 1, 512
        rw_bb, rw_tt = 1, 512
    else:
        xa_bb, xa_tt = 4, t_len
        ff_bb, ff_tt = batch, t_len
        rw_bb, rw_tt = batch, t_len
    for i in range(DEPTH):
        kind, j = i % N_MIXERS, i // N_MIXERS
        x2 = x3.reshape(m, d)
        g_mix = w['norm_mix'][i][None]
        if kind == 0:
            qkv = _norm_matmul(x2, g_mix, w['attn_w_qkv'][j], w['attn_head_gain'][j], hn_width=A_HEAD_DIM,
                               hn_tiles=(True,) * (2 * N_GROUPS) + (False,) * N_GROUPS, tn=A_WIDTH)
            if prompt:
                o = _attn_prompt(qkv, batch, t_len)
                new_a = tuple(_kv_cache_out(qkv, None if new_a is None else new_a[gi], gi, j, n_a, batch, t_len)
                              for gi in range(N_GROUPS))
            else:
                o, c0, c1, c2 = _attn_sample(qkv, a_bufs, new_a, j, batch, t_len)
                new_a = (c0, c1, c2)
            w_mix = w['attn_w_o'][j]
        elif kind == 1:
            proj = _norm_matmul(x2, g_mix, w['hg_w_in'][j])
            o, st = _hgrn(proj, w['hg_lb_logits'], w['hg_out_gain'][j][None], hg_s[j], i, batch, t_len)
            new_hg.append(st)
            w_mix = w['hg_w_o'][j]
        else:
            proj = _rwkv_proj(x3, w['norm_mix'][i][None], rw_shift[j][:, None, :], w['rw_mu'][j], w['rw_w_rkv'][j],
                              w['rw_w0'][j][None], w['rw_w1'][j], w['rw_w2'][j], w['rw_a0'][j][None],
                              w['rw_a1'][j], w['rw_a2'][j], w['rw_g1'][j], w['rw_g2'][j],
                              w['rw_k_k'][j][None], w['rw_k_a'][j][None], rw_bb, rw_tt)
            o, st = _rwkv_rec(proj[:7], w['rw_r_k'][j].reshape(1, d), w['rw_ln_g'][j][None], w['rw_ln_b'][j][None],
                              rw_s[j], batch, t_len)
            new_rw.append(st)
            new_sh.append(proj[7][:, 0, :])
            w_mix = w['rw_w_o'][j]
        x3 = _xattn(x3, o, w_mix, w['norm_mem'][i][None], w['xa_w_q'], w['xa_q_gain'][i][None], mem_kv, i,
                    w['xa_w_o'], xa_bb, xa_tt)
        x3, fb = _ffn(x3, w['norm_ffn'][i][None], w['ffn_w_in'], w['ffn_conv_w'][i], w['ffn_conv_b'][i][None],
                      w['ffn_w_down'], ffn_buf, i, ff_bb, ff_tt)
        new_ffn.append(fb)
    return (x3, new_a, jnp.stack(new_hg), jnp.stack(new_rw), jnp.stack(new_sh), jnp.stack(new_ffn))


def kernel(x_prompt, x_sample, mem_prompt, cache_attn_kv_w128, cache_attn_kv_w512, cache_attn_kv_w2048, state_hgrn, state_rwkv, state_rwkv_shift, state_ffn_conv, cache_mem_kv, norm_mix, norm_mem, norm_ffn, mem_norm, attn_w_qkv, attn_q_gain, attn_k_gain, attn_w_o, hg_w_in, hg_lb_logits, hg_out_gain, hg_w_o, rw_mu, rw_w_rkv, rw_w0, rw_w1, rw_w2, rw_a0, rw_a1, rw_a2, rw_g1, rw_g2, rw_k_k, rw_k_a, rw_r_k, rw_ln_g, rw_ln_b, rw_w_o, xa_w_q, xa_w_kv, xa_q_gain, xa_k_gain, xa_w_o, ffn_w_in, ffn_conv_w, ffn_conv_b, ffn_w_down):
    d = D_MODEL
    bp = x_prompt.shape[0]
    bs = x_sample.shape[0]
    n_a, n_b, n_c = attn_w_qkv.shape[0], hg_w_in.shape[0], rw_w_rkv.shape[0]
    bf = lambda a: a.astype(BF16)

    tile_h = lambda gn: jnp.broadcast_to(gn[:, :, None, :], (n_a, N_GROUPS, A_HEADS, A_HEAD_DIM)).reshape(n_a, 1, -1)
    head_gain = jnp.concatenate([tile_h(attn_q_gain), tile_h(attn_k_gain),
                                 jnp.ones((n_a, 1, N_GROUPS * A_WIDTH), F32)], axis=-1)

    w = {
        'norm_mix': norm_mix, 'norm_mem': norm_mem, 'norm_ffn': norm_ffn,
        'attn_w_qkv': bf(attn_w_qkv), 'attn_head_gain': head_gain, 'attn_w_o': bf(attn_w_o),
        'hg_w_in': bf(hg_w_in), 'hg_lb_logits': hg_lb_logits, 'hg_out_gain': hg_out_gain, 'hg_w_o': bf(hg_w_o),
        'rw_mu': rw_mu, 'rw_w_rkv': bf(rw_w_rkv), 'rw_w0': rw_w0, 'rw_w1': bf(rw_w1), 'rw_w2': bf(rw_w2),
        'rw_a0': rw_a0, 'rw_a1': bf(rw_a1), 'rw_a2': bf(rw_a2), 'rw_g1': bf(rw_g1), 'rw_g2': bf(rw_g2),
        'rw_k_k': rw_k_k, 'rw_k_a': rw_k_a, 'rw_r_k': rw_r_k, 'rw_ln_g': rw_ln_g, 'rw_ln_b': rw_ln_b,
        'rw_w_o': bf(rw_w_o),
        'xa_w_q': bf(xa_w_q), 'xa_q_gain': xa_q_gain, 'xa_w_o': bf(xa_w_o),
        'ffn_w_in': bf(ffn_w_in), 'ffn_conv_w': ffn_conv_w, 'ffn_conv_b': ffn_conv_b, 'ffn_w_down': bf(ffn_w_down),
    }

    mem_kv_prompt, mem_kv_flat = _memory_kv(mem_prompt.reshape(bp * N_MEM, d), mem_norm[:, None, :], bf(xa_w_kv),
                                            xa_k_gain[:, None, :], bp)

    as_rows = lambda cc: cc.reshape(cc.shape[0], cc.shape[1], cc.shape[2] * cc.shape[3] * cc.shape[4], cc.shape[5])
    as_cache = lambda rr: rr.reshape(rr.shape[0], rr.shape[1], rr.shape[2] // KV_ROWS, 2, A_HEADS, A_HEAD_DIM)
    y_prompt, a_p, hg_p, rw_p, sh_p, ffn_p = _trunk(
        x_prompt, mem_kv_flat, None,
        jnp.zeros((n_b, bp, HG_HEADS, HG_DK, HG_DK), F32),
        jnp.zeros((n_c, bp, RW_HEADS, RW_HEAD_DIM, RW_HEAD_DIM), F32),
        jnp.zeros((n_c, bp, d), F32),
        jnp.zeros((DEPTH, bp, CONV_W - 1, D_FF), F32), w, True)
    y_sample, a_s, hg_s, rw_s, sh_s, ffn_s = _trunk(
        x_sample, cache_mem_kv,
        tuple(as_rows(cc) for cc in (cache_attn_kv_w128, cache_attn_kv_w512, cache_attn_kv_w2048)),
        state_hgrn, state_rwkv, state_rwkv_shift, state_ffn_conv, w, False)
    a_p = tuple(as_cache(rr) for rr in a_p)
    a_s = tuple(as_cache(rr) for rr in a_s)
    return (y_prompt, y_sample, a_p[0], a_p[1], a_p[2], hg_p, rw_p, sh_p, ffn_p, mem_kv_prompt,
            a_s[0], a_s[1], a_s[2], hg_s, rw_s, sh_s, ffn_s)
```

```python
import functools
import math

import numpy as np
import jax
import jax.numpy as jnp
from jax import lax
from jax.experimental import pallas as pl
from jax.experimental.pallas import tpu as pltpu

F32 = jnp.float32
BF16 = jnp.bfloat16

D_MODEL = 1024
DEPTH = 4
N_MIXERS = 3
DIL_WINDOWS = (128, 512, 2048)
DIL_RATES = (1, 4, 16)
N_GROUPS = 3
A_SPAN = 128
A_HEADS = 4
A_HEAD_DIM = 128
A_BLOCK = 128
A_UNROLL = 16
A_WIDTH = A_HEADS * A_HEAD_DIM
KV_ROWS = 2 * A_HEADS
HG_HEADS = 8
HG_DK = 128
HG_CHUNK = 128
HG_SHORT_CHUNK = 16
RW_HEADS = 16
RW_HEAD_DIM = 64
RW_CHUNK = 64
RW_GN_EPS = 64e-5
N_MEM = 256
XA_HEADS = 4
XA_HEAD_DIM = D_MODEL // XA_HEADS
D_FF = 2816
CONV_W = 3
FF_GROUP = 6
RMS_EPS = 1e-6
LANES = 128
VMEM_LIMIT = 52 * 1024 * 1024

NT_DIMS = (((1,), (1,)), ((), ()))
TN_DIMS = (((0,), (0,)), ((), ()))


def _params(n_axes):
    return pltpu.CompilerParams(dimension_semantics=("arbitrary",) * n_axes, vmem_limit_bytes=VMEM_LIMIT)


def _rms(x):
    return x * lax.rsqrt(jnp.mean(x * x, axis=-1, keepdims=True) + RMS_EPS)


def _dot(a, b):
    return jnp.dot(a.astype(BF16), b.astype(BF16), preferred_element_type=F32)


def _dot_nt(a, b):
    return lax.dot_general(a.astype(BF16), b.astype(BF16), NT_DIMS, preferred_element_type=F32)


def _dot_tn(a, b):
    return lax.dot_general(a.astype(BF16), b.astype(BF16), TN_DIMS, preferred_element_type=F32)


def _split3(x):
    p1 = x.astype(BF16)
    r1 = x - p1.astype(F32)
    p2 = r1.astype(BF16)
    p3 = (r1 - p2.astype(F32)).astype(BF16)
    return jnp.concatenate([p1, p2, p3], axis=1)


def _exact_rowmix(mat_bf16, x):
    w = x.shape[1]
    y = jnp.dot(mat_bf16, _split3(x), preferred_element_type=F32)
    return y[:, :w] + y[:, w:2 * w] + y[:, 2 * w:]


def _softplus(x):
    return jnp.maximum(x, 0.0) + jnp.log1p(jnp.exp(-jnp.abs(x)))


def _sigmoid(x):
    return 1.0 / (1.0 + jnp.exp(-x))


def _silu(x):
    return x * _sigmoid(x)


def _nmm_body(x_ref, g_ref, w_ref, hg_ref, o_ref, *, hn_width, hn_tiles, tn):
    h = (_rms(x_ref[...]) * g_ref[...]).astype(BF16)
    for c in range(w_ref.shape[1] // tn):
        y = jnp.dot(h, w_ref[:, c * tn:(c + 1) * tn], preferred_element_type=F32)
        if hn_width is not None and hn_tiles[c]:
            for cc in range(tn // hn_width):
                sl = slice(c * tn + cc * hn_width, c * tn + (cc + 1) * hn_width)
                o_ref[:, sl] = _rms(y[:, cc * hn_width:(cc + 1) * hn_width]) * hg_ref[:, sl]
        else:
            o_ref[:, c * tn:(c + 1) * tn] = y


def _norm_matmul(x, gain, w, head_gain=None, hn_width=None, hn_tiles=None, tm=256, tn=512):
    m, d = x.shape
    n = w.shape[1]
    tm = min(tm, m)
    if head_gain is None:
        head_gain = jnp.ones((1, n), F32)
    return pl.pallas_call(
        functools.partial(_nmm_body, hn_width=hn_width, hn_tiles=hn_tiles, tn=tn),
        grid=(m // tm,),
        in_specs=[
            pl.BlockSpec((tm, d), lambda i: (i, 0)),
            pl.BlockSpec((1, d), lambda i: (0, 0)),
            pl.BlockSpec((d, n), lambda i: (0, 0)),
            pl.BlockSpec((1, n), lambda i: (0, 0)),
        ],
        out_specs=pl.BlockSpec((tm, n), lambda i: (i, 0)),
        out_shape=jax.ShapeDtypeStruct((m, n), F32),
        compiler_params=_params(1),
        name="norm_matmul",
    )(x, gain, w, head_gain)


def _ds(start, size, stride):
    return pl.ds(start, size) if stride == 1 else pl.ds(start, size, stride=stride)


def _attn_prompt_body(q0, q1, q2, k0, v0, k1, v1, k2, v2, o_ref, acc, m_s, l_s, *, seq):
    scale = A_HEAD_DIM ** -0.5
    blk = A_BLOCK
    for g, (q_ref, k_ref, v_ref) in enumerate(((q0, k0, v0), (q1, k1, v1), (q2, k2, v2))):
        dil = DIL_RATES[g]
        n = seq // dil
        nb = n // blk
        width = 2 * blk if nb > 1 else blk
        ii = lax.broadcasted_iota(jnp.int32, (blk, width), 0)
        jj = lax.broadcasted_iota(jnp.int32, (blk, width), 1)
        if nb > 1:
            in_cur = (jj >= blk) & (jj - blk <= ii)
            in_prev = (jj < blk) & (jj >= ii)
        else:
            in_cur, in_prev = jj <= ii, None

        def body(it, carry, g=g, dil=dil, nb=nb, q_ref=q_ref, k_ref=k_ref, v_ref=v_ref, in_cur=in_cur,
                 in_prev=in_prev):
            rows, ss, vws = [], [], []
            for uu in range(A_UNROLL):
                idx = it * A_UNROLL + uu
                r = idx // nb
                mb = idx % nb
                rw = _ds(r + dil * blk * mb, blk, dil)
                kc = k_ref[rw, :].astype(BF16)
                vc = v_ref[rw, :].astype(BF16)
                if nb > 1:
                    prev = _ds(r + dil * blk * jnp.maximum(mb - 1, 0), blk, dil)
                    kw = jnp.concatenate([k_ref[prev, :].astype(BF16), kc], axis=0)
                    vw = jnp.concatenate([v_ref[prev, :].astype(BF16), vc], axis=0)
                    prev_bias = jnp.where(mb > 0, 0.0, -jnp.inf)
                    bias = jnp.where(in_cur, 0.0, jnp.where(in_prev, prev_bias, -jnp.inf))
                else:
                    kw, vw = kc, vc
                    bias = jnp.where(in_cur, 0.0, -jnp.inf)
                rows.append(rw)
                vws.append(jnp.concatenate([vw, jnp.ones((vw.shape[0], LANES), BF16)], axis=1))
                ss.append(_dot_nt(q_ref[rw, :], kw) * scale + bias)
            ms = [jnp.max(s, axis=-1, keepdims=True) for s in ss]
            ps = [jnp.exp(s - m) for s, m in zip(ss, ms)]
            ols = [_dot(p, vw) for p, vw in zip(ps, vws)]
            os_ = [x[:, 0:A_HEAD_DIM] for x in ols]
            l_bs = [x[:, A_HEAD_DIM:A_HEAD_DIM + LANES] for x in ols]
            m_bs = [jnp.broadcast_to(m, (blk, LANES)) for m in ms]
            if g == 0:
                for rw, o, m_b, l_b in zip(rows, os_, m_bs, l_bs):
                    acc[rw, :] = o
                    m_s[rw, :] = m_b
                    l_s[rw, :] = l_b
            else:
                olds = [(m_s[rw, :], l_s[rw, :], acc[rw, :]) for rw in rows]
                for rw, o, m_b, l_b, (m_old, l_old, acc_old) in zip(rows, os_, m_bs, l_bs, olds):
                    m_new = jnp.maximum(m_old, m_b)
                    a_old = jnp.exp(m_old - m_new)
                    a_cur = jnp.exp(m_b - m_new)
                    acc[rw, :] = acc_old * a_old + o * a_cur
                    l_s[rw, :] = l_old * a_old + l_b * a_cur
                    m_s[rw, :] = m_new
            return carry

        assert (dil * nb) % A_UNROLL == 0
        lax.fori_loop(0, dil * nb // A_UNROLL, body, 0)
    o_ref[...] = acc[...] / l_s[...]


def _attn_prompt(qkv, batch, seq):
    m = batch * seq
    assert seq % (A_BLOCK * max(DIL_RATES)) == 0 and A_BLOCK == A_SPAN
    nq = N_GROUPS * A_HEADS

    def spec(col_fn):
        return pl.BlockSpec((seq, A_HEAD_DIM), lambda b, h: (b, col_fn(h)))

    in_specs = [spec(lambda h, g=g: g * A_HEADS + h) for g in range(N_GROUPS)]
    for g in range(N_GROUPS):
        in_specs.append(spec(lambda h, g=g: nq + g * A_HEADS + h))
        in_specs.append(spec(lambda h, g=g: 2 * nq + g * A_HEADS + h))
    return pl.pallas_call(
        functools.partial(_attn_prompt_body, seq=seq),
        grid=(batch, A_HEADS),
        in_specs=in_specs,
        out_specs=pl.BlockSpec((seq, A_HEAD_DIM), lambda b, h: (b, h)),
        out_shape=jax.ShapeDtypeStruct((m, A_WIDTH), F32),
        scratch_shapes=[pltpu.VMEM((seq, A_HEAD_DIM), F32), pltpu.VMEM((seq, LANES), F32),
                        pltpu.VMEM((seq, LANES), F32)],
        compiler_params=_params(2),
        name="dilated_attn_prompt",
    )(*([qkv] * 9))


def _attn_sample_body(*refs, t_new, n_alias):
    qkv_ref, bufs = refs[0], refs[1:4]
    o_ref, outs = refs[4 + n_alias], refs[5 + n_alias:8 + n_alias]
    scale = A_HEAD_DIM ** -0.5
    pad = jnp.zeros((LANES - t_new, A_HEAD_DIM), F32)
    nq = N_GROUPS * A_WIDTH

    def new_rows(g, kv, h):
        base = (1 + kv) * nq + g * A_WIDTH + h * A_HEAD_DIM
        return qkv_ref[:, base:base + A_HEAD_DIM]

    for g in range(N_GROUPS):
        length = bufs[g].shape[0] // KV_ROWS
        keep = (length - t_new) * KV_ROWS
        outs[g][0:keep, :] = bufs[g][t_new * KV_ROWS:length * KV_ROWS, :]
        for kv in range(2):
            for h in range(A_HEADS):
                outs[g][pl.ds(keep + kv * A_HEADS + h, t_new, stride=KV_ROWS), :] = new_rows(g, kv, h)

    for h in range(A_HEADS):
        parts, vals = [], []
        for g in range(N_GROUPS):
            dil = DIL_RATES[g]
            length = bufs[g].shape[0] // KV_ROWS
            q = qkv_ref[:, g * A_WIDTH + h * A_HEAD_DIM: g * A_WIDTH + (h + 1) * A_HEAD_DIM]
            s_buf = _dot_nt(q, bufs[g][pl.ds(h, length, stride=KV_ROWS), :]) * scale
            s_new = _dot_nt(q, jnp.concatenate([new_rows(g, 0, h), pad], axis=0)) * scale
            for s, base in ((s_buf, 0), (s_new, length)):
                ii = lax.broadcasted_iota(jnp.int32, s.shape, 0)
                jj = lax.broadcasted_iota(jnp.int32, s.shape, 1) + base
                dd = jj - ii
                valid = (dd >= 0) & (dd <= A_SPAN * dil) & ((dd & (dil - 1)) == 0) & (jj < length + t_new)
                parts.append(jnp.where(valid, s, -jnp.inf))
            vals.append(bufs[g][pl.ds(A_HEADS + h, length, stride=KV_ROWS), :])
            vals.append(jnp.concatenate([new_rows(g, 1, h), pad], axis=0))
        mx = functools.reduce(jnp.maximum, [jnp.max(s, axis=-1, keepdims=True) for s in parts])
        exps = [jnp.exp(s - mx) for s in parts]
        den = functools.reduce(lambda a, b: a + b, [jnp.sum(e, axis=-1, keepdims=True) for e in exps])
        o = functools.reduce(lambda a, b: a + b, [_dot(e / den, vv) for e, vv in zip(exps, vals)])
        o_ref[:, h * A_HEAD_DIM:(h + 1) * A_HEAD_DIM] = o


def _attn_sample(qkv, caches, prev_out, j, batch, t_new):
    assert t_new % 8 == 0 and t_new <= LANES
    lens = [cch.shape[2] // KV_ROWS for cch in caches]
    for g in range(N_GROUPS):
        assert lens[g] == A_SPAN * DIL_RATES[g] and (DIL_RATES[g] & (DIL_RATES[g] - 1)) == 0

    def cache_spec(g):
        return pl.BlockSpec((None, None, lens[g] * KV_ROWS, A_HEAD_DIM), lambda b: (j, b, 0, 0))

    in_specs = [pl.BlockSpec((t_new, qkv.shape[1]), lambda b: (b, 0))] + [cache_spec(g) for g in range(N_GROUPS)]
    n_alias = 0 if prev_out is None else N_GROUPS
    aliases = {}
    if prev_out is not None:
        in_specs += [pl.BlockSpec(memory_space=pl.ANY)] * N_GROUPS
        aliases = {4 + g: 1 + g for g in range(N_GROUPS)}
    out_specs = [pl.BlockSpec((t_new, A_WIDTH), lambda b: (b, 0))] + [cache_spec(g) for g in range(N_GROUPS)]
    out_shape = [jax.ShapeDtypeStruct((batch * t_new, A_WIDTH), F32)]
    out_shape += [jax.ShapeDtypeStruct(cch.shape, F32) for cch in caches]
    return pl.pallas_call(
        functools.partial(_attn_sample_body, t_new=t_new, n_alias=n_alias),
        grid=(batch,),
        in_specs=in_specs,
        out_specs=out_specs,
        out_shape=out_shape,
        input_output_aliases=aliases,
        compiler_params=_params(1),
        name="dilated_attn_sample",
    )(qkv, *caches, *(prev_out or ()))


def _kv_cache_body(*refs):
    k_ref, v_ref, o_ref = refs[0], refs[1], refs[-1]
    tr = k_ref.shape[0]
    for kv, x_ref in enumerate((k_ref, v_ref)):
        for h in range(A_HEADS):
            o_ref[pl.ds(kv * A_HEADS + h, tr, stride=KV_ROWS), :] = x_ref[:, h * A_HEAD_DIM:(h + 1) * A_HEAD_DIM]


def _kv_cache_out(qkv, prev_out, g, j, n_layers, batch, seq):
    win = min(DIL_WINDOWS[g], seq)
    tr = min(win, 512)
    first = (seq - win) // tr
    row = lambda b, t: b * (seq // tr) + first + t
    in_specs = [pl.BlockSpec((tr, A_WIDTH), lambda b, t: (row(b, t), N_GROUPS + g)),
                pl.BlockSpec((tr, A_WIDTH), lambda b, t: (row(b, t), 2 * N_GROUPS + g))]
    aliases = {}
    if prev_out is not None:
        in_specs.append(pl.BlockSpec(memory_space=pl.ANY))
        aliases = {2: 0}
    return pl.pallas_call(
        _kv_cache_body,
        grid=(batch, win // tr),
        in_specs=in_specs,
        out_specs=pl.BlockSpec((None, None, tr * KV_ROWS, A_HEAD_DIM), lambda b, t: (j, b, t, 0)),
        out_shape=jax.ShapeDtypeStruct((n_layers, batch, win * KV_ROWS, A_HEAD_DIM), F32),
        input_output_aliases=aliases,
        compiler_params=_params(2),
        name="kv_cache_out",
    )(qkv, qkv, *(() if prev_out is None else (prev_out,)))


def _hgrn_consts(c):
    t = np.arange(c)[:, None]
    u = np.arange(c)[None, :]
    masks = [(u == t)]
    s = 1
    while s < c:
        upper = (t // s) % 2 == 1
        lower_u = (u // s) % 2 == 0
        masks.append(upper & lower_u & (t // (2 * s) == u // (2 * s)))
        s *= 2
    masks = np.stack([m.astype(np.float32) for m in masks], axis=0)
    return jnp.asarray((u <= t).astype(np.float32), BF16), jnp.asarray(masks, F32)


def _hgrn_level_exponent(b, s, row):
    c, width = b.shape
    if s >= 8:
        mid = jnp.concatenate([jnp.broadcast_to(b[j * 2 * s + s - 1:j * 2 * s + s, :], (2 * s, width))
                               for j in range(c // (2 * s))], axis=0)
    else:
        b3 = b.reshape(c // 8, 8, width)
        sub = lax.broadcasted_iota(jnp.int32, b3.shape, 1)
        pick = lambda k: jnp.broadcast_to(b3[:, k:k + 1, :], b3.shape)
        mid3 = pick(s - 1)
        for k in range(2 * s + s - 1, 8, 2 * s):
            mid3 = jnp.where(sub >= k - (s - 1), pick(k), mid3)
        mid = mid3.reshape(c, width)
    return jnp.where((row & s) != 0, b - mid, mid - b)


def _hgrn_body(q_ref, z_ref, i_ref, g_ref, lbl_ref, og_ref, s0_ref, mat_ref, msk_ref, o_ref, st_ref, st_scr,
               *, layer, t_len, c, hp):
    n_lvl = msk_ref.shape[0] - 1
    width = hp * HG_DK
    hr = range(hp)
    lane = lambda x, hd: x[:, hd * HG_DK:(hd + 1) * HG_DK]
    lg = lbl_ref[...]
    e = jnp.exp(lg - jnp.max(lg, axis=0, keepdims=True))
    prob = e / jnp.sum(e, axis=0, keepdims=True)
    lb = jnp.sum(prob[0:layer + 1], axis=0, keepdims=True) - prob[0:1]
    log_lb = jnp.log(lb)
    log_1m = jnp.log1p(-lb)
    @pl.when(pl.program_id(2) == 0)
    def _():
        for hd in hr:
            st_scr[hd] = s0_ref[hd].T

    row = lax.broadcasted_iota(jnp.int32, (c, width), 0)
    out_gain = jnp.concatenate([og_ref[...]] * hp, axis=1)

    def chunk(ci, carry):
        if t_len >= c:
            rows = pl.ds(pl.multiple_of(ci * c, c), c)
            ld = lambda ref: ref[rows, :]
            live = None
        else:
            padz = jnp.zeros((c - t_len, width), F32)
            ld = lambda ref: jnp.concatenate([ref[...], padz], axis=0)
            live = row < t_len
        z = ld(z_ref)
        lsig = jnp.minimum(z, 0.0) - jnp.log1p(jnp.exp(-jnp.abs(z)))
        a1 = jnp.broadcast_to(log_lb, z.shape)
        a2 = log_1m + lsig
        hi = jnp.maximum(a1, a2)
        lo = jnp.minimum(a1, a2)
        log_f = hi + jnp.log1p(jnp.exp(lo - hi))
        kg = (1.0 - lb) * _sigmoid(-z)
        if live is not None:
            log_f = jnp.where(live, log_f, 0.0)
            kg = jnp.where(live, kg, 0.0)
        qa = _silu(ld(q_ref))
        vv = ld(i_ref)
        b = _exact_rowmix(mat_ref[...], log_f)
        sts = [st_scr[hd] for hd in hr]
        a = [_dot_nt(lane(qa, hd), lane(kg, hd)) * msk_ref[0] for hd in hr]
        for lv in range(n_lvl):
            wv = jnp.exp(_hgrn_level_exponent(b, 1 << lv, row))
            qw = qa * wv
            kw = kg * wv
            a = [a[hd] + _dot_nt(lane(qw, hd), lane(kw, hd)) * msk_ref[1 + lv] for hd in hr]
        qb = qa * jnp.exp(b)
        kd = kg * jnp.exp(b[c - 1:c] - b)
        dec = jnp.exp(b[c - 1:c])
        o = [_dot_nt(lane(qb, hd), sts[hd]) + _dot(a[hd], lane(vv, hd)) for hd in hr]
        st_scr[...] = jnp.stack([sts[hd] * lane(dec, hd) + _dot_tn(lane(vv, hd), lane(kd, hd)) for hd in hr], axis=0)
        y = jnp.concatenate([_rms(x) for x in o], axis=1) * out_gain * _silu(ld(g_ref))
        if t_len >= c:
            o_ref[rows, :] = y
        else:
            o_ref[...] = y[0:t_len]
        return carry

    lax.fori_loop(0, max(t_len // c, 1), chunk, 0)
    for hd in hr:
        st_ref[hd] = st_scr[hd].T


def _hgrn(proj, lb_logits, out_gain, s0, layer, batch, t_len, tt=512):
    tt = min(tt, t_len)
    if tt >= HG_CHUNK:
        c, hp = HG_CHUNK, 8
        assert tt % c == 0 and t_len % tt == 0
    else:
        c, hp = HG_SHORT_CHUNK, HG_HEADS
        assert tt % 8 == 0 and tt <= c
    mats, masks = _hgrn_consts(c)
    width = hp * HG_DK
    ng = HG_HEADS // hp
    nt = t_len // tt

    def col(part):
        return pl.BlockSpec((tt, width), lambda b, h, t: (b * nt + t, part * ng + h))

    st_spec = pl.BlockSpec((None, hp, HG_DK, HG_DK), lambda b, h, t: (b, h, 0, 0))
    return pl.pallas_call(
        functools.partial(_hgrn_body, layer=layer, t_len=tt, c=c, hp=hp),
        grid=(batch, ng, nt),
        in_specs=[col(0), col(1), col(2), col(3),
                  pl.BlockSpec((DEPTH, width), lambda b, h, t: (0, h)),
                  pl.BlockSpec((1, HG_DK), lambda b, h, t: (0, 0)),
                  st_spec,
                  pl.BlockSpec(mats.shape, lambda b, h, t: (0, 0)),
                  pl.BlockSpec(masks.shape, lambda b, h, t: (0, 0, 0))],
        out_specs=[pl.BlockSpec((tt, width), lambda b, h, t: (b * nt + t, h)), st_spec],
        out_shape=[jax.ShapeDtypeStruct((batch * t_len, HG_HEADS * HG_DK), F32),
                   jax.ShapeDtypeStruct((batch, HG_HEADS, HG_DK, HG_DK), F32)],
        scratch_shapes=[pltpu.VMEM((hp, HG_DK, HG_DK), F32)],
        compiler_params=_params(3),
        name="hgrn2_chunked",
    )(proj, proj, proj, proj, lb_logits, out_gain, s0, mats, masks)


def _rwkv_proj_body(x_ref, g_ref, sh_ref, mu_ref, wrkv_ref, w0_ref, w1_ref, w2_ref, a0_ref, a1_ref, a2_ref,
                    g1_ref, g2_ref, kk_ref, ka_ref,
                    r_o, k_o, v_o, ld_o, kk_o, a_o, g_o, sh_o, carry, *, bb, tt):
    t = pl.program_id(1)
    d = D_MODEL

    @pl.when(t == 0)
    def _():
        carry[...] = sh_ref[...]

    h3 = _rms(x_ref[...]) * g_ref[...]
    h = h3.reshape(bb * tt, d)
    rolled = pltpu.roll(h, 1, 0).reshape(bb, tt, d)
    tpos = lax.broadcasted_iota(jnp.int32, (bb, tt, d), 1)
    prev = jnp.where(tpos == 0, carry[...], rolled).reshape(bb * tt, d)
    last = h3[:, tt - 1:tt, :]
    carry[...] = last
    sh_o[...] = last
    dx = prev - h
    mix = lambda jm: h + dx * mu_ref[jm:jm + 1, :]
    r = _dot(mix(0), wrkv_ref[0])
    k = _dot(mix(1), wrkv_ref[1])
    v = _dot(mix(2), wrkv_ref[2])
    wl = w0_ref[...] + _dot(jnp.tanh(_dot(mix(3), w1_ref[...])), w2_ref[...])
    wlog = -_softplus(-wl) - 0.5
    a = _sigmoid(a0_ref[...] + _dot(_dot(mix(4), a1_ref[...]), a2_ref[...]))
    gate = _dot(_sigmoid(_dot(mix(5), g1_ref[...])), g2_ref[...])
    r_o[...] = r
    k_o[...] = k * (1.0 + (a - 1.0) * ka_ref[...])
    v_o[...] = v
    ld_o[...] = -jnp.exp(wlog)
    kk_o[...] = k * kk_ref[...]
    a_o[...] = a
    g_o[...] = gate


def _rwkv_proj(x3, gain, shift, mu, wrkv, w0, w1, w2, a0, a1, a2, g1, g2, k_k, k_a, bb, tt):
    batch, t_len, d = x3.shape
    m = batch * t_len
    rows = bb * tt
    full = lambda arr: pl.BlockSpec(arr.shape, lambda b, t: (0,) * arr.ndim)
    row_spec = pl.BlockSpec((rows, d), lambda b, t: (b * (t_len // tt) + t, 0))
    outs = pl.pallas_call(
        functools.partial(_rwkv_proj_body, bb=bb, tt=tt),
        grid=(batch // bb, t_len // tt),
        in_specs=[pl.BlockSpec((bb, tt, d), lambda b, t: (b, t, 0)), full(gain),
                  pl.BlockSpec((bb, 1, d), lambda b, t: (b, 0, 0)), full(mu), full(wrkv),
                  full(w0), full(w1), full(w2), full(a0), full(a1), full(a2), full(g1), full(g2),
                  full(k_k), full(k_a)],
        out_specs=[row_spec] * 7 + [pl.BlockSpec((bb, 1, d), lambda b, t: (b, 0, 0))],
        out_shape=[jax.ShapeDtypeStruct((m, d), F32)] * 7 + [jax.ShapeDtypeStruct((batch, 1, d), F32)],
        scratch_shapes=[pltpu.VMEM((bb, 1, d), F32)],
        compiler_params=_params(2),
        name="rwkv_proj",
    )(x3, gain, shift, mu, wrkv, w0, w1, w2, a0, a1, a2, g1, g2, k_k, k_a)
    return outs


def _rwkv_consts(c):
    n = RW_HEAD_DIM
    t = np.arange(c)[:, None]
    u = np.arange(c)[None, :]
    f = lambda mat, dt: jnp.asarray(np.asarray(mat).astype(np.float32), dt)
    tri = f(u <= t, BF16)
    half = np.concatenate([u < t, u <= t], axis=0)
    lo = f(np.concatenate([half, half], axis=1), F32)
    eye = f(u == t, F32)
    lane = np.arange(LANES)
    bdiag = f((lane[:, None] // n) == (lane[None, :] // n), F32)
    return tri, lo, eye, bdiag


def _rwkv_rec_body(r_ref, k_ref, v_ref, ld_ref, kk_ref, a_ref, g_ref, rk_ref, lng_ref, lnb_ref, s0_ref,
                   tri_ref, lo_ref, eye_ref, bd_ref, o_ref, st_ref, st_scr, *, tt):
    c = RW_CHUNK
    n = RW_HEAD_DIM
    heads = RW_HEADS
    pairs = heads // 2
    w = D_MODEL
    t = pl.program_id(1)

    @pl.when(t == 0)
    def _():
        zero = jnp.zeros((n, n), F32)
        for p in range(pairs):
            top = jnp.concatenate([s0_ref[2 * p], zero], axis=1)
            bot = jnp.concatenate([zero, s0_ref[2 * p + 1]], axis=1)
            st_scr[p] = jnp.concatenate([top, bot], axis=0)

    low_2c = lax.broadcasted_iota(jnp.int32, (2 * c, LANES), 1) < n
    low_c = lax.broadcasted_iota(jnp.int32, (c, LANES), 1) < n
    tile = lambda x, p: x[:, p * LANES:(p + 1) * LANES]
    hr = range(heads)
    pr = range(pairs)
    zeros_c = jnp.zeros((c, LANES), F32)

    def head_sums(x):
        out = []
        for p in pr:
            xt = tile(x, p)
            lo_s = jnp.sum(jnp.where(low_c, xt, 0.0), axis=-1, keepdims=True)
            hi_s = jnp.sum(jnp.where(low_c, 0.0, xt), axis=-1, keepdims=True)
            out.append(jnp.where(low_c, lo_s, hi_s))
        return jnp.concatenate(out, axis=1)

    def chunk(ci, carry):
        if tt >= c:
            rows = pl.ds(pl.multiple_of(ci * c, c), c)
            ld = lambda ref: ref[rows, :]
        else:
            padz = jnp.zeros((c - tt, w), F32)
            ld = lambda ref: jnp.concatenate([ref[...], padz], axis=0)
        r, kf, v, lc, kk, a, gate = (ld(ref) for ref in (r_ref, k_ref, v_ref, ld_ref, kk_ref, a_ref, g_ref))
        kk = kk / jnp.maximum(jnp.sqrt(head_sums(kk * kk)), 1e-12)
        bvec = kk * a
        cum = _exact_rowmix(tri_ref[...], lc)
        c_end = cum[c - 1:c]
        e_neg = jnp.exp(-cum)
        e_end = jnp.exp(c_end - cum)
        d_end = jnp.exp(c_end)
        kr = jnp.concatenate([kk * jnp.exp(cum - lc), r * jnp.exp(cum)], axis=0)
        bx = jnp.concatenate([bvec * e_neg, kf * e_neg], axis=0)
        bk = jnp.concatenate([bvec * e_end, kf * e_end], axis=0)
        lo = lo_ref[...]
        eye = eye_ref[...]
        krm = [jnp.where(low_2c if h % 2 == 0 else jnp.logical_not(low_2c), tile(kr, h // 2), 0.0) for h in hr]
        gm = [_dot_nt(krm[h], tile(bx, h // 2)) * lo for h in hr]
        nn = [x[0:c, 0:c] for x in gm]
        tm = [eye - x for x in nn]
        npow = [_dot(x, x) for x in nn]
        p2 = 2
        while p2 < c:
            tm = [x + _dot(x, y) for x, y in zip(tm, npow)]
            p2 *= 2
            if p2 < c:
                npow = [_dot(x, x) for x in npow]
        sts = [st_scr[p] for p in pr]
        ks = [_dot_nt(tile(kr, p), sts[p]) for p in pr]
        vz = [jnp.concatenate([zeros_c, tile(v, p)], axis=0) for p in pr]
        av = [jnp.where(low_2c, _dot(gm[2 * p], vz[p]), _dot(gm[2 * p + 1], vz[p])) for p in pr]
        rhs = [ks[p][0:c] + av[p][0:c] for p in pr]
        u = [jnp.where(low_c, -_dot(tm[2 * p], rhs[p]), -_dot(tm[2 * p + 1], rhs[p])) for p in pr]
        uz = [jnp.concatenate([u[p], zeros_c], axis=0) for p in pr]
        y = [ks[p][c:2 * c] + av[p][c:2 * c]
             + jnp.where(low_c, _dot(gm[2 * p][c:2 * c], uz[p]), _dot(gm[2 * p + 1][c:2 * c], uz[p])) for p in pr]
        new_st = [sts[p] * tile(d_end, p)
                  + _dot_tn(jnp.concatenate([u[p], tile(v, p)], axis=0), tile(bk, p)) * bd_ref[...] for p in pr]
        st_scr[...] = jnp.stack(new_st, axis=0)
        yy = jnp.concatenate(y, axis=1)
        yc = yy - head_sums(yy) * (1.0 / n)
        var = head_sums(yc * yc) * (1.0 / n)
        yn = yc * lax.rsqrt(var + RW_GN_EPS) * lng_ref[...] + lnb_ref[...]
        z = (yn + head_sums(r * kf * rk_ref[...]) * v) * gate
        if tt >= c:
            o_ref[rows, :] = z
        else:
            o_ref[...] = z[0:tt]
        return carry

    lax.fori_loop(0, max(tt // c, 1), chunk, 0)
    for p in range(pairs):
        blk = st_scr[p]
        st_ref[2 * p] = blk[0:n, 0:n]
        st_ref[2 * p + 1] = blk[n:2 * n, n:2 * n]


def _rwkv_rec(proj, r_k, ln_g, ln_b, s0, batch, t_len, tt=512):
    r, kf, v, ld, kk, a, gate = proj
    tt = min(tt, t_len)
    assert tt % RW_CHUNK == 0 or (tt < RW_CHUNK and tt % 8 == 0)
    assert RW_HEADS * RW_HEAD_DIM == D_MODEL and 2 * RW_HEAD_DIM == LANES and 2 * RW_CHUNK == LANES
    consts = _rwkv_consts(RW_CHUNK)
    nt = t_len // tt
    row_spec = pl.BlockSpec((tt, D_MODEL), lambda b, t: (b * nt + t, 0))
    par_spec = pl.BlockSpec((1, D_MODEL), lambda b, t: (0, 0))
    st_spec = pl.BlockSpec((None, RW_HEADS, RW_HEAD_DIM, RW_HEAD_DIM), lambda b, t: (b, 0, 0, 0))
    const = lambda arr: pl.BlockSpec(arr.shape, lambda b, t: (0, 0))
    return pl.pallas_call(
        functools.partial(_rwkv_rec_body, tt=tt),
        grid=(batch, nt),
        in_specs=[row_spec] * 7 + [par_spec] * 3 + [st_spec] + [const(x) for x in consts],
        out_specs=[row_spec, st_spec],
        out_shape=[jax.ShapeDtypeStruct((batch * t_len, D_MODEL), F32),
                   jax.ShapeDtypeStruct((batch, RW_HEADS, RW_HEAD_DIM, RW_HEAD_DIM), F32)],
        scratch_shapes=[pltpu.VMEM((RW_HEADS // 2, LANES, LANES), F32)],
        compiler_params=_params(2),
        name="rwkv7_chunked",
    )(r, kf, v, ld, kk, a, gate, r_k, ln_g, ln_b, s0, *consts)


def _xattn_body(x_ref, a_ref, wm_ref, g_ref, wq_ref, qg_ref, kv_ref, wo_ref, o_ref, *, bb, tt, native):
    d = D_MODEL
    e = XA_HEAD_DIM
    scale = XA_HEAD_DIM ** -0.5
    x = x_ref[...].reshape(bb * tt, d) + jnp.dot(a_ref[...].astype(BF16), wm_ref[...], preferred_element_type=F32)
    q = jnp.dot((_rms(x) * g_ref[...]).astype(BF16), wq_ref[...], preferred_element_type=F32)
    qn = lambda b, h: _rms(q[b * tt:(b + 1) * tt, h * e:(h + 1) * e]) * qg_ref[...]
    per_batch = []
    for b in range(bb):
        if native:
            rows = N_MEM * XA_HEADS
            s = _dot_nt(jnp.concatenate([qn(b, h) for h in range(XA_HEADS)], axis=0),
                        kv_ref[b, :, 0].reshape(rows, e)) * scale
            own = (lax.broadcasted_iota(jnp.int32, s.shape, 0) // tt
                   == lax.broadcasted_iota(jnp.int32, s.shape, 1) % XA_HEADS)
            s = jnp.where(own, s, -jnp.inf)
            p = jnp.exp(s - jnp.max(s, axis=-1, keepdims=True))
            p = p / jnp.sum(p, axis=-1, keepdims=True)
            oh = _dot(p, kv_ref[b, :, 1].reshape(rows, e))
            per_batch.append(jnp.concatenate([oh[h * tt:(h + 1) * tt] for h in range(XA_HEADS)], axis=-1))
            continue
        heads = []
        for h in range(XA_HEADS):
            s = _dot_nt(qn(b, h), kv_ref[b, :, h * e:(h + 1) * e]) * scale
            p = jnp.exp(s - jnp.max(s, axis=-1, keepdims=True))
            p = p / jnp.sum(p, axis=-1, keepdims=True)
            heads.append(_dot(p, kv_ref[b, :, d + h * e:d + (h + 1) * e]))
        per_batch.append(jnp.concatenate(heads, axis=-1))
    o = jnp.concatenate(per_batch, axis=0) if bb > 1 else per_batch[0]
    y = x + jnp.dot(o.astype(BF16), wo_ref[...], preferred_element_type=F32)
    o_ref[...] = y.reshape(bb, tt, d)


def _xattn(x3, mix_out, w_mix, gain, wq, q_gain, kv_all, layer, wo, bb, tt):
    batch, t_len, d = x3.shape
    km = mix_out.shape[1]
    nt = t_len // tt
    assert bb == 1 or nt == 1
    native = kv_all.ndim == 6
    if native:
        kv_spec = pl.BlockSpec((None, bb, N_MEM, 2, XA_HEADS, XA_HEAD_DIM), lambda b, t: (layer, b, 0, 0, 0, 0))
    else:
        kv_spec = pl.BlockSpec((None, bb, N_MEM, 2 * d), lambda b, t: (layer, b, 0, 0))
    return pl.pallas_call(
        functools.partial(_xattn_body, bb=bb, tt=tt, native=native),
        grid=(batch // bb, t_len // tt),
        in_specs=[pl.BlockSpec((bb, tt, d), lambda b, t: (b, t, 0)),
                  pl.BlockSpec((bb * tt, km), lambda b, t: (b * nt + t, 0)),
                  pl.BlockSpec((km, d), lambda b, t: (0, 0)),
                  pl.BlockSpec((1, d), lambda b, t: (0, 0)),
                  pl.BlockSpec((None, d, d), lambda b, t: (layer, 0, 0)),
                  pl.BlockSpec((1, XA_HEAD_DIM), lambda b, t: (0, 0)),
                  kv_spec,
                  pl.BlockSpec((None, d, d), lambda b, t: (layer, 0, 0))],
        out_specs=pl.BlockSpec((bb, tt, d), lambda b, t: (b, t, 0)),
        out_shape=jax.ShapeDtypeStruct((batch, t_len, d), F32),
        compiler_params=_params(2),
        name="memory_xattn",
    )(x3, mix_out, w_mix, gain, wq, q_gain, kv_all, wo)


def _memkv_body(x_ref, g_ref, w_ref, kg_ref, o_ref, flat_ref):
    c = pl.program_id(1)
    e = XA_HEAD_DIM
    h = (_rms(x_ref[...]) * g_ref[...]).astype(BF16)
    y = jnp.dot(h, w_ref[...], preferred_element_type=F32)

    @pl.when(c == 0)
    def _():
        for h in range(XA_HEADS):
            kn = _rms(y[:, h * e:(h + 1) * e]) * kg_ref[...]
            o_ref[:, h, :] = kn
            flat_ref[:, h * e:(h + 1) * e] = kn

    @pl.when(c == 1)
    def _():
        flat_ref[...] = y
        for h in range(XA_HEADS):
            o_ref[:, h, :] = y[:, h * e:(h + 1) * e]


def _memory_kv(mem2, mem_norm, w_kv, k_gain, batch):
    d = D_MODEL
    return pl.pallas_call(
        _memkv_body,
        grid=(DEPTH, 2, batch),
        in_specs=[pl.BlockSpec((N_MEM, d), lambda l, c, b: (b, 0)),
                  pl.BlockSpec((None, 1, d), lambda l, c, b: (l, 0, 0)),
                  pl.BlockSpec((None, d, d), lambda l, c, b: (l, 0, c)),
                  pl.BlockSpec((None, 1, XA_HEAD_DIM), lambda l, c, b: (l, 0, 0))],
        out_specs=[pl.BlockSpec((None, None, N_MEM, None, XA_HEADS, XA_HEAD_DIM), lambda l, c, b: (l, b, 0, c, 0, 0)),
                   pl.BlockSpec((None, None, N_MEM, d), lambda l, c, b: (l, b, 0, c))],
        out_shape=[jax.ShapeDtypeStruct((DEPTH, batch, N_MEM, 2, XA_HEADS, XA_HEAD_DIM), F32),
                   jax.ShapeDtypeStruct((DEPTH, batch, N_MEM, 2 * d), F32)],
        compiler_params=_params(3),
        name="memory_kv",
    )(mem2, mem_norm, w_kv, k_gain)


def _ffn_body(x_ref, g_ref, win_ref, cw_ref, cb_ref, wd_ref, pin_ref, o_ref, st_ref, acc, carry, *, bb, tt, tf):
    t = pl.program_id(1)
    d = D_MODEL

    @pl.when(t == 0)
    def _():
        carry[...] = pin_ref[...]

    x = x_ref[...].reshape(bb * tt, d)
    h = (_rms(x) * g_ref[...]).astype(BF16)
    acc[...] = x
    tpos = lax.broadcasted_iota(jnp.int32, (bb, tt, tf), 1)
    tails, acts = [], []
    nf = D_FF // tf
    for f in range(nf):
        fs = slice(f * tf, (f + 1) * tf)
        u = jnp.dot(h, win_ref[:, fs], preferred_element_type=F32)
        gate = jnp.dot(h, win_ref[:, D_FF + f * tf:D_FF + (f + 1) * tf], preferred_element_type=F32)
        prev = carry[:, :, fs]
        u3 = u.reshape(bb, tt, tf)
        u1 = jnp.where(tpos == 0, prev[:, 1:2, :], pltpu.roll(u, 1, 0).reshape(bb, tt, tf))
        u2 = pltpu.roll(u, 2, 0).reshape(bb, tt, tf)
        u2 = jnp.where(tpos == 0, prev[:, 0:1, :], jnp.where(tpos == 1, prev[:, 1:2, :], u2))
        conv = cb_ref[:, fs] + cw_ref[0:1, fs] * u2 + cw_ref[1:2, fs] * u1 + cw_ref[2:3, fs] * u3
        acts.append((_silu(conv) * gate.reshape(bb, tt, tf)).reshape(bb * tt, tf).astype(BF16))
        tails.append(u3[:, tt - 2:tt, :])
        if len(acts) == FF_GROUP or f == nf - 1:
            lo_f = f + 1 - len(acts)
            acc[...] += jnp.dot(jnp.concatenate(acts, axis=1) if len(acts) > 1 else acts[0],
                                wd_ref[lo_f * tf:(f + 1) * tf, :], preferred_element_type=F32)
            acts = []
    tail = jnp.concatenate(tails, axis=-1)
    carry[...] = tail
    st_ref[...] = tail
    o_ref[...] = acc[...].reshape(bb, tt, d)


def _ffn(x3, gain, w_in, conv_w, conv_b, w_down, state_all, layer, bb, tt, tf=256):
    batch, t_len, d = x3.shape
    assert t_len >= CONV_W - 1 and tt >= CONV_W - 1 and D_FF % tf == 0
    const = lambda arr: pl.BlockSpec(arr.shape, lambda b, t: (0, 0), pipeline_mode=pl.Buffered(1))
    layer_w = lambda arr: pl.BlockSpec((None,) + arr.shape[1:], lambda b, t: (layer, 0, 0),
                                       pipeline_mode=pl.Buffered(1))
    return pl.pallas_call(
        functools.partial(_ffn_body, bb=bb, tt=tt, tf=tf),
        grid=(batch // bb, t_len // tt),
        in_specs=[pl.BlockSpec((bb, tt, d), lambda b, t: (b, t, 0)),
                  const(gain), layer_w(w_in), const(conv_w), const(conv_b), layer_w(w_down),
                  pl.BlockSpec((None, bb, CONV_W - 1, D_FF), lambda b, t: (layer, b, 0, 0))],
        out_specs=[pl.BlockSpec((bb, tt, d), lambda b, t: (b, t, 0)),
                   pl.BlockSpec((bb, CONV_W - 1, D_FF), lambda b, t: (b, 0, 0))],
        out_shape=[jax.ShapeDtypeStruct((batch, t_len, d), F32),
                   jax.ShapeDtypeStruct((batch, CONV_W - 1, D_FF), F32)],
        scratch_shapes=[pltpu.VMEM((bb * tt, d), F32), pltpu.VMEM((bb, CONV_W - 1, D_FF), F32)],
        compiler_params=_params(2),
        name="conv_ffn",
    )(x3, gain, w_in, conv_w, conv_b, w_down, state_all)


def _trunk(x3, mem_kv, a_bufs, hg_s, rw_s, rw_shift, ffn_buf, w, prompt):
    batch, t_len, d = x3.shape
    m = batch * t_len
    n_a = w['attn_w_qkv'].shape[0]
    new_a = None
    new_hg, new_rw, new_sh, new_ffn = [], [], [], []
    if prompt:
        xa_bb, xa_tt = 1, 512
        ff_bb, ff_tt = 1, 512
        rw_bb, rw_tt = 1, 512
    else:
        xa_bb, xa_tt = 4, t_len
        ff_bb, ff_tt = batch, t_len
        rw_bb, rw_tt = batch, t_len
    for i in range(DEPTH):
        kind, j = i % N_MIXERS, i // N_MIXERS
        x2 = x3.reshape(m, d)
        g_mix = w['norm_mix'][i][None]
        if kind == 0:
            qkv = _norm_matmul(x2, g_mix, w['attn_w_qkv'][j], w['attn_head_gain'][j], hn_width=A_HEAD_DIM,
                               hn_tiles=(True,) * (2 * N_GROUPS) + (False,) * N_GROUPS, tn=A_WIDTH)
            if prompt:
                o = _attn_prompt(qkv, batch, t_len)
                new_a = tuple(_kv_cache_out(qkv, None if new_a is None else new_a[gi], gi, j, n_a, batch, t_len)
                              for gi in range(N_GROUPS))
            else:
                o, c0, c1, c2 = _attn_sample(qkv, a_bufs, new_a, j, batch, t_len)
                new_a = (c0, c1, c2)
            w_mix = w['attn_w_o'][j]
        elif kind == 1:
            proj = _norm_matmul(x2, g_mix, w['hg_w_in'][j])
            o, st = _hgrn(proj, w['hg_lb_logits'], w['hg_out_gain'][j][None], hg_s[j], i, batch, t_len)
            new_hg.append(st)
            w_mix = w['hg_w_o'][j]
        else:
            proj = _rwkv_proj(x3, w['norm_mix'][i][None], rw_shift[j][:, None, :], w['rw_mu'][j], w['rw_w_rkv'][j],
                              w['rw_w0'][j][None], w['rw_w1'][j], w['rw_w2'][j], w['rw_a0'][j][None],
                              w['rw_a1'][j], w['rw_a2'][j], w['rw_g1'][j], w['rw_g2'][j],
                              w['rw_k_k'][j][None], w['rw_k_a'][j][None], rw_bb, rw_tt)
            o, st = _rwkv_rec(proj[:7], w['rw_r_k'][j].reshape(1, d), w['rw_ln_g'][j][None], w['rw_ln_b'][j][None],
                              rw_s[j], batch, t_len)
            new_rw.append(st)
            new_sh.append(proj[7][:, 0, :])
            w_mix = w['rw_w_o'][j]
        x3 = _xattn(x3, o, w_mix, w['norm_mem'][i][None], w['xa_w_q'], w['xa_q_gain'][i][None], mem_kv, i,
                    w['xa_w_o'], xa_bb, xa_tt)
        x3, fb = _ffn(x3, w['norm_ffn'][i][None], w['ffn_w_in'], w['ffn_conv_w'][i], w['ffn_conv_b'][i][None],
                      w['ffn_w_down'], ffn_buf, i, ff_bb, ff_tt)
        new_ffn.append(fb)
    return (x3, new_a, jnp.stack(new_hg), jnp.stack(new_rw), jnp.stack(new_sh), jnp.stack(new_ffn))


def kernel(x_prompt, x_sample, mem_prompt, cache_attn_kv_w128, cache_attn_kv_w512, cache_attn_kv_w2048, state_hgrn, state_rwkv, state_rwkv_shift, state_ffn_conv, cache_mem_kv, norm_mix, norm_mem, norm_ffn, mem_norm, attn_w_qkv, attn_q_gain, attn_k_gain, attn_w_o, hg_w_in, hg_lb_logits, hg_out_gain, hg_w_o, rw_mu, rw_w_rkv, rw_w0, rw_w1, rw_w2, rw_a0, rw_a1, rw_a2, rw_g1, rw_g2, rw_k_k, rw_k_a, rw_r_k, rw_ln_g, rw_ln_b, rw_w_o, xa_w_q, xa_w_kv, xa_q_gain, xa_k_gain, xa_w_o, ffn_w_in, ffn_conv_w, ffn_conv_b, ffn_w_down):
    d = D_MODEL
    bp = x_prompt.shape[0]
    bs = x_sample.shape[0]
    n_a, n_b, n_c = attn_w_qkv.shape[0], hg_w_in.shape[0], rw_w_rkv.shape[0]
    bf = lambda a: a.astype(BF16)

    tile_h = lambda gn: jnp.broadcast_to(gn[:, :, None, :], (n_a, N_GROUPS, A_HEADS, A_HEAD_DIM)).reshape(n_a, 1, -1)
    head_gain = jnp.concatenate([tile_h(attn_q_gain), tile_h(attn_k_gain),
                                 jnp.ones((n_a, 1, N_GROUPS * A_WIDTH), F32)], axis=-1)

    w = {
        'norm_mix': norm_mix, 'norm_mem': norm_mem, 'norm_ffn': norm_ffn,
        'attn_w_qkv': bf(attn_w_qkv), 'attn_head_gain': head_gain, 'attn_w_o': bf(attn_w_o),
        'hg_w_in': bf(hg_w_in), 'hg_lb_logits': hg_lb_logits, 'hg_out_gain': hg_out_gain, 'hg_w_o': bf(hg_w_o),
        'rw_mu': rw_mu, 'rw_w_rkv': bf(rw_w_rkv), 'rw_w0': rw_w0, 'rw_w1': bf(rw_w1), 'rw_w2': bf(rw_w2),
        'rw_a0': rw_a0, 'rw_a1': bf(rw_a1), 'rw_a2': bf(rw_a2), 'rw_g1': bf(rw_g1), 'rw_g2': bf(rw_g2),
        'rw_k_k': rw_k_k, 'rw_k_a': rw_k_a, 'rw_r_k': rw_r_k, 'rw_ln_g': rw_ln_g, 'rw_ln_b': rw_ln_b,
        'rw_w_o': bf(rw_w_o),
        'xa_w_q': bf(xa_w_q), 'xa_q_gain': xa_q_gain, 'xa_w_o': bf(xa_w_o),
        'ffn_w_in': bf(ffn_w_in), 'ffn_conv_w': ffn_conv_w, 'ffn_conv_b': ffn_conv_b, 'ffn_w_down': bf(ffn_w_down),
    }

    mem_kv_prompt, mem_kv_flat = _memory_kv(mem_prompt.reshape(bp * N_MEM, d), mem_norm[:, None, :], bf(xa_w_kv),
                                            xa_k_gain[:, None, :], bp)

    as_rows = lambda cc: cc.reshape(cc.shape[0], cc.shape[1], cc.shape[2] * cc.shape[3] * cc.shape[4], cc.shape[5])
    as_cache = lambda rr: rr.reshape(rr.shape[0], rr.shape[1], rr.shape[2] // KV_ROWS, 2, A_HEADS, A_HEAD_DIM)
    y_prompt, a_p, hg_p, rw_p, sh_p, ffn_p = _trunk(
        x_prompt, mem_kv_flat, None,
        jnp.zeros((n_b, bp, HG_HEADS, HG_DK, HG_DK), F32),
        jnp.zeros((n_c, bp, RW_HEADS, RW_HEAD_DIM, RW_HEAD_DIM), F32),
        jnp.zeros((n_c, bp, d), F32),
        jnp.zeros((DEPTH, bp, CONV_W - 1, D_FF), F32), w, True)
    y_sample, a_s, hg_s, rw_s, sh_s, ffn_s = _trunk(
        x_sample, cache_mem_kv,
        tuple(as_rows(cc) for cc in (cache_attn_kv_w128, cache_attn_kv_w512, cache_attn_kv_w2048)),
        state_hgrn, state_rwkv, state_rwkv_shift, state_ffn_conv, w, False)
    a_p = tuple(as_cache(rr) for rr in a_p)
    a_s = tuple(as_cache(rr) for rr in a_s)
    return (y_prompt, y_sample, a_p[0], a_p[1], a_p[2], hg_p, rw_p, sh_p, ffn_p, mem_kv_prompt,
            a_s[0], a_s[1], a_s[2], hg_s, rw_s, sh_s, ffn_s)
```
